```python
import jax, jax.numpy as jnp
from jax import lax
import numpy as np

D_MODEL = 1024
BATCH = 4
SEQ = 8192
DEPTH = 1

GRID_W = 64
NA_HEAD_DIM = 64
NA_WIDTH = D_MODEL // 2
NA_HEADS = NA_WIDTH // NA_HEAD_DIM
NA_WIN_H = 8
NA_WIN_W = 16
GLA_HEADS = 4
GLA_KEY_WIDTH = D_MODEL // 4
GLA_VAL_WIDTH = D_MODEL // 2
GLA_DK = GLA_KEY_WIDTH // GLA_HEADS
GLA_DV = GLA_VAL_WIDTH // GLA_HEADS
GLA_GATE_RANK = 16
GLA_GATE_TAU = 16.0
GLA_CHUNK = 64
D_FF = 2816
ALPHA = (2 * DEPTH) ** 0.25
BETA = (8 * DEPTH) ** -0.25
LN_EPS = 1e-5
RMS_EPS = 1e-6
IN_SIZES = (NA_WIDTH, NA_WIDTH, NA_WIDTH,
            GLA_KEY_WIDTH, GLA_KEY_WIDTH,
            GLA_VAL_WIDTH, GLA_VAL_WIDTH,
            GLA_GATE_RANK, GLA_GATE_RANK,
            D_MODEL, D_MODEL)
N_IN = 3 * NA_WIDTH + 2 * GLA_KEY_WIDTH + 2 * GLA_VAL_WIDTH + 2 * GLA_GATE_RANK + 2 * D_MODEL

kernel_name = "hybrid_na_gla_macaron_deepnorm"


def layer_norm(x, g, b):
    xf = x.astype(jnp.float32)
    mu = jnp.mean(xf, axis=-1, keepdims=True)
    var = jnp.mean(jnp.square(xf - mu), axis=-1, keepdims=True)
    y = (xf - mu) * lax.rsqrt(var + LN_EPS)
    return (y * g + b).astype(x.dtype)


def swiglu_ffn(x, w_gate, w_up, w_down):
    return (jax.nn.silu(x @ w_gate) * (x @ w_up)) @ w_down


def split_points():
    pts, acc = [], 0
    for sz in IN_SIZES[:-1]:
        acc += sz
        pts.append(acc)
    return pts


def neighborhood_attention(q, k, v, rpb):
    b, s, h, dh = q.shape
    rows = s // GRID_W
    kh = min(NA_WIN_H, rows)
    kw = NA_WIN_W
    grid = lambda t: t.reshape(b, rows, GRID_W, h, dh).transpose(0, 3, 1, 2, 4)
    qg = grid(q) * (dh ** -0.5)
    kg, vg = grid(k), grid(v)
    cols = jnp.arange(GRID_W)
    col_start = jnp.clip(cols - kw // 2, 0, GRID_W - kw)
    col_idx = col_start[:, None] + jnp.arange(kw)[None, :]
    dc_idx = col_idx - cols[:, None] + (kw - 1)
    rpb_c = rpb[:, :, dc_idx]

    def row_block(args):
        r, q_r = args
        r_start = jnp.clip(r - kh // 2, 0, rows - kh)
        k_band = lax.dynamic_slice_in_dim(kg, r_start, kh, axis=2)
        v_band = lax.dynamic_slice_in_dim(vg, r_start, kh, axis=2)
        k_sel = k_band[:, :, :, col_idx]
        v_sel = v_band[:, :, :, col_idx]
        dr_idx = r_start + jnp.arange(kh) - r + (NA_WIN_H - 1)
        bias = jnp.take(rpb_c, dr_idx, axis=1).transpose(0, 2, 1, 3)
        scores = jnp.einsum('bhqd,bhiqjd->bhqij', q_r, k_sel).astype(jnp.float32)
        scores = scores + bias[None].astype(jnp.float32)
        p = jax.nn.softmax(scores.reshape(b, h, GRID_W, kh * kw), axis=-1)
        p = p.reshape(b, h, GRID_W, kh, kw).astype(v.dtype)
        return jnp.einsum('bhqij,bhiqjd->bhqd', p, v_sel)

    out = lax.map(row_block, (jnp.arange(rows), qg.transpose(2, 0, 1, 3, 4)))
    return out.transpose(1, 0, 3, 2, 4).reshape(b, s, h * dh)


def gla_direction(q, k, v, log_a, include_diag):
    c = q.shape[3]
    b_cum = jnp.cumsum(log_a, axis=3)
    b_last = b_cum[:, :, :, -1:, :]
    q_t = q * jnp.exp(b_cum)
    k_t = k * jnp.exp(-b_cum)
    mask = jnp.tril(jnp.ones((c, c), dtype=bool), k=0 if include_diag else -1)
    attn = jnp.where(mask, jnp.einsum('bhncd,bhnsd->bhncs', q_t, k_t), 0.0)
    o_intra = jnp.einsum('bhncs,bhnse->bhnce', attn, v)
    k_end = k * jnp.exp(b_last - b_cum)
    state_add = jnp.einsum('bhncd,bhnce->bhnde', k_end, v)
    chunk_decay = jnp.exp(b_last[:, :, :, 0, :])

    def step(state, inp):
        dec, add = inp
        return dec[..., None] * state + add, state

    init = jnp.zeros(q.shape[:2] + (q.shape[-1], v.shape[-1]), q.dtype)
    _, states = lax.scan(step, init, (jnp.moveaxis(chunk_decay, 2, 0), jnp.moveaxis(state_add, 2, 0)))
    states = jnp.moveaxis(states, 0, 2)
    o_inter = jnp.einsum('bhncd,bhnde->bhnce', q_t, states)
    return o_intra + o_inter


def gla_bidirectional(q, k, v, g_out, dec_lr_f, dec_lr_b, w_dec2, b_dec, norm_g):
    bsz, s, _ = q.shape
    n = s // GLA_CHUNK
    f32 = jnp.float32

    def chunks(t, d):
        return t.astype(f32).reshape(bsz, n, GLA_CHUNK, GLA_HEADS, d).transpose(0, 3, 1, 2, 4)

    def unchunk(t):
        return t.transpose(0, 2, 3, 1, 4).reshape(bsz, s, GLA_HEADS, GLA_DV)

    qs = q * (GLA_DK ** -0.5)
    log_a_f = jax.nn.log_sigmoid((dec_lr_f @ w_dec2[0] + b_dec[0]).astype(f32)) / GLA_GATE_TAU
    log_a_b = jax.nn.log_sigmoid((dec_lr_b @ w_dec2[1] + b_dec[1]).astype(f32)) / GLA_GATE_TAU
    o_f = gla_direction(chunks(qs, GLA_DK), chunks(k, GLA_DK), chunks(v, GLA_DV),
                        chunks(log_a_f, GLA_DK), True)
    flip = lambda t: jnp.flip(t, axis=1)
    o_b = gla_direction(chunks(flip(qs), GLA_DK), chunks(flip(k), GLA_DK), chunks(flip(v), GLA_DV),
                        chunks(flip(log_a_b), GLA_DK), False)
    o = unchunk(o_f) + jnp.flip(unchunk(o_b), axis=1)
    o = o * lax.rsqrt(jnp.mean(jnp.square(o), axis=-1, keepdims=True) + RMS_EPS) * norm_g
    o = o * jax.nn.silu(g_out.astype(f32).reshape(bsz, s, GLA_HEADS, GLA_DV))
    return o.reshape(bsz, s, GLA_VAL_WIDTH).astype(q.dtype)


def hybrid_mixer(h, w_in, na_rpb, gla_w_dec2, gla_b_dec, gla_norm_g, w_branch_na, w_branch_gla, w_out):
    b, s, _ = h.shape
    proj = h @ w_in
    na_q, na_k, na_v, g_q, g_k, g_v, g_g, dlf, dlb, gate_na, gate_gla = jnp.split(proj, split_points(), axis=-1)
    heads = lambda t: t.reshape(b, s, NA_HEADS, NA_HEAD_DIM)
    y_na = neighborhood_attention(heads(na_q), heads(na_k), heads(na_v), na_rpb) @ w_branch_na
    y_gla = gla_bidirectional(g_q, g_k, g_v, g_g, dlf, dlb, gla_w_dec2, gla_b_dec, gla_norm_g) @ w_branch_gla
    merged = jax.nn.sigmoid(gate_na) * y_na + jax.nn.sigmoid(gate_gla) * y_gla
    return merged @ w_out


def setup_inputs(seed: int = 0) -> dict:
    key = jax.random.key(seed)
    ks = jax.random.split(key, 24)
    L, D, F = DEPTH, D_MODEL, D_FF
    nrm = lambda k, shape, scale: jax.random.normal(k, shape, jnp.float32) * scale
    col_scale = jnp.concatenate([jnp.full((sz,), BETA if i in (2, 5) else 1.0, jnp.float32)
                                 for i, sz in enumerate(IN_SIZES)])
    return {
        "x": nrm(ks[0], (BATCH, SEQ, D), 1.0),
        "ffn1_w_gate": nrm(ks[1], (L, D, F), D ** -0.5),
        "ffn1_w_up": nrm(ks[2], (L, D, F), D ** -0.5),
        "ffn1_w_down": nrm(ks[3], (L, F, D), BETA * F ** -0.5),
        "ln1_g": 1.0 + nrm(ks[4], (L, D), 0.02),
        "ln1_b": nrm(ks[5], (L, D), 0.02),
        "w_in": nrm(ks[6], (L, D, N_IN), D ** -0.5) * col_scale,
        "na_rpb": nrm(ks[7], (L, NA_HEADS, 2 * NA_WIN_H - 1, 2 * NA_WIN_W - 1), 0.02),
        "gla_w_dec2": nrm(ks[8], (L, 2, GLA_GATE_RANK, GLA_KEY_WIDTH), GLA_GATE_RANK ** -0.5),
        "gla_b_dec": nrm(ks[9], (L, 2, GLA_KEY_WIDTH), 0.01),
        "gla_norm_g": 1.0 + nrm(ks[10], (L, GLA_DV), 0.02),
        "w_branch_na": nrm(ks[11], (L, NA_WIDTH, D), NA_WIDTH ** -0.5),
        "w_branch_gla": nrm(ks[12], (L, GLA_VAL_WIDTH, D), GLA_VAL_WIDTH ** -0.5),
        "w_out": nrm(ks[13], (L, D, D), BETA * D ** -0.5),
        "ln2_g": 1.0 + nrm(ks[14], (L, D), 0.02),
        "ln2_b": nrm(ks[15], (L, D), 0.02),
        "ffn2_w_gate": nrm(ks[16], (L, D, F), D ** -0.5),
        "ffn2_w_up": nrm(ks[17], (L, D, F), D ** -0.5),
        "ffn2_w_down": nrm(ks[18], (L, F, D), BETA * F ** -0.5),
        "ln3_g": 1.0 + nrm(ks[19], (L, D), 0.02),
        "ln3_b": nrm(ks[20], (L, D), 0.02),
    }


def reference(x, ffn1_w_gate, ffn1_w_up, ffn1_w_down, ln1_g, ln1_b, w_in, na_rpb, gla_w_dec2,
              gla_b_dec, gla_norm_g, w_branch_na, w_branch_gla, w_out, ln2_g, ln2_b,
              ffn2_w_gate, ffn2_w_up, ffn2_w_down, ln3_g, ln3_b):
    for l in range(DEPTH):
        x = layer_norm(ALPHA * x + 0.5 * swiglu_ffn(x, ffn1_w_gate[l], ffn1_w_up[l], ffn1_w_down[l]),
                       ln1_g[l], ln1_b[l])
        x = layer_norm(ALPHA * x + hybrid_mixer(x, w_in[l], na_rpb[l], gla_w_dec2[l], gla_b_dec[l],
                                                gla_norm_g[l], w_branch_na[l], w_branch_gla[l], w_out[l]),
                       ln2_g[l], ln2_b[l])
        x = layer_norm(ALPHA * x + 0.5 * swiglu_ffn(x, ffn2_w_gate[l], ffn2_w_up[l], ffn2_w_down[l]),
                       ln3_g[l], ln3_b[l])
    return x
```

```python
import functools

import jax
import jax.numpy as jnp
from jax import lax
from jax.experimental import pallas as pl
from jax.experimental.pallas import tpu as pltpu

F32 = jnp.float32
BF16 = jnp.bfloat16

D_MODEL = 1024
D_FF = 2816
GRID_W = 64
NA_HEADS = 8
NA_HEAD_DIM = 64
NA_WIDTH = NA_HEADS * NA_HEAD_DIM
NA_WIN_H = 8
NA_WIN_W = 16
GLA_HEADS = 4
GLA_DK = 64
GLA_DV = 128
GLA_KEY_WIDTH = GLA_HEADS * GLA_DK
GLA_VAL_WIDTH = GLA_HEADS * GLA_DV
GLA_GATE_RANK = 16
GLA_GATE_TAU = 16.0
GLA_CHUNK = 64
DEPTH = 1
ALPHA = (2 * DEPTH) ** 0.25
LN_EPS = 1e-5
RMS_EPS = 1e-6

OFF_NA_END = 3 * NA_WIDTH
OFF_GQK_END = OFF_NA_END + 2 * GLA_KEY_WIDTH
OFF_GVG_END = OFF_GQK_END + 2 * GLA_VAL_WIDTH
OFF_GDL_END = OFF_GVG_END + 2 * GLA_GATE_RANK
N_IN = OFF_GDL_END + 2 * D_MODEL

LANES = 128
VMEM_LIMIT_BYTES = 56 * 1024 * 1024

NA_RQ = 4
NA_BAND = NA_RQ + NA_WIN_H
NA_QT = NA_RQ * GRID_W
NA_KT = NA_BAND * GRID_W
NEG_BIG = -1e30

GLA_UNROLL = 4
TOKEN_TILE = 256


def _const_spec(shape):
    nd = len(shape)
    return pl.BlockSpec(shape, lambda *_: (0,) * nd, pipeline_mode=pl.Buffered(1))


def _layer_norm(z, g, b):
    mu = jnp.mean(z, axis=-1, keepdims=True)
    zc = z - mu
    var = jnp.mean(zc * zc, axis=-1, keepdims=True)
    return zc * lax.rsqrt(var + LN_EPS) * g + b


def _swiglu(xb, wg_ref, wu_ref, wd_ref):
    g = jnp.dot(xb, wg_ref[...], preferred_element_type=F32)
    u = jnp.dot(xb, wu_ref[...], preferred_element_type=F32)
    h = (g * jax.nn.sigmoid(g)) * u
    return jnp.dot(h.astype(BF16), wd_ref[...], preferred_element_type=F32)


def _ffn_inproj_body(x_ref, wg_ref, wu_ref, wd_ref, g1_ref, b1_ref, win_ref,
                     x1_ref, naqkv_ref, gqk_ref, gvg_ref, gdl_ref):
    x = x_ref[...]
    y = _swiglu(x.astype(BF16), wg_ref, wu_ref, wd_ref)
    x1 = _layer_norm(ALPHA * x + 0.5 * y, g1_ref[...], b1_ref[...])
    x1_ref[...] = x1
    proj = jnp.dot(x1.astype(BF16), win_ref[...], preferred_element_type=F32)
    naqkv_ref[...] = proj[:, :OFF_NA_END].astype(BF16)
    gqk_ref[...] = proj[:, OFF_NA_END:OFF_GQK_END].astype(BF16)
    gvg_ref[...] = proj[:, OFF_GQK_END:OFF_GVG_END].astype(BF16)
    gdl_ref[...] = proj[:, OFF_GVG_END:OFF_GDL_END].astype(BF16)


def _ffn_inproj(x2d, wg, wu, wd, g1, b1, win_a):
    t = x2d.shape[0]
    tm = min(TOKEN_TILE, t)
    row = lambda w: pl.BlockSpec((tm, w), lambda i: (i, 0))
    widths = (OFF_NA_END, OFF_GQK_END - OFF_NA_END, OFF_GVG_END - OFF_GQK_END, OFF_GDL_END - OFF_GVG_END)
    return pl.pallas_call(
        _ffn_inproj_body,
        grid=(t // tm,),
        in_specs=[row(D_MODEL), _const_spec(wg.shape), _const_spec(wu.shape), _const_spec(wd.shape),
                  _const_spec(g1.shape), _const_spec(b1.shape), _const_spec(win_a.shape)],
        out_specs=[row(D_MODEL)] + [row(w) for w in widths],
        out_shape=[jax.ShapeDtypeStruct((t, D_MODEL), F32)]
                  + [jax.ShapeDtypeStruct((t, w), BF16) for w in widths],
        compiler_params=pltpu.CompilerParams(dimension_semantics=("arbitrary",),
                                             vmem_limit_bytes=VMEM_LIMIT_BYTES),
        name="ffn1_ln1_inproj",
    )(x2d, wg, wu, wd, g1, b1, win_a)


def _na_bias_tables(rpb, rows):
    kh, kw = NA_WIN_H, NA_WIN_W
    ri = jnp.arange(NA_RQ)[:, None, None, None]
    c = jnp.arange(GRID_W)[None, :, None, None]
    bj = jnp.arange(NA_BAND)[None, None, :, None]
    kc = jnp.arange(GRID_W)[None, None, None, :]
    c_start = jnp.clip(c - kw // 2, 0, GRID_W - kw)
    col_ok = (kc >= c_start) & (kc < c_start + kw)
    dc = jnp.clip(kc - c + (kw - 1), 0, 2 * kw - 2)
    tables = []
    for r0, b0 in ((0, 0), (NA_RQ, 0), (rows - NA_RQ, rows - NA_BAND)):
        r = r0 + ri
        kr = b0 + bj
        r_start = jnp.clip(r - kh // 2, 0, rows - kh)
        row_ok = (kr >= r_start) & (kr < r_start + kh)
        dr = jnp.clip(kr - r + (kh - 1), 0, 2 * kh - 2)
        bias = rpb[:, dr, dc]
        bias = jnp.where((row_ok & col_ok)[None], bias, NEG_BIG)
        tables.append(bias.reshape(NA_HEADS, NA_QT, NA_KT))
    return jnp.stack(tables).astype(F32)


def _na_body(q_ref, k_ref, v_ref, bias_ref, o_ref, *, rows):
    nblk = rows // NA_RQ
    lane = lax.broadcasted_iota(jnp.int32, (1, LANES), 1)
    head_lanes = (lane < NA_HEAD_DIM, lane >= NA_HEAD_DIM)

    def block(i, carry):
        q0 = pl.multiple_of(i * NA_QT, NA_QT)
        band0 = jnp.clip(NA_RQ * i - NA_WIN_H // 2, 0, rows - NA_BAND)
        k0 = pl.multiple_of(band0 * GRID_W, GRID_W)
        kind = jnp.where(i == 0, 0, jnp.where(i == nblk - 1, 2, 1))
        q = q_ref[0, pl.ds(q0, NA_QT), :] * jnp.asarray(NA_HEAD_DIM ** -0.5, BF16)
        k = k_ref[0, pl.ds(k0, NA_KT), :]
        v = v_ref[0, pl.ds(k0, NA_KT), :]
        out = None
        for hh in range(2):
            kh = jnp.where(head_lanes[hh], k, jnp.zeros_like(k))
            vh = jnp.where(head_lanes[hh], v, jnp.zeros_like(v))
            s = lax.dot_general(q, kh, (((1,), (1,)), ((), ())), preferred_element_type=F32)
            s = s + bias_ref[kind, hh]
            mx = jnp.max(s, axis=-1, keepdims=True)
            p = jnp.exp(s - mx)
            den = jnp.sum(p, axis=-1, keepdims=True)
            o = jnp.dot(p.astype(BF16), vh, preferred_element_type=F32) * (1.0 / den)
            out = o if out is None else out + o
        o_ref[0, pl.ds(q0, NA_QT), :] = out.astype(BF16)
        return carry

    lax.fori_loop(0, nblk, block, 0)


def _neighborhood_attention(naqkv, bias, rows):
    b, s, _ = naqkv.shape
    npairs = NA_HEADS // 2
    tok = lambda off: pl.BlockSpec((1, s, LANES), lambda bi, p: (bi, 0, off + p))
    return pl.pallas_call(
        functools.partial(_na_body, rows=rows),
        grid=(b, npairs),
        in_specs=[tok(0), tok(npairs), tok(2 * npairs),
                  pl.BlockSpec((3, 2, NA_QT, NA_KT), lambda bi, p: (0, p, 0, 0))],
        out_specs=tok(0),
        out_shape=jax.ShapeDtypeStruct((b, s, NA_WIDTH), BF16),
        compiler_params=pltpu.CompilerParams(dimension_semantics=("arbitrary", "arbitrary"),
                                             vmem_limit_bytes=VMEM_LIMIT_BYTES),
        name="neighborhood_attention",
    )(naqkv, naqkv, naqkv, bias)


def _split3(x):
    hi = x.astype(BF16)
    r1 = x - hi.astype(F32)
    mid = r1.astype(BF16)
    lo = (r1 - mid.astype(F32)).astype(BF16)
    return hi, mid, lo


def _gla_chunk(qk_ref, vg_ref, dl_ref, wdec_ref, bdec_ref, st_ref, c0, fwd):
    C, KW = GLA_CHUNK, GLA_KEY_WIDTH
    d = 0 if fwd else 1
    rows = pl.ds(c0, C)
    q = qk_ref[0, rows, 0:KW].astype(F32)
    k = qk_ref[0, rows, KW:2 * KW].astype(F32)
    v = vg_ref[0, rows, 0:GLA_VAL_WIDTH]
    z = jnp.dot(dl_ref[0, rows, :], wdec_ref[d], preferred_element_type=F32) + bdec_ref[d:d + 1, :]
    log_a = (jnp.minimum(z, 0.0) - jnp.log1p(jnp.exp(-jnp.abs(z)))) * (1.0 / GLA_GATE_TAU)

    ci = lax.broadcasted_iota(jnp.int32, (C, C), 0)
    si = lax.broadcasted_iota(jnp.int32, (C, C), 1)
    tri = jnp.where((si <= ci) if fwd else (si >= ci), 1.0, 0.0).astype(BF16)
    pieces = jnp.concatenate(_split3(log_a), axis=1)
    bc = jnp.dot(tri, pieces, preferred_element_type=F32)
    b_cum = (bc[:, 0:KW] + bc[:, KW:2 * KW]) + bc[:, 2 * KW:3 * KW]
    b_last = b_cum[C - 1:C, :] if fwd else b_cum[0:1, :]

    q_t = ((q * (GLA_DK ** -0.5)) * jnp.exp(b_cum)).astype(BF16)
    k_t = (k * jnp.exp(-b_cum)).astype(BF16)
    k_end = (k * jnp.exp(b_last - b_cum)).astype(BF16)

    rh = lax.broadcasted_iota(jnp.int32, (GLA_HEADS * C, KW), 0) // C
    lh = lax.broadcasted_iota(jnp.int32, (GLA_HEADS * C, KW), 1) // GLA_DK
    same_head = rh == lh
    q_bd = jnp.where(same_head, jnp.concatenate([q_t] * GLA_HEADS, axis=0), jnp.zeros((), BF16))
    kend_bd = jnp.where(same_head, jnp.concatenate([k_end] * GLA_HEADS, axis=0), jnp.zeros((), BF16))

    attn = lax.dot_general(q_bd, k_t, (((1,), (1,)), ((), ())), preferred_element_type=F32)
    cc = lax.broadcasted_iota(jnp.int32, (GLA_HEADS * C, C), 0) % C
    ss = lax.broadcasted_iota(jnp.int32, (GLA_HEADS * C, C), 1)
    attn = jnp.where((ss <= cc) if fwd else (ss > cc), attn, 0.0).astype(BF16)
    v_heads = [v[:, h * GLA_DV:(h + 1) * GLA_DV] for h in range(GLA_HEADS)]
    o_intra = jnp.concatenate(
        [jnp.dot(attn[h * C:(h + 1) * C], v_heads[h], preferred_element_type=F32) for h in range(GLA_HEADS)],
        axis=0)
    st = st_ref[...]
    o_inter = lax.dot_general(q_bd, st.astype(BF16), (((1,), (1,)), ((), ())), preferred_element_type=F32)

    v_stack = jnp.concatenate(v_heads, axis=0)
    st_add = lax.dot_general(v_stack, kend_bd, (((0,), (0,)), ((), ())), preferred_element_type=F32)
    st_ref[...] = jnp.exp(b_last) * st + st_add
    return o_intra + o_inter


def _gla_body(qk_ref, vg_ref, dl_ref, wdec_ref, bdec_ref, ng_ref, o_ref, st_ref, ob_ref, *, cbk, nbk):
    C = GLA_CHUNK
    ph = pl.program_id(1)
    j = pl.program_id(2)
    chunks_per_block = cbk // C
    unroll = min(GLA_UNROLL, chunks_per_block)
    ngroups = chunks_per_block // unroll

    @pl.when(j == 0)
    def _():
        st_ref[...] = jnp.zeros_like(st_ref)

    def sweep(fwd):
        blk = j if fwd else nbk - 1 - j

        def group(gi, carry):
            g0 = gi if fwd else ngroups - 1 - gi
            for u in range(unroll):
                uu = u if fwd else unroll - 1 - u
                cl = g0 * unroll + uu
                c0 = pl.multiple_of(cl * C, C)
                cg = blk * chunks_per_block + cl
                o = _gla_chunk(qk_ref, vg_ref, dl_ref, wdec_ref, bdec_ref, st_ref, c0, fwd)
                if not fwd:
                    ob_ref[cg] = o
                else:
                    o = o + ob_ref[cg]
                    o = o * lax.rsqrt(jnp.mean(o * o, axis=-1, keepdims=True) + RMS_EPS) * ng_ref[...]
                    gate = vg_ref[0, pl.ds(c0, C), GLA_VAL_WIDTH:2 * GLA_VAL_WIDTH].astype(F32)
                    gate = jnp.concatenate(
                        [gate[:, h * GLA_DV:(h + 1) * GLA_DV] for h in range(GLA_HEADS)], axis=0)
                    o = o * (gate * jax.nn.sigmoid(gate))
                    o_ref[0, pl.ds(c0, C), :] = jnp.concatenate(
                        [o[h * C:(h + 1) * C] for h in range(GLA_HEADS)], axis=1).astype(BF16)
            return carry

        lax.fori_loop(0, ngroups, group, 0)

    @pl.when(ph == 0)
    def _():
        sweep(False)

    @pl.when(ph == 1)
    def _():
        sweep(True)


def _gla(gqk, gvg, gdl, wdec_pad, bdec, norm_g):
    b, s, _ = gqk.shape
    cbk = min(1024, s)
    nbk = s // cbk
    blk = lambda w: pl.BlockSpec((1, cbk, w), lambda bi, ph, j: (bi, jnp.where(ph == 0, nbk - 1 - j, j), 0))
    return pl.pallas_call(
        functools.partial(_gla_body, cbk=cbk, nbk=nbk),
        grid=(b, 2, nbk),
        in_specs=[blk(2 * GLA_KEY_WIDTH), blk(2 * GLA_VAL_WIDTH), blk(2 * GLA_GATE_RANK),
                  _const_spec(wdec_pad.shape), _const_spec(bdec.shape), _const_spec(norm_g.shape)],
        out_specs=pl.BlockSpec((1, cbk, GLA_VAL_WIDTH), lambda bi, ph, j: (bi, jnp.where(ph == 0, 0, j), 0)),
        out_shape=jax.ShapeDtypeStruct((b, s, GLA_VAL_WIDTH), BF16),
        scratch_shapes=[pltpu.VMEM((GLA_DV, GLA_KEY_WIDTH), F32),
                        pltpu.VMEM((s // GLA_CHUNK, GLA_HEADS * GLA_CHUNK, GLA_DV), F32)],
        compiler_params=pltpu.CompilerParams(dimension_semantics=("arbitrary", "arbitrary", "arbitrary"),
                                             vmem_limit_bytes=VMEM_LIMIT_BYTES),
        name="gla_bidirectional",
    )(gqk, gvg, gdl, wdec_pad, bdec, norm_g)


def _merge_ffn_body(x1_ref, na_ref, gla_ref, wgate_ref, wbn_ref, wbg_ref, wo_ref, g2_ref, b2_ref,
                    wg_ref, wu_ref, wd_ref, g3_ref, b3_ref, o_ref):
    x1 = x1_ref[...]
    gates = jnp.dot(x1.astype(BF16), wgate_ref[...], preferred_element_type=F32)
    y_na = jnp.dot(na_ref[...], wbn_ref[...], preferred_element_type=F32)
    y_gla = jnp.dot(gla_ref[...], wbg_ref[...], preferred_element_type=F32)
    merged = jax.nn.sigmoid(gates[:, :D_MODEL]) * y_na + jax.nn.sigmoid(gates[:, D_MODEL:]) * y_gla
    mix = jnp.dot(merged.astype(BF16), wo_ref[...], preferred_element_type=F32)
    x2 = _layer_norm(ALPHA * x1 + mix, g2_ref[...], b2_ref[...])
    y = _swiglu(x2.astype(BF16), wg_ref, wu_ref, wd_ref)
    o_ref[...] = _layer_norm(ALPHA * x2 + 0.5 * y, g3_ref[...], b3_ref[...])


def _merge_ffn(x1, na_o, gla_o, wgate, wbn, wbg, wo, g2, b2, wg, wu, wd, g3, b3):
    t = x1.shape[0]
    tm = min(TOKEN_TILE, t)
    row = lambda w: pl.BlockSpec((tm, w), lambda i: (i, 0))
    consts = (wgate, wbn, wbg, wo, g2, b2, wg, wu, wd, g3, b3)
    return pl.pallas_call(
        _merge_ffn_body,
        grid=(t // tm,),
        in_specs=[row(D_MODEL), row(NA_WIDTH), row(GLA_VAL_WIDTH)] + [_const_spec(c.shape) for c in consts],
        out_specs=row(D_MODEL),
        out_shape=jax.ShapeDtypeStruct((t, D_MODEL), F32),
        compiler_params=pltpu.CompilerParams(dimension_semantics=("arbitrary",),
                                             vmem_limit_bytes=VMEM_LIMIT_BYTES),
        name="merge_ln2_ffn2_ln3",
    )(x1, na_o, gla_o, *consts)


def _layer(x, ffn1_w_gate, ffn1_w_up, ffn1_w_down, ln1_g, ln1_b, w_in, na_rpb, gla_w_dec2, gla_b_dec,
           gla_norm_g, w_branch_na, w_branch_gla, w_out, ln2_g, ln2_b, ffn2_w_gate, ffn2_w_up, ffn2_w_down,
           ln3_g, ln3_b):
    b, s, d = x.shape
    assert d == D_MODEL and s % (GRID_W * NA_RQ) == 0 and s // GRID_W >= NA_BAND
    rows = s // GRID_W
    bf = lambda w: w.astype(BF16)
    vec = lambda p: p.reshape(1, -1)

    x1, naqkv, gqk, gvg, gdl = _ffn_inproj(
        x.reshape(b * s, d), bf(ffn1_w_gate), bf(ffn1_w_up), bf(ffn1_w_down), vec(ln1_g), vec(ln1_b),
        bf(w_in[:, :OFF_GDL_END]))

    na_o = _neighborhood_attention(naqkv.reshape(b, s, -1), _na_bias_tables(na_rpb, rows), rows)

    zr = jnp.zeros((GLA_GATE_RANK, GLA_KEY_WIDTH), gla_w_dec2.dtype)
    wdec_pad = bf(jnp.stack([jnp.concatenate([gla_w_dec2[0], zr]), jnp.concatenate([zr, gla_w_dec2[1]])]))
    gla_o = _gla(gqk.reshape(b, s, -1), gvg.reshape(b, s, -1), gdl.reshape(b, s, -1),
                 wdec_pad, gla_b_dec, vec(gla_norm_g))

    out = _merge_ffn(x1, na_o.reshape(b * s, -1), gla_o.reshape(b * s, -1), bf(w_in[:, OFF_GDL_END:]),
                     bf(w_branch_na), bf(w_branch_gla), bf(w_out), vec(ln2_g), vec(ln2_b),
                     bf(ffn2_w_gate), bf(ffn2_w_up), bf(ffn2_w_down), vec(ln3_g), vec(ln3_b))
    return out.reshape(b, s, d)


def kernel(x, ffn1_w_gate, ffn1_w_up, ffn1_w_down, ln1_g, ln1_b, w_in, na_rpb, gla_w_dec2, gla_b_dec, gla_norm_g,
           w_branch_na, w_branch_gla, w_out, ln2_g, ln2_b, ffn2_w_gate, ffn2_w_up, ffn2_w_down, ln3_g, ln3_b):
    params = (ffn1_w_gate, ffn1_w_up, ffn1_w_down, ln1_g, ln1_b, w_in, na_rpb, gla_w_dec2, gla_b_dec, gla_norm_g,
              w_branch_na, w_branch_gla, w_out, ln2_g, ln2_b, ffn2_w_gate, ffn2_w_up, ffn2_w_down, ln3_g, ln3_b)
    for l in range(DEPTH):
        x = _layer(x, *(p[l] for p in params))
    return x
```

```python
import functools

import jax
import jax.numpy as jnp
import numpy as np
from jax import lax
from jax.experimental import pallas as pl
from jax.experimental.pallas import tpu as pltpu

F32 = jnp.float32
BF16 = jnp.bfloat16

D_MODEL = 1024
D_FF = 2816
GRID_W = 64
NA_HEADS = 8
NA_HEAD_DIM = 64
NA_WIDTH = NA_HEADS * NA_HEAD_DIM
NA_WIN_H = 8
NA_WIN_W = 16
GLA_HEADS = 4
GLA_DK = 64
GLA_DV = 128
GLA_KEY_WIDTH = GLA_HEADS * GLA_DK
GLA_VAL_WIDTH = GLA_HEADS * GLA_DV
GLA_GATE_RANK = 16
GLA_GATE_TAU = 16.0
GLA_CHUNK = 64
DEPTH = 1
ALPHA = (2 * DEPTH) ** 0.25
LN_EPS = 1e-5
RMS_EPS = 1e-6

OFF_NA_END = 3 * NA_WIDTH
OFF_GQK_END = OFF_NA_END + 2 * GLA_KEY_WIDTH
OFF_GVG_END = OFF_GQK_END + 2 * GLA_VAL_WIDTH
OFF_GDL_END = OFF_GVG_END + 2 * GLA_GATE_RANK
N_IN = OFF_GDL_END + 2 * D_MODEL

LANES = 128
VMEM_LIMIT_BYTES = 56 * 1024 * 1024

NA_RQ = 4
NA_BAND = NA_RQ + NA_WIN_H
NA_QT = NA_RQ * GRID_W
NA_KT = NA_BAND * GRID_W
NEG_BIG = -1e30

GLA_UNROLL = 4
TOKEN_TILE = 256


def _const_spec(shape):
    nd = len(shape)
    return pl.BlockSpec(shape, lambda *_: (0,) * nd, pipeline_mode=pl.Buffered(1))


def _layer_norm(z, g, b):
    mu = jnp.mean(z, axis=-1, keepdims=True)
    zc = z - mu
    var = jnp.mean(zc * zc, axis=-1, keepdims=True)
    return zc * lax.rsqrt(var + LN_EPS) * g + b


def _swiglu(xb, wg_ref, wu_ref, wd_ref):
    g = jnp.dot(xb, wg_ref[...], preferred_element_type=F32)
    u = jnp.dot(xb, wu_ref[...], preferred_element_type=F32)
    h = (g * jax.nn.sigmoid(g)) * u
    return jnp.dot(h.astype(BF16), wd_ref[...], preferred_element_type=F32)


def _ffn_inproj_body(x_ref, wg_ref, wu_ref, wd_ref, g1_ref, b1_ref, win_ref,
                     x1_ref, naqkv_ref, gqk_ref, gvg_ref, gdl_ref):
    x = x_ref[...]
    y = _swiglu(x.astype(BF16), wg_ref, wu_ref, wd_ref)
    x1 = _layer_norm(ALPHA * x + 0.5 * y, g1_ref[...], b1_ref[...])
    x1_ref[...] = x1
    proj = jnp.dot(x1.astype(BF16), win_ref[...], preferred_element_type=F32)
    naqkv_ref[...] = proj[:, :OFF_NA_END].astype(BF16)
    gqk_ref[...] = proj[:, OFF_NA_END:OFF_GQK_END].astype(BF16)
    gvg_ref[...] = proj[:, OFF_GQK_END:OFF_GVG_END].astype(BF16)
    gdl_ref[...] = proj[:, OFF_GVG_END:OFF_GDL_END].astype(BF16)


def _ffn_inproj(x2d, wg, wu, wd, g1, b1, win_a):
    t = x2d.shape[0]
    tm = min(TOKEN_TILE, t)
    row = lambda w: pl.BlockSpec((tm, w), lambda i: (i, 0))
    widths = (OFF_NA_END, OFF_GQK_END - OFF_NA_END, OFF_GVG_END - OFF_GQK_END, OFF_GDL_END - OFF_GVG_END)
    return pl.pallas_call(
        _ffn_inproj_body,
        grid=(t // tm,),
        in_specs=[row(D_MODEL), _const_spec(wg.shape), _const_spec(wu.shape), _const_spec(wd.shape),
                  _const_spec(g1.shape), _const_spec(b1.shape), _const_spec(win_a.shape)],
        out_specs=[row(D_MODEL)] + [row(w) for w in widths],
        out_shape=[jax.ShapeDtypeStruct((t, D_MODEL), F32)]
                  + [jax.ShapeDtypeStruct((t, w), BF16) for w in widths],
        compiler_params=pltpu.CompilerParams(dimension_semantics=("arbitrary",),
                                             vmem_limit_bytes=VMEM_LIMIT_BYTES),
        name="ffn1_ln1_inproj",
    )(x2d, wg, wu, wd, g1, b1, win_a)


def _na_bias_tables(rpb, rows):
    kh, kw = NA_WIN_H, NA_WIN_W
    c = np.arange(GRID_W)[:, None]
    kc = np.arange(GRID_W)[None, :]
    c_start = np.clip(c - kw // 2, 0, GRID_W - kw)
    col_ok = (kc >= c_start) & (kc < c_start + kw)
    dc = kc - c + (kw - 1)
    sel_c = ((dc[..., None] == np.arange(2 * kw - 1)) & col_ok[..., None]).astype(np.float32)
    ri = np.arange(NA_RQ)[:, None]
    bj = np.arange(NA_BAND)[None, :]
    sel_r, ok = [], []
    for r0, b0 in ((0, 0), (NA_RQ, 0), (rows - NA_RQ, rows - NA_BAND)):
        r = r0 + ri
        kr = b0 + bj
        r_start = np.clip(r - kh // 2, 0, rows - kh)
        row_ok = (kr >= r_start) & (kr < r_start + kh)
        dr = kr - r + (kh - 1)
        sel_r.append(((dr[..., None] == np.arange(2 * kh - 1)) & row_ok[..., None]).astype(np.float32))
        ok.append(row_ok[:, None, :, None] & col_ok[None, :, None, :])
    sel_r = np.stack(sel_r)
    ok = np.stack(ok)[:, None]
    rpb_c = jnp.einsum('hdw,ckw->hdck', rpb, sel_c, precision=lax.Precision.HIGHEST)
    bias = jnp.einsum('tibd,hdck->thicbk', sel_r, rpb_c, precision=lax.Precision.HIGHEST)
    bias = jnp.where(ok, bias, NEG_BIG)
    return bias.reshape(3, NA_HEADS, NA_QT, NA_KT).astype(F32)


def _na_body(q_ref, k_ref, v_ref, bias_ref, o_ref, *, rows):
    nblk = rows // NA_RQ
    lane = lax.broadcasted_iota(jnp.int32, (1, LANES), 1)
    head_lanes = (lane < NA_HEAD_DIM, lane >= NA_HEAD_DIM)

    def block(i, carry):
        q0 = pl.multiple_of(i * NA_QT, NA_QT)
        band0 = jnp.clip(NA_RQ * i - NA_WIN_H // 2, 0, rows - NA_BAND)
        k0 = pl.multiple_of(band0 * GRID_W, GRID_W)
        kind = jnp.where(i == 0, 0, jnp.where(i == nblk - 1, 2, 1))
        q = q_ref[0, pl.ds(q0, NA_QT), :] * jnp.asarray(NA_HEAD_DIM ** -0.5, BF16)
        k = k_ref[0, pl.ds(k0, NA_KT), :]
        v = v_ref[0, pl.ds(k0, NA_KT), :]
        out = None
        for hh in range(2):
            kh = jnp.where(head_lanes[hh], k, jnp.zeros_like(k))
            vh = jnp.where(head_lanes[hh], v, jnp.zeros_like(v))
            s = lax.dot_general(q, kh, (((1,), (1,)), ((), ())), preferred_element_type=F32)
            s = s + bias_ref[kind, hh]
            mx = jnp.max(s, axis=-1, keepdims=True)
            p = jnp.exp(s - mx)
            den = jnp.sum(p, axis=-1, keepdims=True)
            o = jnp.dot(p.astype(BF16), vh, preferred_element_type=F32) * (1.0 / den)
            out = o if out is None else out + o
        o_ref[0, pl.ds(q0, NA_QT), :] = out.astype(BF16)
        return carry

    lax.fori_loop(0, nblk, block, 0)


def _neighborhood_attention(naqkv, bias, rows):
    b, s, _ = naqkv.shape
    npairs = NA_HEADS // 2
    tok = lambda off: pl.BlockSpec((1, s, LANES), lambda bi, p: (bi, 0, off + p))
    return pl.pallas_call(
        functools.partial(_na_body, rows=rows),
        grid=(b, npairs),
        in_specs=[tok(0), tok(npairs), tok(2 * npairs),
                  pl.BlockSpec((3, 2, NA_QT, NA_KT), lambda bi, p: (0, p, 0, 0))],
        out_specs=tok(0),
        out_shape=jax.ShapeDtypeStruct((b, s, NA_WIDTH), BF16),
        compiler_params=pltpu.CompilerParams(dimension_semantics=("arbitrary", "arbitrary"),
                                             vmem_limit_bytes=VMEM_LIMIT_BYTES),
        name="neighborhood_attention",
    )(naqkv, naqkv, naqkv, bias)


def _split3(x):
    hi = x.astype(BF16)
    r1 = x - hi.astype(F32)
    mid = r1.astype(BF16)
    lo = (r1 - mid.astype(F32)).astype(BF16)
    return hi, mid, lo


def _gla_chunk(qk_ref, vg_ref, dl_ref, wdec_ref, bdec_ref, st_ref, c0, fwd):
    C, KW = GLA_CHUNK, GLA_KEY_WIDTH
    d = 0 if fwd else 1
    rows = pl.ds(c0, C)
    q = qk_ref[0, rows, 0:KW].astype(F32)
    k = qk_ref[0, rows, KW:2 * KW].astype(F32)
    v = vg_ref[0, rows, 0:GLA_VAL_WIDTH]
    z = jnp.dot(dl_ref[0, rows, :], wdec_ref[d], preferred_element_type=F32) + bdec_ref[d:d + 1, :]
    log_a = (jnp.minimum(z, 0.0) - jnp.log1p(jnp.exp(-jnp.abs(z)))) * (1.0 / GLA_GATE_TAU)

    ci = lax.broadcasted_iota(jnp.int32, (C, C), 0)
    si = lax.broadcasted_iota(jnp.int32, (C, C), 1)
    tri = jnp.where((si <= ci) if fwd else (si >= ci), 1.0, 0.0).astype(BF16)
    pieces = jnp.concatenate(_split3(log_a), axis=1)
    bc = jnp.dot(tri, pieces, preferred_element_type=F32)
    b_cum = (bc[:, 0:KW] + bc[:, KW:2 * KW]) + bc[:, 2 * KW:3 * KW]
    b_last = b_cum[C - 1:C, :] if fwd else b_cum[0:1, :]

    q_t = ((q * (GLA_DK ** -0.5)) * jnp.exp(b_cum)).astype(BF16)
    k_t = (k * jnp.exp(-b_cum)).astype(BF16)
    k_end = (k * jnp.exp(b_last - b_cum)).astype(BF16)

    rh = lax.broadcasted_iota(jnp.int32, (GLA_HEADS * C, KW), 0) // C
    lh = lax.broadcasted_iota(jnp.int32, (GLA_HEADS * C, KW), 1) // GLA_DK
    same_head = rh == lh
    q_bd = jnp.where(same_head, jnp.concatenate([q_t] * GLA_HEADS, axis=0), jnp.zeros((), BF16))
    kend_bd = jnp.where(same_head, jnp.concatenate([k_end] * GLA_HEADS, axis=0), jnp.zeros((), BF16))

    attn = lax.dot_general(q_bd, k_t, (((1,), (1,)), ((), ())), preferred_element_type=F32)
    cc = lax.broadcasted_iota(jnp.int32, (GLA_HEADS * C, C), 0) % C
    ss = lax.broadcasted_iota(jnp.int32, (GLA_HEADS * C, C), 1)
    attn = jnp.where((ss <= cc) if fwd else (ss > cc), attn, 0.0).astype(BF16)
    v_heads = [v[:, h * GLA_DV:(h + 1) * GLA_DV] for h in range(GLA_HEADS)]
    o_intra = jnp.concatenate(
        [jnp.dot(attn[h * C:(h + 1) * C], v_heads[h], preferred_element_type=F32) for h in range(GLA_HEADS)],
        axis=0)
    st = st_ref[...]
    o_inter = lax.dot_general(q_bd, st.astype(BF16), (((1,), (1,)), ((), ())), preferred_element_type=F32)

    v_stack = jnp.concatenate(v_heads, axis=0)
    st_add = lax.dot_general(v_stack, kend_bd, (((0,), (0,)), ((), ())), preferred_element_type=F32)
    st_ref[...] = jnp.exp(b_last) * st + st_add
    return o_intra + o_inter


def _gla_body(qk_ref, vg_ref, dl_ref, wdec_ref, bdec_ref, ng_ref, o_ref, st_ref, ob_ref, *, cbk, nbk):
    C = GLA_CHUNK
    ph = pl.program_id(1)
    j = pl.program_id(2)
    chunks_per_block = cbk // C
    unroll = min(GLA_UNROLL, chunks_per_block)
    ngroups = chunks_per_block // unroll

    @pl.when(j == 0)
    def _():
        st_ref[...] = jnp.zeros_like(st_ref)

    def sweep(fwd):
        blk = j if fwd else nbk - 1 - j

        def group(gi, carry):
            g0 = gi if fwd else ngroups - 1 - gi
            for u in range(unroll):
                uu = u if fwd else unroll - 1 - u
                cl = g0 * unroll + uu
                c0 = pl.multiple_of(cl * C, C)
                cg = blk * chunks_per_block + cl
                o = _gla_chunk(qk_ref, vg_ref, dl_ref, wdec_ref, bdec_ref, st_ref, c0, fwd)
                if not fwd:
                    ob_ref[cg] = o
                else:
                    o = o + ob_ref[cg]
                    o = o * lax.rsqrt(jnp.mean(o * o, axis=-1, keepdims=True) + RMS_EPS) * ng_ref[...]
                    gate = vg_ref[0, pl.ds(c0, C), GLA_VAL_WIDTH:2 * GLA_VAL_WIDTH].astype(F32)
                    gate = jnp.concatenate(
                        [gate[:, h * GLA_DV:(h + 1) * GLA_DV] for h in range(GLA_HEADS)], axis=0)
                    o = o * (gate * jax.nn.sigmoid(gate))
                    o_ref[0, pl.ds(c0, C), :] = jnp.concatenate(
                        [o[h * C:(h + 1) * C] for h in range(GLA_HEADS)], axis=1).astype(BF16)
            return carry

        lax.fori_loop(0, ngroups, group, 0)

    @pl.when(ph == 0)
    def _():
        sweep(False)

    @pl.when(ph == 1)
    def _():
        sweep(True)


def _gla(gqk, gvg, gdl, wdec_pad, bdec, norm_g):
    b, s, _ = gqk.shape
    cbk = min(1024, s)
    nbk = s // cbk
    blk = lambda w: pl.BlockSpec((1, cbk, w), lambda bi, ph, j: (bi, jnp.where(ph == 0, nbk - 1 - j, j), 0))
    return pl.pallas_call(
        functools.partial(_gla_body, cbk=cbk, nbk=nbk),
        grid=(b, 2, nbk),
        in_specs=[blk(2 * GLA_KEY_WIDTH), blk(2 * GLA_VAL_WIDTH), blk(2 * GLA_GATE_RANK),
                  _const_spec(wdec_pad.shape), _const_spec(bdec.shape), _const_spec(norm_g.shape)],
        out_specs=pl.BlockSpec((1, cbk, GLA_VAL_WIDTH), lambda bi, ph, j: (bi, jnp.where(ph == 0, 0, j), 0)),
        out_shape=jax.ShapeDtypeStruct((b, s, GLA_VAL_WIDTH), BF16),
        scratch_shapes=[pltpu.VMEM((GLA_DV, GLA_KEY_WIDTH), F32),
                        pltpu.VMEM((s // GLA_CHUNK, GLA_HEADS * GLA_CHUNK, GLA_DV), F32)],
        compiler_params=pltpu.CompilerParams(dimension_semantics=("arbitrary", "arbitrary", "arbitrary"),
                                             vmem_limit_bytes=VMEM_LIMIT_BYTES),
        name="gla_bidirectional",
    )(gqk, gvg, gdl, wdec_pad, bdec, norm_g)


def _merge_ffn_body(x1_ref, na_ref, gla_ref, wgate_ref, wbn_ref, wbg_ref, wo_ref, g2_ref, b2_ref,
                    wg_ref, wu_ref, wd_ref, g3_ref, b3_ref, o_ref):
    x1 = x1_ref[...]
    gates = jnp.dot(x1.astype(BF16), wgate_ref[...], preferred_element_type=F32)
    y_na = jnp.dot(na_ref[...], wbn_ref[...], preferred_element_type=F32)
    y_gla = jnp.dot(gla_ref[...], wbg_ref[...], preferred_element_type=F32)
    merged = jax.nn.sigmoid(gates[:, :D_MODEL]) * y_na + jax.nn.sigmoid(gates[:, D_MODEL:]) * y_gla
    mix = jnp.dot(merged.astype(BF16), wo_ref[...], preferred_element_type=F32)
    x2 = _layer_norm(ALPHA * x1 + mix, g2_ref[...], b2_ref[...])
    y = _swiglu(x2.astype(BF16), wg_ref, wu_ref, wd_ref)
    o_ref[...] = _layer_norm(ALPHA * x2 + 0.5 * y, g3_ref[...], b3_ref[...])


def _merge_ffn(x1, na_o, gla_o, wgate, wbn, wbg, wo, g2, b2, wg, wu, wd, g3, b3):
    t = x1.shape[0]
    tm = min(TOKEN_TILE, t)
    row = lambda w: pl.BlockSpec((tm, w), lambda i: (i, 0))
    consts = (wgate, wbn, wbg, wo, g2, b2, wg, wu, wd, g3, b3)
    return pl.pallas_call(
        _merge_ffn_body,
        grid=(t // tm,),
        in_specs=[row(D_MODEL), row(NA_WIDTH), row(GLA_VAL_WIDTH)] + [_const_spec(c.shape) for c in consts],
        out_specs=row(D_MODEL),
        out_shape=jax.ShapeDtypeStruct((t, D_MODEL), F32),
        compiler_params=pltpu.CompilerParams(dimension_semantics=("arbitrary",),
                                             vmem_limit_bytes=VMEM_LIMIT_BYTES),
        name="merge_ln2_ffn2_ln3",
    )(x1, na_o, gla_o, *consts)


def _layer(x, ffn1_w_gate, ffn1_w_up, ffn1_w_down, ln1_g, ln1_b, w_in, na_rpb, gla_w_dec2, gla_b_dec,
           gla_norm_g, w_branch_na, w_branch_gla, w_out, ln2_g, ln2_b, ffn2_w_gate, ffn2_w_up, ffn2_w_down,
           ln3_g, ln3_b):
    b, s, d = x.shape
    assert d == D_MODEL and s % (GRID_W * NA_RQ) == 0 and s // GRID_W >= NA_BAND
    rows = s // GRID_W
    bf = lambda w: w.astype(BF16)
    vec = lambda p: p.reshape(1, -1)

    x1, naqkv, gqk, gvg, gdl = _ffn_inproj(
        x.reshape(b * s, d), bf(ffn1_w_gate), bf(ffn1_w_up), bf(ffn1_w_down), vec(ln1_g), vec(ln1_b),
        bf(w_in[:, :OFF_GDL_END]))

    na_o = _neighborhood_attention(naqkv.reshape(b, s, -1), _na_bias_tables(na_rpb, rows), rows)

    zr = jnp.zeros((GLA_GATE_RANK, GLA_KEY_WIDTH), gla_w_dec2.dtype)
    wdec_pad = bf(jnp.stack([jnp.concatenate([gla_w_dec2[0], zr]), jnp.concatenate([zr, gla_w_dec2[1]])]))
    gla_o = _gla(gqk.reshape(b, s, -1), gvg.reshape(b, s, -1), gdl.reshape(b, s, -1),
                 wdec_pad, gla_b_dec, vec(gla_norm_g))

    out = _merge_ffn(x1, na_o.reshape(b * s, -1), gla_o.reshape(b * s, -1), bf(w_in[:, OFF_GDL_END:]),
                     bf(w_branch_na), bf(w_branch_gla), bf(w_out), vec(ln2_g), vec(ln2_b),
                     bf(ffn2_w_gate), bf(ffn2_w_up), bf(ffn2_w_down), vec(ln3_g), vec(ln3_b))
    return out.reshape(b, s, d)


def kernel(x, ffn1_w_gate, ffn1_w_up, ffn1_w_down, ln1_g, ln1_b, w_in, na_rpb, gla_w_dec2, gla_b_dec, gla_norm_g,
           w_branch_na, w_branch_gla, w_out, ln2_g, ln2_b, ffn2_w_gate, ffn2_w_up, ffn2_w_down, ln3_g, ln3_b):
    params = (ffn1_w_gate, ffn1_w_up, ffn1_w_down, ln1_g, ln1_b, w_in, na_rpb, gla_w_dec2, gla_b_dec, gla_norm_g,
              w_branch_na, w_branch_gla, w_out, ln2_g, ln2_b, ffn2_w_gate, ffn2_w_up, ffn2_w_down, ln3_g, ln3_b)
    for l in range(DEPTH):
        x = _layer(x, *(p[l] for p in params))
    return x
```

```python
import functools

import jax
import jax.numpy as jnp
import numpy as np
from jax import lax
from jax.experimental import pallas as pl
from jax.experimental.pallas import tpu as pltpu

F32 = jnp.float32
BF16 = jnp.bfloat16

D_MODEL = 1024
D_FF = 2816
GRID_W = 64
NA_HEADS = 8
NA_HEAD_DIM = 64
NA_WIDTH = NA_HEADS * NA_HEAD_DIM
NA_WIN_H = 8
NA_WIN_W = 16
GLA_HEADS = 4
GLA_DK = 64
GLA_DV = 128
GLA_KEY_WIDTH = GLA_HEADS * GLA_DK
GLA_VAL_WIDTH = GLA_HEADS * GLA_DV
GLA_GATE_RANK = 16
GLA_GATE_TAU = 16.0
GLA_CHUNK = 64
DEPTH = 1
ALPHA = (2 * DEPTH) ** 0.25
LN_EPS = 1e-5
RMS_EPS = 1e-6

OFF_NA_END = 3 * NA_WIDTH
OFF_GQK_END = OFF_NA_END + 2 * GLA_KEY_WIDTH
OFF_GVG_END = OFF_GQK_END + 2 * GLA_VAL_WIDTH
OFF_GDL_END = OFF_GVG_END + 2 * GLA_GATE_RANK
N_IN = OFF_GDL_END + 2 * D_MODEL

LANES = 128
VMEM_LIMIT_BYTES = 56 * 1024 * 1024

NA_RQ = 4
NA_BAND = NA_RQ + NA_WIN_H
NA_QT = NA_RQ * GRID_W
NA_KT = NA_BAND * GRID_W
NEG_BIG = -1e30
LOG2E = 1.4426950408889634
NA_Q_SCALE = LOG2E * NA_HEAD_DIM ** -0.5

GLA_GROUP = 4
GLA_BLOCK = 1024
TOKEN_TILE = 256


def _const_spec(shape):
    nd = len(shape)
    return pl.BlockSpec(shape, lambda *_: (0,) * nd, pipeline_mode=pl.Buffered(1))


def _layer_norm(z, g, b):
    mu = jnp.mean(z, axis=-1, keepdims=True)
    zc = z - mu
    var = jnp.mean(zc * zc, axis=-1, keepdims=True)
    return zc * lax.rsqrt(var + LN_EPS) * g + b


def _swiglu(xb, wg_ref, wu_ref, wd_ref):
    g = jnp.dot(xb, wg_ref[...], preferred_element_type=F32)
    u = jnp.dot(xb, wu_ref[...], preferred_element_type=F32)
    h = (g * jax.nn.sigmoid(g)) * u
    return jnp.dot(h.astype(BF16), wd_ref[...], preferred_element_type=F32)


def _ffn_inproj_body(x_ref, wg_ref, wu_ref, wd_ref, g1_ref, b1_ref, win_ref,
                     x1_ref, naqkv_ref, gqk_ref, gvg_ref, gdl_ref):
    x = x_ref[...]
    y = _swiglu(x.astype(BF16), wg_ref, wu_ref, wd_ref)
    x1 = _layer_norm(ALPHA * x + 0.5 * y, g1_ref[...], b1_ref[...])
    x1_ref[...] = x1
    proj = jnp.dot(x1.astype(BF16), win_ref[...], preferred_element_type=F32)
    naqkv_ref[:, :NA_WIDTH] = (proj[:, :NA_WIDTH] * NA_Q_SCALE).astype(BF16)
    naqkv_ref[:, NA_WIDTH:] = proj[:, NA_WIDTH:OFF_NA_END].astype(BF16)
    gqk_ref[...] = proj[:, OFF_NA_END:OFF_GQK_END].astype(BF16)
    gvg_ref[...] = proj[:, OFF_GQK_END:OFF_GVG_END].astype(BF16)
    gdl_ref[...] = proj[:, OFF_GVG_END:OFF_GDL_END].astype(BF16)


def _ffn_inproj(x2d, wg, wu, wd, g1, b1, win_a):
    t = x2d.shape[0]
    tm = min(TOKEN_TILE, t)
    row = lambda w: pl.BlockSpec((tm, w), lambda i: (i, 0))
    widths = (OFF_NA_END, OFF_GQK_END - OFF_NA_END, OFF_GVG_END - OFF_GQK_END, OFF_GDL_END - OFF_GVG_END)
    return pl.pallas_call(
        _ffn_inproj_body,
        grid=(t // tm,),
        in_specs=[row(D_MODEL), _const_spec(wg.shape), _const_spec(wu.shape), _const_spec(wd.shape),
                  _const_spec(g1.shape), _const_spec(b1.shape), _const_spec(win_a.shape)],
        out_specs=[row(D_MODEL)] + [row(w) for w in widths],
        out_shape=[jax.ShapeDtypeStruct((t, D_MODEL), F32)]
                  + [jax.ShapeDtypeStruct((t, w), BF16) for w in widths],
        compiler_params=pltpu.CompilerParams(dimension_semantics=("arbitrary",),
                                             vmem_limit_bytes=VMEM_LIMIT_BYTES),
        name="ffn1_ln1_inproj",
    )(x2d, wg, wu, wd, g1, b1, win_a)


def _na_bias_tables(rpb, rows):
    kh, kw = NA_WIN_H, NA_WIN_W
    c = np.arange(GRID_W)[:, None]
    kc = np.arange(GRID_W)[None, :]
    c_start = np.clip(c - kw // 2, 0, GRID_W - kw)
    col_ok = (kc >= c_start) & (kc < c_start + kw)
    dc = kc - c + (kw - 1)
    sel_c = ((dc[..., None] == np.arange(2 * kw - 1)) & col_ok[..., None]).astype(np.float32)
    ri = np.arange(NA_RQ)[:, None]
    bj = np.arange(NA_BAND)[None, :]
    sel_r, ok = [], []
    for r0, b0 in ((0, 0), (NA_RQ, 0), (rows - NA_RQ, rows - NA_BAND)):
        r = r0 + ri
        kr = b0 + bj
        r_start = np.clip(r - kh // 2, 0, rows - kh)
        row_ok = (kr >= r_start) & (kr < r_start + kh)
        dr = kr - r + (kh - 1)
        sel_r.append(((dr[..., None] == np.arange(2 * kh - 1)) & row_ok[..., None]).astype(np.float32))
        ok.append(row_ok[:, None, :, None] & col_ok[None, :, None, :])
    sel_r = np.stack(sel_r)
    ok = np.stack(ok)[:, None]
    rpb_c = jnp.einsum('hdw,ckw->hdck', rpb, sel_c, precision=lax.Precision.HIGHEST)
    bias = jnp.einsum('tibd,hdck->thicbk', sel_r, rpb_c, precision=lax.Precision.HIGHEST)
    bias = jnp.where(ok, bias * LOG2E, NEG_BIG)
    return bias.reshape(3, NA_HEADS // 2, 2 * NA_QT, NA_KT).astype(F32)


def _na_body(q_ref, k_ref, v_ref, bias_ref, o_ref, va_ref, vb_ref, s_ref, *, rows):
    nblk = rows // NA_RQ
    lane = lax.broadcasted_iota(jnp.int32, (1, LANES), 1)
    first_head = lane < NA_HEAD_DIM

    def augment(t, carry):
        r = pl.ds(pl.multiple_of(t * NA_QT, NA_QT), NA_QT)
        v = v_ref[0, r, :]
        one = jnp.ones((), BF16)
        va_ref[r, :] = jnp.where(first_head, v, one)
        vb_ref[r, :] = jnp.where(first_head, one, v)
        return carry

    lax.fori_loop(0, nblk, augment, 0)

    def band_start(i):
        band0 = jnp.clip(NA_RQ * i - NA_WIN_H // 2, 0, rows - NA_BAND)
        return pl.multiple_of(band0 * GRID_W, GRID_W)

    def scores(i, slot):
        q0 = pl.multiple_of(i * NA_QT, NA_QT)
        kind = jnp.where(i == 0, 0, jnp.where(i == nblk - 1, 2, 1))
        q = q_ref[0, pl.ds(q0, NA_QT), :]
        k = k_ref[0, pl.ds(band_start(i), NA_KT), :]
        zero = jnp.zeros((), BF16)
        q2 = jnp.concatenate([jnp.where(first_head, q, zero), jnp.where(first_head, zero, q)], axis=0)
        s = lax.dot_general(q2, k, (((1,), (1,)), ((), ())), preferred_element_type=F32)
        s_ref[slot] = s + bias_ref[kind]

    def finish(i, slot):
        q0 = pl.multiple_of(i * NA_QT, NA_QT)
        k0 = band_start(i)
        s = s_ref[slot]
        p = jnp.exp2(s - jnp.max(s, axis=-1, keepdims=True)).astype(BF16)
        oa = jnp.dot(p[:NA_QT], va_ref[pl.ds(k0, NA_KT), :], preferred_element_type=F32)
        ob = jnp.dot(p[NA_QT:], vb_ref[pl.ds(k0, NA_KT), :], preferred_element_type=F32)
        num = jnp.where(first_head, oa, ob)
        den = pltpu.roll(jnp.where(first_head, ob, oa), NA_HEAD_DIM, axis=1)
        o_ref[0, pl.ds(q0, NA_QT), :] = (num / den).astype(BF16)

    scores(0, 0)

    def pair(j, carry):
        i = 2 * j
        scores(i + 1, 1)
        finish(i, 0)
        scores(jnp.minimum(i + 2, nblk - 1), 0)
        finish(i + 1, 1)
        return carry

    lax.fori_loop(0, nblk // 2, pair, 0)


def _neighborhood_attention(naqkv, bias, rows):
    b, s, _ = naqkv.shape
    npairs = NA_HEADS // 2
    tok = lambda off: pl.BlockSpec((1, s, LANES), lambda bi, p: (bi, 0, off + p))
    return pl.pallas_call(
        functools.partial(_na_body, rows=rows),
        grid=(b, npairs),
        in_specs=[tok(0), tok(npairs), tok(2 * npairs),
                  pl.BlockSpec((3, None, 2 * NA_QT, NA_KT), lambda bi, p: (0, p, 0, 0))],
        out_specs=tok(0),
        out_shape=jax.ShapeDtypeStruct((b, s, NA_WIDTH), BF16),
        scratch_shapes=[pltpu.VMEM((s, LANES), BF16), pltpu.VMEM((s, LANES), BF16),
                        pltpu.VMEM((2, 2 * NA_QT, NA_KT), F32)],
        compiler_params=pltpu.CompilerParams(dimension_semantics=("arbitrary", "arbitrary"),
                                             vmem_limit_bytes=VMEM_LIMIT_BYTES),
        name="neighborhood_attention",
    )(naqkv, naqkv, naqkv, bias)


def _split3(x):
    hi = x.astype(BF16)
    r1 = x - hi.astype(F32)
    mid = r1.astype(BF16)
    lo = (r1 - mid.astype(F32)).astype(BF16)
    return hi, mid, lo


def _gla_group(qk_ref, vg_ref, dl_ref, wdec_ref, bdec_ref, st, r0, fwd, nchunks):
    C, KW, H = GLA_CHUNK, GLA_KEY_WIDTH, GLA_HEADS
    G = nchunks * C
    d = 0 if fwd else 1
    rows = pl.ds(r0, G)
    q = qk_ref[0, rows, 0:KW].astype(F32)
    k = qk_ref[0, rows, KW:2 * KW].astype(F32)
    v = vg_ref[0, rows, 0:GLA_VAL_WIDTH]
    z = jnp.dot(dl_ref[0, rows, :], wdec_ref[d], preferred_element_type=F32) + bdec_ref[d:d + 1, :]
    log_a = (jnp.minimum(z, 0.0) - jnp.log1p(jnp.exp(-jnp.abs(z)))) * (1.0 / GLA_GATE_TAU)

    ci = lax.broadcasted_iota(jnp.int32, (G, G), 0)
    si = lax.broadcasted_iota(jnp.int32, (G, G), 1)
    in_order = (si <= ci) if fwd else (si >= ci)
    tri = jnp.where(in_order & ((ci // C) == (si // C)), 1.0, 0.0).astype(BF16)
    pieces = jnp.concatenate(_split3(log_a), axis=1)
    bc = jnp.dot(tri, pieces, preferred_element_type=F32)
    b_cum = (bc[:, 0:KW] + bc[:, KW:2 * KW]) + bc[:, 2 * KW:3 * KW]
    last = C - 1 if fwd else 0
    b_last_rows = [b_cum[u * C + last:u * C + last + 1, :] for u in range(nchunks)]
    b_last = jnp.concatenate([jnp.broadcast_to(r, (C, KW)) for r in b_last_rows], axis=0)

    q_t = ((q * (GLA_DK ** -0.5)) * jnp.exp(b_cum)).astype(BF16)
    k_t = (k * jnp.exp(-b_cum)).astype(BF16)
    k_end = (k * jnp.exp(b_last - b_cum)).astype(BF16)

    rh = lax.broadcasted_iota(jnp.int32, (H * C, KW), 0) // C
    lh = lax.broadcasted_iota(jnp.int32, (H * C, KW), 1) // GLA_DK
    same_head = rh == lh
    cc = lax.broadcasted_iota(jnp.int32, (H * C, C), 0) % C
    ss = lax.broadcasted_iota(jnp.int32, (H * C, C), 1)
    causal = (ss <= cc) if fwd else (ss > cc)
    zero = jnp.zeros((), BF16)
    chunk = lambda a, u: a[u * C:(u + 1) * C]
    q_bd = [jnp.where(same_head, jnp.concatenate([chunk(q_t, u)] * H, axis=0), zero) for u in range(nchunks)]
    kend_bd = [jnp.where(same_head, jnp.concatenate([chunk(k_end, u)] * H, axis=0), zero) for u in range(nchunks)]
    v_heads = [[chunk(v, u)[:, h * GLA_DV:(h + 1) * GLA_DV] for h in range(H)] for u in range(nchunks)]

    nt = (((1,), (1,)), ((), ()))
    attn = [lax.dot_general(q_bd[u], chunk(k_t, u), nt, preferred_element_type=F32) for u in range(nchunks)]
    attn = [jnp.where(causal, a, 0.0).astype(BF16) for a in attn]
    st_add = [lax.dot_general(jnp.concatenate(v_heads[u], axis=0), kend_bd[u], (((0,), (0,)), ((), ())),
                              preferred_element_type=F32) for u in range(nchunks)]
    o_intra = [jnp.concatenate([jnp.dot(chunk(attn[u], h), v_heads[u][h], preferred_element_type=F32)
                                for h in range(H)], axis=0) for u in range(nchunks)]
    outs = [None] * nchunks
    for u in (range(nchunks) if fwd else reversed(range(nchunks))):
        outs[u] = o_intra[u] + lax.dot_general(q_bd[u], st.astype(BF16), nt, preferred_element_type=F32)
        st = jnp.exp(b_last_rows[u]) * st + st_add[u]
    return outs, st


def _gla_body(qk_ref, vg_ref, dl_ref, wdec_ref, bdec_ref, ng_ref, o_ref, st_ref, ob_ref, *, cbk, nbk):
    C, H = GLA_CHUNK, GLA_HEADS
    ph = pl.program_id(1)
    j = pl.program_id(2)
    chunks_per_block = cbk // C
    nchunks = min(GLA_GROUP, chunks_per_block)
    ngroups = chunks_per_block // nchunks

    @pl.when(j == 0)
    def _():
        st_ref[...] = jnp.zeros_like(st_ref)

    def sweep(fwd):
        blk = j if fwd else nbk - 1 - j

        def group(gi, carry):
            g0 = gi if fwd else ngroups - 1 - gi
            r0 = pl.multiple_of(g0 * nchunks * C, nchunks * C)
            outs, st = _gla_group(qk_ref, vg_ref, dl_ref, wdec_ref, bdec_ref, st_ref[...], r0, fwd, nchunks)
            st_ref[...] = st
            for u in range(nchunks):
                cg = blk * chunks_per_block + g0 * nchunks + u
                if not fwd:
                    ob_ref[cg] = outs[u]
                    continue
                c0 = pl.multiple_of(r0 + u * C, C)
                o = outs[u] + ob_ref[cg]
                o = o * lax.rsqrt(jnp.mean(o * o, axis=-1, keepdims=True) + RMS_EPS) * ng_ref[...]
                gate = vg_ref[0, pl.ds(c0, C), GLA_VAL_WIDTH:2 * GLA_VAL_WIDTH].astype(F32)
                gate = jnp.concatenate([gate[:, h * GLA_DV:(h + 1) * GLA_DV] for h in range(H)], axis=0)
                o = o * (gate * jax.nn.sigmoid(gate))
                o_ref[0, pl.ds(c0, C), :] = jnp.concatenate(
                    [o[h * C:(h + 1) * C] for h in range(H)], axis=1).astype(BF16)
            return carry

        lax.fori_loop(0, ngroups, group, 0)

    @pl.when(ph == 0)
    def _():
        sweep(False)

    @pl.when(ph == 1)
    def _():
        sweep(True)


def _gla(gqk, gvg, gdl, wdec_pad, bdec, norm_g):
    b, s, _ = gqk.shape
    cbk = min(GLA_BLOCK, s)
    nbk = s // cbk
    blk = lambda w: pl.BlockSpec((1, cbk, w), lambda bi, ph, j: (bi, jnp.where(ph == 0, nbk - 1 - j, j), 0))
    return pl.pallas_call(
        functools.partial(_gla_body, cbk=cbk, nbk=nbk),
        grid=(b, 2, nbk),
        in_specs=[blk(2 * GLA_KEY_WIDTH), blk(2 * GLA_VAL_WIDTH), blk(2 * GLA_GATE_RANK),
                  _const_spec(wdec_pad.shape), _const_spec(bdec.shape), _const_spec(norm_g.shape)],
        out_specs=pl.BlockSpec((1, cbk, GLA_VAL_WIDTH), lambda bi, ph, j: (bi, jnp.where(ph == 0, 0, j), 0)),
        out_shape=jax.ShapeDtypeStruct((b, s, GLA_VAL_WIDTH), BF16),
        scratch_shapes=[pltpu.VMEM((GLA_DV, GLA_KEY_WIDTH), F32),
                        pltpu.VMEM((s // GLA_CHUNK, GLA_HEADS * GLA_CHUNK, GLA_DV), F32)],
        compiler_params=pltpu.CompilerParams(dimension_semantics=("arbitrary", "arbitrary", "arbitrary"),
                                             vmem_limit_bytes=VMEM_LIMIT_BYTES),
        name="gla_bidirectional",
    )(gqk, gvg, gdl, wdec_pad, bdec, norm_g)


def _merge_ffn_body(x1_ref, na_ref, gla_ref, wgate_ref, wbn_ref, wbg_ref, wo_ref, g2_ref, b2_ref,
                    wg_ref, wu_ref, wd_ref, g3_ref, b3_ref, o_ref):
    x1 = x1_ref[...]
    gates = jnp.dot(x1.astype(BF16), wgate_ref[...], preferred_element_type=F32)
    y_na = jnp.dot(na_ref[...], wbn_ref[...], preferred_element_type=F32)
    y_gla = jnp.dot(gla_ref[...], wbg_ref[...], preferred_element_type=F32)
    merged = jax.nn.sigmoid(gates[:, :D_MODEL]) * y_na + jax.nn.sigmoid(gates[:, D_MODEL:]) * y_gla
    mix = jnp.dot(merged.astype(BF16), wo_ref[...], preferred_element_type=F32)
    x2 = _layer_norm(ALPHA * x1 + mix, g2_ref[...], b2_ref[...])
    y = _swiglu(x2.astype(BF16), wg_ref, wu_ref, wd_ref)
    o_ref[...] = _layer_norm(ALPHA * x2 + 0.5 * y, g3_ref[...], b3_ref[...])


def _merge_ffn(x1, na_o, gla_o, wgate, wbn, wbg, wo, g2, b2, wg, wu, wd, g3, b3):
    t = x1.shape[0]
    tm = min(TOKEN_TILE, t)
    row = lambda w: pl.BlockSpec((tm, w), lambda i: (i, 0))
    consts = (wgate, wbn, wbg, wo, g2, b2, wg, wu, wd, g3, b3)
    return pl.pallas_call(
        _merge_ffn_body,
        grid=(t // tm,),
        in_specs=[row(D_MODEL), row(NA_WIDTH), row(GLA_VAL_WIDTH)] + [_const_spec(c.shape) for c in consts],
        out_specs=row(D_MODEL),
        out_shape=jax.ShapeDtypeStruct((t, D_MODEL), F32),
        compiler_params=pltpu.CompilerParams(dimension_semantics=("arbitrary",),
                                             vmem_limit_bytes=VMEM_LIMIT_BYTES),
        name="merge_ln2_ffn2_ln3",
    )(x1, na_o, gla_o, *consts)


def _layer(x, ffn1_w_gate, ffn1_w_up, ffn1_w_down, ln1_g, ln1_b, w_in, na_rpb, gla_w_dec2, gla_b_dec,
           gla_norm_g, w_branch_na, w_branch_gla, w_out, ln2_g, ln2_b, ffn2_w_gate, ffn2_w_up, ffn2_w_down,
           ln3_g, ln3_b):
    b, s, d = x.shape
    assert d == D_MODEL and s % (2 * NA_QT) == 0 and s // GRID_W >= NA_BAND
    rows = s // GRID_W
    bf = lambda w: w.astype(BF16)
    vec = lambda p: p.reshape(1, -1)

    x1, naqkv, gqk, gvg, gdl = _ffn_inproj(
        x.reshape(b * s, d), bf(ffn1_w_gate), bf(ffn1_w_up), bf(ffn1_w_down), vec(ln1_g), vec(ln1_b),
        bf(w_in[:, :OFF_GDL_END]))

    na_o = _neighborhood_attention(naqkv.reshape(b, s, -1), _na_bias_tables(na_rpb, rows), rows)

    zr = jnp.zeros((GLA_GATE_RANK, GLA_KEY_WIDTH), gla_w_dec2.dtype)
    wdec_pad = bf(jnp.stack([jnp.concatenate([gla_w_dec2[0], zr]), jnp.concatenate([zr, gla_w_dec2[1]])]))
    gla_o = _gla(gqk.reshape(b, s, -1), gvg.reshape(b, s, -1), gdl.reshape(b, s, -1),
                 wdec_pad, gla_b_dec, vec(gla_norm_g))

    out = _merge_ffn(x1, na_o.reshape(b * s, -1), gla_o.reshape(b * s, -1), bf(w_in[:, OFF_GDL_END:]),
                     bf(w_branch_na), bf(w_branch_gla), bf(w_out), vec(ln2_g), vec(ln2_b),
                     bf(ffn2_w_gate), bf(ffn2_w_up), bf(ffn2_w_down), vec(ln3_g), vec(ln3_b))
    return out.reshape(b, s, d)


def kernel(x, ffn1_w_gate, ffn1_w_up, ffn1_w_down, ln1_g, ln1_b, w_in, na_rpb, gla_w_dec2, gla_b_dec, gla_norm_g,
           w_branch_na, w_branch_gla, w_out, ln2_g, ln2_b, ffn2_w_gate, ffn2_w_up, ffn2_w_down, ln3_g, ln3_b):
    params = (ffn1_w_gate, ffn1_w_up, ffn1_w_down, ln1_g, ln1_b, w_in, na_rpb, gla_w_dec2, gla_b_dec, gla_norm_g,
              w_branch_na, w_branch_gla, w_out, ln2_g, ln2_b, ffn2_w_gate, ffn2_w_up, ffn2_w_down, ln3_g, ln3_b)
    for l in range(DEPTH):
        x = _layer(x, *(p[l] for p in params))
    return x
```

```python
import functools

import jax
import jax.numpy as jnp
import numpy as np
from jax import lax
from jax.experimental import pallas as pl
from jax.experimental.pallas import tpu as pltpu

F32 = jnp.float32
BF16 = jnp.bfloat16

D_MODEL = 1024
D_FF = 2816
GRID_W = 64
NA_HEADS = 8
NA_HEAD_DIM = 64
NA_WIDTH = NA_HEADS * NA_HEAD_DIM
NA_WIN_H = 8
NA_WIN_W = 16
GLA_HEADS = 4
GLA_DK = 64
GLA_DV = 128
GLA_KEY_WIDTH = GLA_HEADS * GLA_DK
GLA_VAL_WIDTH = GLA_HEADS * GLA_DV
GLA_GATE_RANK = 16
GLA_GATE_TAU = 16.0
GLA_CHUNK = 64
DEPTH = 1
ALPHA = (2 * DEPTH) ** 0.25
LN_EPS = 1e-5
RMS_EPS = 1e-6

OFF_NA_END = 3 * NA_WIDTH
OFF_GQK_END = OFF_NA_END + 2 * GLA_KEY_WIDTH
OFF_GVG_END = OFF_GQK_END + 2 * GLA_VAL_WIDTH
OFF_GDL_END = OFF_GVG_END + 2 * GLA_GATE_RANK
N_IN = OFF_GDL_END + 2 * D_MODEL

LANES = 128
VMEM_LIMIT_BYTES = 56 * 1024 * 1024

NA_RQ = 4
NA_BAND = NA_RQ + NA_WIN_H
NA_QT = NA_RQ * GRID_W
NA_KT = NA_BAND * GRID_W
NEG_BIG = -1e30
LOG2E = 1.4426950408889634
NA_Q_SCALE = LOG2E * NA_HEAD_DIM ** -0.5

GLA_GROUP = 4
GLA_BLOCK = 1024
TOKEN_TILE = 512
SUB_TILES = 2


def _const_spec(shape):
    nd = len(shape)
    return pl.BlockSpec(shape, lambda *_: (0,) * nd, pipeline_mode=pl.Buffered(1))


def _layer_norm(z, g, b):
    mu = jnp.mean(z, axis=-1, keepdims=True)
    zc = z - mu
    var = jnp.mean(zc * zc, axis=-1, keepdims=True)
    return zc * lax.rsqrt(var + LN_EPS) * g + b


def _swiglu(xb, wg_ref, wu_ref, wd_ref):
    g = jnp.dot(xb, wg_ref[...], preferred_element_type=F32)
    u = jnp.dot(xb, wu_ref[...], preferred_element_type=F32)
    h = (g * jax.nn.sigmoid(g)) * u
    return jnp.dot(h.astype(BF16), wd_ref[...], preferred_element_type=F32)


def _ffn_inproj_body(x_ref, wg_ref, wu_ref, wd_ref, g1_ref, b1_ref, win_ref,
                     x1_ref, naqkv_ref, gqk_ref, gvg_ref, gdl_ref):
    tm = x_ref.shape[0]
    subs = [pl.ds(r, tm // SUB_TILES) for r in range(0, tm, tm // SUB_TILES)]
    zs = []
    for rows in subs:
        x = x_ref[rows, :]
        zs.append(ALPHA * x + 0.5 * _swiglu(x.astype(BF16), wg_ref, wu_ref, wd_ref))
    for rows, z in zip(subs, zs):
        x1 = _layer_norm(z, g1_ref[...], b1_ref[...])
        x1_ref[rows, :] = x1
        proj = jnp.dot(x1.astype(BF16), win_ref[...], preferred_element_type=F32)
        naqkv_ref[rows, :NA_WIDTH] = (proj[:, :NA_WIDTH] * NA_Q_SCALE).astype(BF16)
        naqkv_ref[rows, NA_WIDTH:] = proj[:, NA_WIDTH:OFF_NA_END].astype(BF16)
        gqk_ref[rows, :] = proj[:, OFF_NA_END:OFF_GQK_END].astype(BF16)
        gvg_ref[rows, :] = proj[:, OFF_GQK_END:OFF_GVG_END].astype(BF16)
        gdl_ref[rows, :] = proj[:, OFF_GVG_END:OFF_GDL_END].astype(BF16)


def _ffn_inproj(x2d, wg, wu, wd, g1, b1, win_a):
    t = x2d.shape[0]
    tm = min(TOKEN_TILE, t)
    row = lambda w: pl.BlockSpec((tm, w), lambda i: (i, 0))
    widths = (OFF_NA_END, OFF_GQK_END - OFF_NA_END, OFF_GVG_END - OFF_GQK_END, OFF_GDL_END - OFF_GVG_END)
    return pl.pallas_call(
        _ffn_inproj_body,
        grid=(t // tm,),
        in_specs=[row(D_MODEL), _const_spec(wg.shape), _const_spec(wu.shape), _const_spec(wd.shape),
                  _const_spec(g1.shape), _const_spec(b1.shape), _const_spec(win_a.shape)],
        out_specs=[row(D_MODEL)] + [row(w) for w in widths],
        out_shape=[jax.ShapeDtypeStruct((t, D_MODEL), F32)]
                  + [jax.ShapeDtypeStruct((t, w), BF16) for w in widths],
        compiler_params=pltpu.CompilerParams(dimension_semantics=("arbitrary",),
                                             vmem_limit_bytes=VMEM_LIMIT_BYTES),
        name="ffn1_ln1_inproj",
    )(x2d, wg, wu, wd, g1, b1, win_a)


NA_ROW_OFFSETS = 2 * NA_WIN_H - 1


def _na_bias_tiles(rpb):
    kw = NA_WIN_W
    c = np.arange(GRID_W)[:, None]
    kc = np.arange(GRID_W)[None, :]
    c_start = np.clip(c - kw // 2, 0, GRID_W - kw)
    col_ok = (kc >= c_start) & (kc < c_start + kw)
    dc = kc - c + (kw - 1)
    sel_c = ((dc[..., None] == np.arange(2 * kw - 1)) & col_ok[..., None]).astype(np.float32)
    tiles = jnp.einsum('hdw,ckw->hdck', rpb, sel_c, precision=lax.Precision.HIGHEST)
    tiles = jnp.where(col_ok, tiles * LOG2E, NEG_BIG)
    tiles = jnp.concatenate([tiles, jnp.full_like(tiles[:, :1], NEG_BIG)], axis=1)
    return jnp.concatenate([tiles, tiles], axis=-1).astype(F32)


def _na_row_offsets(rows):
    kh = NA_WIN_H
    ri = np.arange(NA_RQ)[:, None]
    bj = np.arange(NA_BAND)[None, :]
    out = []
    for r0, b0 in ((0, 0), (NA_RQ, 0), (rows - NA_RQ, rows - NA_BAND)):
        r = r0 + ri
        kr = b0 + bj
        r_start = np.clip(r - kh // 2, 0, rows - kh)
        row_ok = (kr >= r_start) & (kr < r_start + kh)
        out.append(np.where(row_ok, kr - r + (kh - 1), NA_ROW_OFFSETS).tolist())
    return out


def _na_body(q_ref, k_ref, v_ref, tile_ref, o_ref, va_ref, vb_ref, s_ref, bias_ref, *, rows):
    nblk = rows // NA_RQ
    lane = lax.broadcasted_iota(jnp.int32, (1, LANES), 1)
    first_head = lane < NA_HEAD_DIM

    @pl.when(pl.program_id(1) == 0)
    def _():
        for kind, offsets in enumerate(_na_row_offsets(rows)):
            for hh in range(2):
                for ri in range(NA_RQ):
                    for bj in range(0, NA_BAND, 2):
                        tile = jnp.where(first_head, tile_ref[hh, offsets[ri][bj]], tile_ref[hh, offsets[ri][bj + 1]])
                        bias_ref[kind, pl.ds(hh * NA_QT + ri * GRID_W, GRID_W), pl.ds(bj * GRID_W, LANES)] = tile

    def augment(t, carry):
        r = pl.ds(pl.multiple_of(t * NA_QT, NA_QT), NA_QT)
        v = v_ref[0, r, :]
        one = jnp.ones((), BF16)
        va_ref[r, :] = jnp.where(first_head, v, one)
        vb_ref[r, :] = jnp.where(first_head, one, v)
        return carry

    lax.fori_loop(0, nblk, augment, 0)

    def band_start(i):
        band0 = jnp.clip(NA_RQ * i - NA_WIN_H // 2, 0, rows - NA_BAND)
        return pl.multiple_of(band0 * GRID_W, GRID_W)

    def scores(i, slot):
        q0 = pl.multiple_of(i * NA_QT, NA_QT)
        kind = jnp.where(i == 0, 0, jnp.where(i == nblk - 1, 2, 1))
        q = q_ref[0, pl.ds(q0, NA_QT), :]
        k = k_ref[0, pl.ds(band_start(i), NA_KT), :]
        zero = jnp.zeros((), BF16)
        q2 = jnp.concatenate([jnp.where(first_head, q, zero), jnp.where(first_head, zero, q)], axis=0)
        s = lax.dot_general(q2, k, (((1,), (1,)), ((), ())), preferred_element_type=F32)
        s_ref[slot] = s + bias_ref[kind]

    def finish(i, slot):
        q0 = pl.multiple_of(i * NA_QT, NA_QT)
        k0 = band_start(i)
        s = s_ref[slot]
        p = jnp.exp2(s - jnp.max(s, axis=-1, keepdims=True)).astype(BF16)
        oa = jnp.dot(p[:NA_QT], va_ref[pl.ds(k0, NA_KT), :], preferred_element_type=F32)
        ob = jnp.dot(p[NA_QT:], vb_ref[pl.ds(k0, NA_KT), :], preferred_element_type=F32)
        num = jnp.where(first_head, oa, ob)
        den = pltpu.roll(jnp.where(first_head, ob, oa), NA_HEAD_DIM, axis=1)
        o_ref[0, pl.ds(q0, NA_QT), :] = (num / den).astype(BF16)

    scores(0, 0)

    def pair(j, carry):
        i = 2 * j
        scores(i + 1, 1)
        finish(i, 0)
        scores(jnp.minimum(i + 2, nblk - 1), 0)
        finish(i + 1, 1)
        return carry

    lax.fori_loop(0, nblk // 2, pair, 0)


def _neighborhood_attention(naqkv, tiles, rows):
    b, s, _ = naqkv.shape
    npairs = NA_HEADS // 2
    tok = lambda off: pl.BlockSpec((1, s, LANES), lambda p, bi: (bi, 0, off + p))
    return pl.pallas_call(
        functools.partial(_na_body, rows=rows),
        grid=(npairs, b),
        in_specs=[tok(0), tok(npairs), tok(2 * npairs),
                  pl.BlockSpec((2, NA_ROW_OFFSETS + 1, GRID_W, LANES), lambda p, bi: (p, 0, 0, 0))],
        out_specs=tok(0),
        out_shape=jax.ShapeDtypeStruct((b, s, NA_WIDTH), BF16),
        scratch_shapes=[pltpu.VMEM((s, LANES), BF16), pltpu.VMEM((s, LANES), BF16),
                        pltpu.VMEM((2, 2 * NA_QT, NA_KT), F32), pltpu.VMEM((3, 2 * NA_QT, NA_KT), F32)],
        compiler_params=pltpu.CompilerParams(dimension_semantics=("arbitrary", "arbitrary"),
                                             vmem_limit_bytes=VMEM_LIMIT_BYTES),
        name="neighborhood_attention",
    )(naqkv, naqkv, naqkv, tiles)


def _split3(x):
    hi = x.astype(BF16)
    r1 = x - hi.astype(F32)
    mid = r1.astype(BF16)
    lo = (r1 - mid.astype(F32)).astype(BF16)
    return hi, mid, lo


def _gla_group(qk_ref, vg_ref, dl_ref, wdec_ref, bdec_ref, st, r0, fwd, nchunks):
    C, KW, H = GLA_CHUNK, GLA_KEY_WIDTH, GLA_HEADS
    G = nchunks * C
    d = 0 if fwd else 1
    rows = pl.ds(r0, G)
    q = qk_ref[0, rows, 0:KW].astype(F32)
    k = qk_ref[0, rows, KW:2 * KW].astype(F32)
    v = vg_ref[0, rows, 0:GLA_VAL_WIDTH]
    z = jnp.dot(dl_ref[0, rows, :], wdec_ref[d], preferred_element_type=F32) + bdec_ref[d:d + 1, :]
    log_a = (jnp.minimum(z, 0.0) - jnp.log1p(jnp.exp(-jnp.abs(z)))) * (1.0 / GLA_GATE_TAU)

    ci = lax.broadcasted_iota(jnp.int32, (G, G), 0)
    si = lax.broadcasted_iota(jnp.int32, (G, G), 1)
    in_order = (si <= ci) if fwd else (si >= ci)
    tri = jnp.where(in_order & ((ci // C) == (si // C)), 1.0, 0.0).astype(BF16)
    pieces = jnp.concatenate(_split3(log_a), axis=1)
    bc = jnp.dot(tri, pieces, preferred_element_type=F32)
    b_cum = (bc[:, 0:KW] + bc[:, KW:2 * KW]) + bc[:, 2 * KW:3 * KW]
    last = C - 1 if fwd else 0
    b_last_rows = [b_cum[u * C + last:u * C + last + 1, :] for u in range(nchunks)]
    b_last = jnp.concatenate([jnp.broadcast_to(r, (C, KW)) for r in b_last_rows], axis=0)

    q_t = ((q * (GLA_DK ** -0.5)) * jnp.exp(b_cum)).astype(BF16)
    k_t = (k * jnp.exp(-b_cum)).astype(BF16)
    k_end = (k * jnp.exp(b_last - b_cum)).astype(BF16)

    rh = lax.broadcasted_iota(jnp.int32, (H * C, KW), 0) // C
    lh = lax.broadcasted_iota(jnp.int32, (H * C, KW), 1) // GLA_DK
    same_head = rh == lh
    cc = lax.broadcasted_iota(jnp.int32, (H * C, C), 0) % C
    ss = lax.broadcasted_iota(jnp.int32, (H * C, C), 1)
    causal = (ss <= cc) if fwd else (ss > cc)
    zero = jnp.zeros((), BF16)
    chunk = lambda a, u: a[u * C:(u + 1) * C]
    q_bd = [jnp.where(same_head, jnp.concatenate([chunk(q_t, u)] * H, axis=0), zero) for u in range(nchunks)]
    kend_bd = [jnp.where(same_head, jnp.concatenate([chunk(k_end, u)] * H, axis=0), zero) for u in range(nchunks)]
    v_heads = [[chunk(v, u)[:, h * GLA_DV:(h + 1) * GLA_DV] for h in range(H)] for u in range(nchunks)]

    nt = (((1,), (1,)), ((), ()))
    attn = [lax.dot_general(q_bd[u], chunk(k_t, u), nt, preferred_element_type=F32) for u in range(nchunks)]
    attn = [jnp.where(causal, a, 0.0).astype(BF16) for a in attn]
    st_add = [lax.dot_general(jnp.concatenate(v_heads[u], axis=0), kend_bd[u], (((0,), (0,)), ((), ())),
                              preferred_element_type=F32) for u in range(nchunks)]
    o_intra = [jnp.concatenate([jnp.dot(chunk(attn[u], h), v_heads[u][h], preferred_element_type=F32)
                                for h in range(H)], axis=0) for u in range(nchunks)]
    outs = [None] * nchunks
    for u in (range(nchunks) if fwd else reversed(range(nchunks))):
        outs[u] = o_intra[u] + lax.dot_general(q_bd[u], st.astype(BF16), nt, preferred_element_type=F32)
        st = jnp.exp(b_last_rows[u]) * st + st_add[u]
    return outs, st


def _gla_body(qk_ref, vg_ref, dl_ref, wdec_ref, bdec_ref, ng_ref, o_ref, st_ref, ob_ref, *, cbk, nbk):
    C, H = GLA_CHUNK, GLA_HEADS
    ph = pl.program_id(1)
    j = pl.program_id(2)
    chunks_per_block = cbk // C
    nchunks = min(GLA_GROUP, chunks_per_block)
    ngroups = chunks_per_block // nchunks

    @pl.when(j == 0)
    def _():
        st_ref[...] = jnp.zeros_like(st_ref)

    def sweep(fwd):
        blk = j if fwd else nbk - 1 - j

        def group(gi, carry):
            g0 = gi if fwd else ngroups - 1 - gi
            r0 = pl.multiple_of(g0 * nchunks * C, nchunks * C)
            outs, st = _gla_group(qk_ref, vg_ref, dl_ref, wdec_ref, bdec_ref, st_ref[...], r0, fwd, nchunks)
            st_ref[...] = st
            for u in range(nchunks):
                cg = blk * chunks_per_block + g0 * nchunks + u
                if not fwd:
                    ob_ref[cg] = outs[u]
                    continue
                c0 = pl.multiple_of(r0 + u * C, C)
                o = outs[u] + ob_ref[cg]
                o = o * lax.rsqrt(jnp.mean(o * o, axis=-1, keepdims=True) + RMS_EPS) * ng_ref[...]
                gate = vg_ref[0, pl.ds(c0, C), GLA_VAL_WIDTH:2 * GLA_VAL_WIDTH].astype(F32)
                gate = jnp.concatenate([gate[:, h * GLA_DV:(h + 1) * GLA_DV] for h in range(H)], axis=0)
                o = o * (gate * jax.nn.sigmoid(gate))
                o_ref[0, pl.ds(c0, C), :] = jnp.concatenate(
                    [o[h * C:(h + 1) * C] for h in range(H)], axis=1).astype(BF16)
            return carry

        lax.fori_loop(0, ngroups, group, 0)

    @pl.when(ph == 0)
    def _():
        sweep(False)

    @pl.when(ph == 1)
    def _():
        sweep(True)


def _gla(gqk, gvg, gdl, wdec_pad, bdec, norm_g):
    b, s, _ = gqk.shape
    cbk = min(GLA_BLOCK, s)
    nbk = s // cbk
    blk = lambda w: pl.BlockSpec((1, cbk, w), lambda bi, ph, j: (bi, jnp.where(ph == 0, nbk - 1 - j, j), 0))
    return pl.pallas_call(
        functools.partial(_gla_body, cbk=cbk, nbk=nbk),
        grid=(b, 2, nbk),
        in_specs=[blk(2 * GLA_KEY_WIDTH), blk(2 * GLA_VAL_WIDTH), blk(2 * GLA_GATE_RANK),
                  _const_spec(wdec_pad.shape), _const_spec(bdec.shape), _const_spec(norm_g.shape)],
        out_specs=pl.BlockSpec((1, cbk, GLA_VAL_WIDTH), lambda bi, ph, j: (bi, jnp.where(ph == 0, 0, j), 0)),
        out_shape=jax.ShapeDtypeStruct((b, s, GLA_VAL_WIDTH), BF16),
        scratch_shapes=[pltpu.VMEM((GLA_DV, GLA_KEY_WIDTH), F32),
                        pltpu.VMEM((s // GLA_CHUNK, GLA_HEADS * GLA_CHUNK, GLA_DV), F32)],
        compiler_params=pltpu.CompilerParams(dimension_semantics=("arbitrary", "arbitrary", "arbitrary"),
                                             vmem_limit_bytes=VMEM_LIMIT_BYTES),
        name="gla_bidirectional",
    )(gqk, gvg, gdl, wdec_pad, bdec, norm_g)


def _merge_ffn_body(x1_ref, na_ref, gla_ref, wgate_ref, wbn_ref, wbg_ref, wo_ref, g2_ref, b2_ref,
                    wg_ref, wu_ref, wd_ref, g3_ref, b3_ref, o_ref):
    tm = x1_ref.shape[0]
    subs = [pl.ds(r, tm // SUB_TILES) for r in range(0, tm, tm // SUB_TILES)]
    dot = functools.partial(jnp.dot, preferred_element_type=F32)
    x1s = [x1_ref[rows, :] for rows in subs]
    pre = [(dot(x1.astype(BF16), wgate_ref[...]), dot(na_ref[rows, :], wbn_ref[...]),
            dot(gla_ref[rows, :], wbg_ref[...])) for rows, x1 in zip(subs, x1s)]
    z2s = []
    for x1, (gates, y_na, y_gla) in zip(x1s, pre):
        merged = jax.nn.sigmoid(gates[:, :D_MODEL]) * y_na + jax.nn.sigmoid(gates[:, D_MODEL:]) * y_gla
        z2s.append(ALPHA * x1 + dot(merged.astype(BF16), wo_ref[...]))
    x2s = [_layer_norm(z2, g2_ref[...], b2_ref[...]) for z2 in z2s]
    z3s = [ALPHA * x2 + 0.5 * _swiglu(x2.astype(BF16), wg_ref, wu_ref, wd_ref) for x2 in x2s]
    for rows, z3 in zip(subs, z3s):
        o_ref[rows, :] = _layer_norm(z3, g3_ref[...], b3_ref[...])


def _merge_ffn(x1, na_o, gla_o, wgate, wbn, wbg, wo, g2, b2, wg, wu, wd, g3, b3):
    t = x1.shape[0]
    tm = min(TOKEN_TILE, t)
    row = lambda w: pl.BlockSpec((tm, w), lambda i: (i, 0))
    consts = (wgate, wbn, wbg, wo, g2, b2, wg, wu, wd, g3, b3)
    return pl.pallas_call(
        _merge_ffn_body,
        grid=(t // tm,),
        in_specs=[row(D_MODEL), row(NA_WIDTH), row(GLA_VAL_WIDTH)] + [_const_spec(c.shape) for c in consts],
        out_specs=row(D_MODEL),
        out_shape=jax.ShapeDtypeStruct((t, D_MODEL), F32),
        compiler_params=pltpu.CompilerParams(dimension_semantics=("arbitrary",),
                                             vmem_limit_bytes=VMEM_LIMIT_BYTES),
        name="merge_ln2_ffn2_ln3",
    )(x1, na_o, gla_o, *consts)


def _layer(x, ffn1_w_gate, ffn1_w_up, ffn1_w_down, ln1_g, ln1_b, w_in, na_rpb, gla_w_dec2, gla_b_dec,
           gla_norm_g, w_branch_na, w_branch_gla, w_out, ln2_g, ln2_b, ffn2_w_gate, ffn2_w_up, ffn2_w_down,
           ln3_g, ln3_b):
    b, s, d = x.shape
    assert d == D_MODEL and s % (2 * NA_QT) == 0 and s // GRID_W >= NA_BAND
    rows = s // GRID_W
    bf = lambda w: w.astype(BF16)
    vec = lambda p: p.reshape(1, -1)

    x1, naqkv, gqk, gvg, gdl = _ffn_inproj(
        x.reshape(b * s, d), bf(ffn1_w_gate), bf(ffn1_w_up), bf(ffn1_w_down), vec(ln1_g), vec(ln1_b),
        bf(w_in[:, :OFF_GDL_END]))

    na_o = _neighborhood_attention(naqkv.reshape(b, s, -1), _na_bias_tiles(na_rpb), rows)

    zr = jnp.zeros((GLA_GATE_RANK, GLA_KEY_WIDTH), gla_w_dec2.dtype)
    wdec_pad = bf(jnp.stack([jnp.concatenate([gla_w_dec2[0], zr]), jnp.concatenate([zr, gla_w_dec2[1]])]))
    gla_o = _gla(gqk.reshape(b, s, -1), gvg.reshape(b, s, -1), gdl.reshape(b, s, -1),
                 wdec_pad, gla_b_dec, vec(gla_norm_g))

    out = _merge_ffn(x1, na_o.reshape(b * s, -1), gla_o.reshape(b * s, -1), bf(w_in[:, OFF_GDL_END:]),
                     bf(w_branch_na), bf(w_branch_gla), bf(w_out), vec(ln2_g), vec(ln2_b),
                     bf(ffn2_w_gate), bf(ffn2_w_up), bf(ffn2_w_down), vec(ln3_g), vec(ln3_b))
    return out.reshape(b, s, d)


def kernel(x, ffn1_w_gate, ffn1_w_up, ffn1_w_down, ln1_g, ln1_b, w_in, na_rpb, gla_w_dec2, gla_b_dec, gla_norm_g,
           w_branch_na, w_branch_gla, w_out, ln2_g, ln2_b, ffn2_w_gate, ffn2_w_up, ffn2_w_down, ln3_g, ln3_b):
    params = (ffn1_w_gate, ffn1_w_up, ffn1_w_down, ln1_g, ln1_b, w_in, na_rpb, gla_w_dec2, gla_b_dec, gla_norm_g,
              w_branch_na, w_branch_gla, w_out, ln2_g, ln2_b, ffn2_w_gate, ffn2_w_up, ffn2_w_down, ln3_g, ln3_b)
    for l in range(DEPTH):
        x = _layer(x, *(p[l] for p in params))
    return x
```

```python
import functools

import jax
import jax.numpy as jnp
import numpy as np
from jax import lax
from jax.experimental import pallas as pl
from jax.experimental.pallas import tpu as pltpu

F32 = jnp.float32
BF16 = jnp.bfloat16

D_MODEL = 1024
D_FF = 2816
GRID_W = 64
NA_HEADS = 8
NA_HEAD_DIM = 64
NA_WIDTH = NA_HEADS * NA_HEAD_DIM
NA_WIN_H = 8
NA_WIN_W = 16
GLA_HEADS = 4
GLA_DK = 64
GLA_DV = 128
GLA_KEY_WIDTH = GLA_HEADS * GLA_DK
GLA_VAL_WIDTH = GLA_HEADS * GLA_DV
GLA_GATE_RANK = 16
GLA_GATE_TAU = 16.0
GLA_CHUNK = 64
DEPTH = 1
ALPHA = (2 * DEPTH) ** 0.25
LN_EPS = 1e-5
RMS_EPS = 1e-6

OFF_NA_END = 3 * NA_WIDTH
OFF_GQK_END = OFF_NA_END + 2 * GLA_KEY_WIDTH
OFF_GVG_END = OFF_GQK_END + 2 * GLA_VAL_WIDTH
OFF_GDL_END = OFF_GVG_END + 2 * GLA_GATE_RANK
N_IN = OFF_GDL_END + 2 * D_MODEL

LANES = 128
VMEM_LIMIT_BYTES = 56 * 1024 * 1024

NA_RQ = 4
NA_BAND = NA_RQ + NA_WIN_H
NA_QT = NA_RQ * GRID_W
NA_KT = NA_BAND * GRID_W
NEG_BIG = -1e30
LOG2E = 1.4426950408889634
NA_Q_SCALE = LOG2E * NA_HEAD_DIM ** -0.5

GLA_GROUP = 4
GLA_BLOCK = 1024
GLA_SEQS = 2
TOKEN_TILE = 512
SUB_TILES = 2


def _const_spec(shape):
    nd = len(shape)
    return pl.BlockSpec(shape, lambda *_: (0,) * nd, pipeline_mode=pl.Buffered(1))


def _layer_norm(z, g, b):
    mu = jnp.mean(z, axis=-1, keepdims=True)
    zc = z - mu
    var = jnp.mean(zc * zc, axis=-1, keepdims=True)
    return zc * lax.rsqrt(var + LN_EPS) * g + b


def _swiglu(xb, wg_ref, wu_ref, wd_ref):
    g = jnp.dot(xb, wg_ref[...], preferred_element_type=F32)
    u = jnp.dot(xb, wu_ref[...], preferred_element_type=F32)
    h = (g * jax.nn.sigmoid(g)) * u
    return jnp.dot(h.astype(BF16), wd_ref[...], preferred_element_type=F32)


def _ffn_inproj_body(x_ref, wg_ref, wu_ref, wd_ref, g1_ref, b1_ref, win_ref,
                     x1_ref, naqkv_ref, gqk_ref, gvg_ref, gdl_ref):
    tm = x_ref.shape[0]
    subs = [pl.ds(r, tm // SUB_TILES) for r in range(0, tm, tm // SUB_TILES)]
    zs = []
    for rows in subs:
        x = x_ref[rows, :]
        zs.append(ALPHA * x + 0.5 * _swiglu(x.astype(BF16), wg_ref, wu_ref, wd_ref))
    for rows, z in zip(subs, zs):
        x1 = _layer_norm(z, g1_ref[...], b1_ref[...])
        x1_ref[rows, :] = x1
        proj = jnp.dot(x1.astype(BF16), win_ref[...], preferred_element_type=F32)
        naqkv_ref[rows, :NA_WIDTH] = (proj[:, :NA_WIDTH] * NA_Q_SCALE).astype(BF16)
        naqkv_ref[rows, NA_WIDTH:] = proj[:, NA_WIDTH:OFF_NA_END].astype(BF16)
        gqk_ref[rows, :] = proj[:, OFF_NA_END:OFF_GQK_END].astype(BF16)
        gvg_ref[rows, :] = proj[:, OFF_GQK_END:OFF_GVG_END].astype(BF16)
        gdl_ref[rows, :] = proj[:, OFF_GVG_END:OFF_GDL_END].astype(BF16)


def _ffn_inproj(x2d, wg, wu, wd, g1, b1, win_a):
    t = x2d.shape[0]
    tm = min(TOKEN_TILE, t)
    row = lambda w: pl.BlockSpec((tm, w), lambda i: (i, 0))
    widths = (OFF_NA_END, OFF_GQK_END - OFF_NA_END, OFF_GVG_END - OFF_GQK_END, OFF_GDL_END - OFF_GVG_END)
    return pl.pallas_call(
        _ffn_inproj_body,
        grid=(t // tm,),
        in_specs=[row(D_MODEL), _const_spec(wg.shape), _const_spec(wu.shape), _const_spec(wd.shape),
                  _const_spec(g1.shape), _const_spec(b1.shape), _const_spec(win_a.shape)],
        out_specs=[row(D_MODEL)] + [row(w) for w in widths],
        out_shape=[jax.ShapeDtypeStruct((t, D_MODEL), F32)]
                  + [jax.ShapeDtypeStruct((t, w), BF16) for w in widths],
        compiler_params=pltpu.CompilerParams(dimension_semantics=("arbitrary",),
                                             vmem_limit_bytes=VMEM_LIMIT_BYTES),
        name="ffn1_ln1_inproj",
    )(x2d, wg, wu, wd, g1, b1, win_a)


NA_ROW_OFFSETS = 2 * NA_WIN_H - 1


def _na_bias_tiles(rpb):
    kw = NA_WIN_W
    c = np.arange(GRID_W)[:, None]
    kc = np.arange(GRID_W)[None, :]
    c_start = np.clip(c - kw // 2, 0, GRID_W - kw)
    col_ok = (kc >= c_start) & (kc < c_start + kw)
    dc = kc - c + (kw - 1)
    sel_c = ((dc[..., None] == np.arange(2 * kw - 1)) & col_ok[..., None]).astype(np.float32)
    tiles = jnp.einsum('hdw,ckw->hdck', rpb, sel_c, precision=lax.Precision.HIGHEST)
    tiles = jnp.where(col_ok, tiles * LOG2E, NEG_BIG)
    tiles = jnp.concatenate([tiles, jnp.full_like(tiles[:, :1], NEG_BIG)], axis=1)
    return jnp.concatenate([tiles, tiles], axis=-1).astype(F32)


def _na_row_offsets(rows):
    kh = NA_WIN_H
    ri = np.arange(NA_RQ)[:, None]
    bj = np.arange(NA_BAND)[None, :]
    out = []
    for r0, b0 in ((0, 0), (NA_RQ, 0), (rows - NA_RQ, rows - NA_BAND)):
        r = r0 + ri
        kr = b0 + bj
        r_start = np.clip(r - kh // 2, 0, rows - kh)
        row_ok = (kr >= r_start) & (kr < r_start + kh)
        out.append(np.where(row_ok, kr - r + (kh - 1), NA_ROW_OFFSETS).tolist())
    return out


def _na_body(q_ref, k_ref, v_ref, tile_ref, o_ref, va_ref, vb_ref, s_ref, bias_ref, *, rows):
    nblk = rows // NA_RQ
    lane = lax.broadcasted_iota(jnp.int32, (1, LANES), 1)
    first_head = lane < NA_HEAD_DIM

    @pl.when(pl.program_id(1) == 0)
    def _():
        for kind, offsets in enumerate(_na_row_offsets(rows)):
            for hh in range(2):
                for ri in range(NA_RQ):
                    for bj in range(0, NA_BAND, 2):
                        tile = jnp.where(first_head, tile_ref[hh, offsets[ri][bj]], tile_ref[hh, offsets[ri][bj + 1]])
                        bias_ref[kind, pl.ds(hh * NA_QT + ri * GRID_W, GRID_W), pl.ds(bj * GRID_W, LANES)] = tile

    def augment(t, carry):
        r = pl.ds(pl.multiple_of(t * NA_QT, NA_QT), NA_QT)
        v = v_ref[0, r, :]
        one = jnp.ones((), BF16)
        va_ref[r, :] = jnp.where(first_head, v, one)
        vb_ref[r, :] = jnp.where(first_head, one, v)
        return carry

    lax.fori_loop(0, nblk, augment, 0)

    def band_start(i):
        band0 = jnp.clip(NA_RQ * i - NA_WIN_H // 2, 0, rows - NA_BAND)
        return pl.multiple_of(band0 * GRID_W, GRID_W)

    def scores(i, slot):
        q0 = pl.multiple_of(i * NA_QT, NA_QT)
        kind = jnp.where(i == 0, 0, jnp.where(i == nblk - 1, 2, 1))
        q = q_ref[0, pl.ds(q0, NA_QT), :]
        k = k_ref[0, pl.ds(band_start(i), NA_KT), :]
        zero = jnp.zeros((), BF16)
        q2 = jnp.concatenate([jnp.where(first_head, q, zero), jnp.where(first_head, zero, q)], axis=0)
        s = lax.dot_general(q2, k, (((1,), (1,)), ((), ())), preferred_element_type=F32)
        s_ref[slot] = s + bias_ref[kind]

    def finish(i, slot):
        q0 = pl.multiple_of(i * NA_QT, NA_QT)
        k0 = band_start(i)
        s = s_ref[slot]
        p = jnp.exp2(s - jnp.max(s, axis=-1, keepdims=True)).astype(BF16)
        oa = jnp.dot(p[:NA_QT], va_ref[pl.ds(k0, NA_KT), :], preferred_element_type=F32)
        ob = jnp.dot(p[NA_QT:], vb_ref[pl.ds(k0, NA_KT), :], preferred_element_type=F32)
        num = jnp.where(first_head, oa, ob)
        den = pltpu.roll(jnp.where(first_head, ob, oa), NA_HEAD_DIM, axis=1)
        o_ref[0, pl.ds(q0, NA_QT), :] = (num / den).astype(BF16)

    scores(0, 0)

    def pair(j, carry):
        i = 2 * j
        scores(i + 1, 1)
        finish(i, 0)
        scores(jnp.minimum(i + 2, nblk - 1), 0)
        finish(i + 1, 1)
        return carry

    lax.fori_loop(0, nblk // 2, pair, 0)


def _neighborhood_attention(naqkv, tiles, rows):
    b, s, _ = naqkv.shape
    npairs = NA_HEADS // 2
    tok = lambda off: pl.BlockSpec((1, s, LANES), lambda p, bi: (bi, 0, off + p))
    return pl.pallas_call(
        functools.partial(_na_body, rows=rows),
        grid=(npairs, b),
        in_specs=[tok(0), tok(npairs), tok(2 * npairs),
                  pl.BlockSpec((2, NA_ROW_OFFSETS + 1, GRID_W, LANES), lambda p, bi: (p, 0, 0, 0))],
        out_specs=tok(0),
        out_shape=jax.ShapeDtypeStruct((b, s, NA_WIDTH), BF16),
        scratch_shapes=[pltpu.VMEM((s, LANES), BF16), pltpu.VMEM((s, LANES), BF16),
                        pltpu.VMEM((2, 2 * NA_QT, NA_KT), F32), pltpu.VMEM((3, 2 * NA_QT, NA_KT), F32)],
        compiler_params=pltpu.CompilerParams(dimension_semantics=("arbitrary", "arbitrary"),
                                             vmem_limit_bytes=VMEM_LIMIT_BYTES),
        name="neighborhood_attention",
    )(naqkv, naqkv, naqkv, tiles)


def _split3(x):
    top16 = lambda a: lax.bitcast_convert_type(
        lax.bitcast_convert_type(a, jnp.int32) & jnp.int32(-65536), F32)
    hi = top16(x)
    r1 = x - hi
    mid = top16(r1)
    lo = r1 - mid
    return hi.astype(BF16), mid.astype(BF16), lo.astype(BF16)


def _gla_cumsum_matrix(fwd, nchunks):
    G = nchunks * GLA_CHUNK
    ci = lax.broadcasted_iota(jnp.int32, (G, G), 0)
    si = lax.broadcasted_iota(jnp.int32, (G, G), 1)
    in_order = (si <= ci) if fwd else (si >= ci)
    return jnp.where(in_order & ((ci // GLA_CHUNK) == (si // GLA_CHUNK)), 1.0, 0.0).astype(BF16)


def _gla_group(qk_ref, vg_ref, dl_ref, wdec_ref, bdec_ref, tri, states, r0, fwd, nchunks):
    C, KW, H = GLA_CHUNK, GLA_KEY_WIDTH, GLA_HEADS
    G = nchunks * C
    nseq = len(states)
    d = 0 if fwd else 1
    rows = pl.ds(r0, G)
    last = C - 1 if fwd else 0
    chunk = lambda a, u: a[u * C:(u + 1) * C]
    nt = (((1,), (1,)), ((), ()))

    zs = [(jnp.dot(dl_ref[b, rows, :], wdec_ref[d], preferred_element_type=F32) + bdec_ref[d:d + 1, :]) * LOG2E
          for b in range(nseq)]
    log_a = [(jnp.minimum(x, 0.0) - jnp.log2(1.0 + jnp.exp2(-jnp.abs(x)))) * (1.0 / GLA_GATE_TAU) for x in zs]

    bc = [jnp.dot(tri, jnp.concatenate(_split3(la), axis=1), preferred_element_type=F32) for la in log_a]
    b_cum = [(x[:, 0:KW] + x[:, KW:2 * KW]) + x[:, 2 * KW:3 * KW] for x in bc]
    b_last_rows = [[bcb[u * C + last:u * C + last + 1, :] for u in range(nchunks)] for bcb in b_cum]
    b_last = [jnp.concatenate([jnp.broadcast_to(r, (C, KW)) for r in rws], axis=0) for rws in b_last_rows]

    lane_head = lax.broadcasted_iota(jnp.int32, (1, KW), 1) // GLA_DK
    zero = jnp.zeros((), BF16)
    block_diag = lambda a: jnp.concatenate([jnp.where(lane_head == h, a, zero) for h in range(H)], axis=0)
    ci = lax.broadcasted_iota(jnp.int32, (C, C), 0)
    si = lax.broadcasted_iota(jnp.int32, (C, C), 1)
    causal = (si <= ci) if fwd else (si > ci)

    q_bd, k_t, kend_bd, v_heads = [], [], [], []
    for b in range(nseq):
        q = qk_ref[b, rows, 0:KW].astype(F32)
        k = qk_ref[b, rows, KW:2 * KW].astype(F32)
        v = vg_ref[b, rows, 0:GLA_VAL_WIDTH]
        q_t = ((q * (GLA_DK ** -0.5)) * jnp.exp2(b_cum[b])).astype(BF16)
        k_t.append((k * jnp.exp2(-b_cum[b])).astype(BF16))
        k_end = (k * jnp.exp2(b_last[b] - b_cum[b])).astype(BF16)
        q_bd.append([block_diag(chunk(q_t, u)) for u in range(nchunks)])
        kend_bd.append([block_diag(chunk(k_end, u)) for u in range(nchunks)])
        v_heads.append([[chunk(v, u)[:, h * GLA_DV:(h + 1) * GLA_DV] for h in range(H)] for u in range(nchunks)])

    units = [(b, u) for b in range(nseq) for u in range(nchunks)]
    attn = {bu: lax.dot_general(q_bd[bu[0]][bu[1]], chunk(k_t[bu[0]], bu[1]), nt, preferred_element_type=F32)
            for bu in units}
    attn = {bu: jnp.concatenate([jnp.where(causal, chunk(a, h), 0.0) for h in range(H)], axis=0).astype(BF16)
            for bu, a in attn.items()}
    st_add = {(b, u): lax.dot_general(jnp.concatenate(v_heads[b][u], axis=0), kend_bd[b][u],
                                      (((0,), (0,)), ((), ())), preferred_element_type=F32) for b, u in units}
    o_intra = {(b, u): jnp.concatenate([jnp.dot(chunk(attn[b, u], h), v_heads[b][u][h],
                                                preferred_element_type=F32) for h in range(H)], axis=0)
               for b, u in units}
    outs = [[None] * nchunks for _ in range(nseq)]
    states = list(states)
    for u in (range(nchunks) if fwd else reversed(range(nchunks))):
        for b in range(nseq):
            st = states[b]
            outs[b][u] = o_intra[b, u] + lax.dot_general(q_bd[b][u], st.astype(BF16), nt,
                                                         preferred_element_type=F32)
            states[b] = jnp.exp2(b_last_rows[b][u]) * st + st_add[b, u]
    return outs, states


def _gla_body(qk_ref, vg_ref, dl_ref, wdec_ref, bdec_ref, ng_ref, o_ref, st_ref, ob_ref, *, cbk, nbk):
    C, H = GLA_CHUNK, GLA_HEADS
    nseq = qk_ref.shape[0]
    ph = pl.program_id(1)
    j = pl.program_id(2)
    chunks_per_block = cbk // C
    nchunks = min(GLA_GROUP, chunks_per_block)
    ngroups = chunks_per_block // nchunks

    @pl.when(j == 0)
    def _():
        st_ref[...] = jnp.zeros_like(st_ref)

    def sweep(fwd):
        blk = j if fwd else nbk - 1 - j
        tri = _gla_cumsum_matrix(fwd, nchunks)

        def group(gi, carry):
            g0 = gi if fwd else ngroups - 1 - gi
            r0 = pl.multiple_of(g0 * nchunks * C, nchunks * C)
            outs, states = _gla_group(qk_ref, vg_ref, dl_ref, wdec_ref, bdec_ref, tri,
                                      [st_ref[b] for b in range(nseq)], r0, fwd, nchunks)
            for b in range(nseq):
                st_ref[b] = states[b]
                for u in range(nchunks):
                    cg = blk * chunks_per_block + g0 * nchunks + u
                    if not fwd:
                        ob_ref[b, cg] = outs[b][u]
                        continue
                    c0 = pl.multiple_of(r0 + u * C, C)
                    o = outs[b][u] + ob_ref[b, cg]
                    o = o * lax.rsqrt(jnp.mean(o * o, axis=-1, keepdims=True) + RMS_EPS) * ng_ref[...]
                    gate = vg_ref[b, pl.ds(c0, C), GLA_VAL_WIDTH:2 * GLA_VAL_WIDTH].astype(F32)
                    gate = jnp.concatenate([gate[:, h * GLA_DV:(h + 1) * GLA_DV] for h in range(H)], axis=0)
                    o = o * (gate * jax.nn.sigmoid(gate))
                    o_ref[b, pl.ds(c0, C), :] = jnp.concatenate(
                        [o[h * C:(h + 1) * C] for h in range(H)], axis=1).astype(BF16)
            return carry

        lax.fori_loop(0, ngroups, group, 0)

    @pl.when(ph == 0)
    def _():
        sweep(False)

    @pl.when(ph == 1)
    def _():
        sweep(True)


def _gla(gqk, gvg, gdl, wdec_pad, bdec, norm_g):
    b, s, _ = gqk.shape
    cbk = min(GLA_BLOCK, s)
    nbk = s // cbk
    nseq = GLA_SEQS if b % GLA_SEQS == 0 else 1
    blk = lambda w: pl.BlockSpec((nseq, cbk, w), lambda bi, ph, j: (bi, jnp.where(ph == 0, nbk - 1 - j, j), 0))
    return pl.pallas_call(
        functools.partial(_gla_body, cbk=cbk, nbk=nbk),
        grid=(b // nseq, 2, nbk),
        in_specs=[blk(2 * GLA_KEY_WIDTH), blk(2 * GLA_VAL_WIDTH), blk(2 * GLA_GATE_RANK),
                  _const_spec(wdec_pad.shape), _const_spec(bdec.shape), _const_spec(norm_g.shape)],
        out_specs=pl.BlockSpec((nseq, cbk, GLA_VAL_WIDTH), lambda bi, ph, j: (bi, jnp.where(ph == 0, 0, j), 0)),
        out_shape=jax.ShapeDtypeStruct((b, s, GLA_VAL_WIDTH), BF16),
        scratch_shapes=[pltpu.VMEM((nseq, GLA_DV, GLA_KEY_WIDTH), F32),
                        pltpu.VMEM((nseq, s // GLA_CHUNK, GLA_HEADS * GLA_CHUNK, GLA_DV), F32)],
        compiler_params=pltpu.CompilerParams(dimension_semantics=("arbitrary", "arbitrary", "arbitrary"),
                                             vmem_limit_bytes=VMEM_LIMIT_BYTES),
        name="gla_bidirectional",
    )(gqk, gvg, gdl, wdec_pad, bdec, norm_g)


def _merge_ffn_body(x1_ref, na_ref, gla_ref, wgate_ref, wbn_ref, wbg_ref, wo_ref, g2_ref, b2_ref,
                    wg_ref, wu_ref, wd_ref, g3_ref, b3_ref, o_ref):
    tm = x1_ref.shape[0]
    subs = [pl.ds(r, tm // SUB_TILES) for r in range(0, tm, tm // SUB_TILES)]
    dot = functools.partial(jnp.dot, preferred_element_type=F32)
    x1s = [x1_ref[rows, :] for rows in subs]
    pre = [(dot(x1.astype(BF16), wgate_ref[...]), dot(na_ref[rows, :], wbn_ref[...]),
            dot(gla_ref[rows, :], wbg_ref[...])) for rows, x1 in zip(subs, x1s)]
    z2s = []
    for x1, (gates, y_na, y_gla) in zip(x1s, pre):
        merged = jax.nn.sigmoid(gates[:, :D_MODEL]) * y_na + jax.nn.sigmoid(gates[:, D_MODEL:]) * y_gla
        z2s.append(ALPHA * x1 + dot(merged.astype(BF16), wo_ref[...]))
    x2s = [_layer_norm(z2, g2_ref[...], b2_ref[...]) for z2 in z2s]
    z3s = [ALPHA * x2 + 0.5 * _swiglu(x2.astype(BF16), wg_ref, wu_ref, wd_ref) for x2 in x2s]
    for rows, z3 in zip(subs, z3s):
        o_ref[rows, :] = _layer_norm(z3, g3_ref[...], b3_ref[...])


def _merge_ffn(x1, na_o, gla_o, wgate, wbn, wbg, wo, g2, b2, wg, wu, wd, g3, b3):
    t = x1.shape[0]
    tm = min(TOKEN_TILE, t)
    row = lambda w: pl.BlockSpec((tm, w), lambda i: (i, 0))
    consts = (wgate, wbn, wbg, wo, g2, b2, wg, wu, wd, g3, b3)
    return pl.pallas_call(
        _merge_ffn_body,
        grid=(t // tm,),
        in_specs=[row(D_MODEL), row(NA_WIDTH), row(GLA_VAL_WIDTH)] + [_const_spec(c.shape) for c in consts],
        out_specs=row(D_MODEL),
        out_shape=jax.ShapeDtypeStruct((t, D_MODEL), F32),
        compiler_params=pltpu.CompilerParams(dimension_semantics=("arbitrary",),
                                             vmem_limit_bytes=VMEM_LIMIT_BYTES),
        name="merge_ln2_ffn2_ln3",
    )(x1, na_o, gla_o, *consts)


def _layer(x, ffn1_w_gate, ffn1_w_up, ffn1_w_down, ln1_g, ln1_b, w_in, na_rpb, gla_w_dec2, gla_b_dec,
           gla_norm_g, w_branch_na, w_branch_gla, w_out, ln2_g, ln2_b, ffn2_w_gate, ffn2_w_up, ffn2_w_down,
           ln3_g, ln3_b):
    b, s, d = x.shape
    assert d == D_MODEL and s % (2 * NA_QT) == 0 and s // GRID_W >= NA_BAND
    rows = s // GRID_W
    bf = lambda w: w.astype(BF16)
    vec = lambda p: p.reshape(1, -1)

    x1, naqkv, gqk, gvg, gdl = _ffn_inproj(
        x.reshape(b * s, d), bf(ffn1_w_gate), bf(ffn1_w_up), bf(ffn1_w_down), vec(ln1_g), vec(ln1_b),
        bf(w_in[:, :OFF_GDL_END]))

    na_o = _neighborhood_attention(naqkv.reshape(b, s, -1), _na_bias_tiles(na_rpb), rows)

    zr = jnp.zeros((GLA_GATE_RANK, GLA_KEY_WIDTH), gla_w_dec2.dtype)
    wdec_pad = bf(jnp.stack([jnp.concatenate([gla_w_dec2[0], zr]), jnp.concatenate([zr, gla_w_dec2[1]])]))
    gla_o = _gla(gqk.reshape(b, s, -1), gvg.reshape(b, s, -1), gdl.reshape(b, s, -1),
                 wdec_pad, gla_b_dec, vec(gla_norm_g))

    out = _merge_ffn(x1, na_o.reshape(b * s, -1), gla_o.reshape(b * s, -1), bf(w_in[:, OFF_GDL_END:]),
                     bf(w_branch_na), bf(w_branch_gla), bf(w_out), vec(ln2_g), vec(ln2_b),
                     bf(ffn2_w_gate), bf(ffn2_w_up), bf(ffn2_w_down), vec(ln3_g), vec(ln3_b))
    return out.reshape(b, s, d)


def kernel(x, ffn1_w_gate, ffn1_w_up, ffn1_w_down, ln1_g, ln1_b, w_in, na_rpb, gla_w_dec2, gla_b_dec, gla_norm_g,
           w_branch_na, w_branch_gla, w_out, ln2_g, ln2_b, ffn2_w_gate, ffn2_w_up, ffn2_w_down, ln3_g, ln3_b):
    params = (ffn1_w_gate, ffn1_w_up, ffn1_w_down, ln1_g, ln1_b, w_in, na_rpb, gla_w_dec2, gla_b_dec, gla_norm_g,
              w_branch_na, w_branch_gla, w_out, ln2_g, ln2_b, ffn2_w_gate, ffn2_w_up, ffn2_w_down, ln3_g, ln3_b)
    for l in range(DEPTH):
        x = _layer(x, *(p[l] for p in params))
    return x
```

```python
import functools

import jax
import jax.numpy as jnp
import numpy as np
from jax import lax
from jax.experimental import pallas as pl
from jax.experimental.pallas import tpu as pltpu

F32 = jnp.float32
BF16 = jnp.bfloat16

D_MODEL = 1024
D_FF = 2816
GRID_W = 64
NA_HEADS = 8
NA_HEAD_DIM = 64
NA_WIDTH = NA_HEADS * NA_HEAD_DIM
NA_WIN_H = 8
NA_WIN_W = 16
GLA_HEADS = 4
GLA_DK = 64
GLA_DV = 128
GLA_KEY_WIDTH = GLA_HEADS * GLA_DK
GLA_VAL_WIDTH = GLA_HEADS * GLA_DV
GLA_GATE_RANK = 16
GLA_GATE_TAU = 16.0
GLA_CHUNK = 64
DEPTH = 1
ALPHA = (2 * DEPTH) ** 0.25
LN_EPS = 1e-5
RMS_EPS = 1e-6

OFF_NA_END = 3 * NA_WIDTH
OFF_GQK_END = OFF_NA_END + 2 * GLA_KEY_WIDTH
OFF_GVG_END = OFF_GQK_END + 2 * GLA_VAL_WIDTH
OFF_GDL_END = OFF_GVG_END + 2 * GLA_GATE_RANK
N_IN = OFF_GDL_END + 2 * D_MODEL

LANES = 128
VMEM_LIMIT_BYTES = 56 * 1024 * 1024

NA_RQ = 4
NA_BAND = NA_RQ + NA_WIN_H
NA_QT = NA_RQ * GRID_W
NA_KT = NA_BAND * GRID_W
NA_UNROLL = 8
NA_ONES_ROWS = 16
NEG_BIG = -1e30
LOG2E = 1.4426950408889634
NA_Q_SCALE = LOG2E * NA_HEAD_DIM ** -0.5

GLA_GROUP = 4
GLA_BLOCK = 1024
GLA_SEQS = 2
TOKEN_TILE = 512
SUB_TILES = 2


def _const_spec(shape):
    nd = len(shape)
    return pl.BlockSpec(shape, lambda *_: (0,) * nd, pipeline_mode=pl.Buffered(1))


def _layer_norm(z, g, b):
    mu = jnp.mean(z, axis=-1, keepdims=True)
    zc = z - mu
    var = jnp.mean(zc * zc, axis=-1, keepdims=True)
    return zc * lax.rsqrt(var + LN_EPS) * g + b


def _swiglu(xb, wg_ref, wu_ref, wd_ref):
    g = jnp.dot(xb, wg_ref[...], preferred_element_type=F32)
    u = jnp.dot(xb, wu_ref[...], preferred_element_type=F32)
    h = (g * jax.nn.sigmoid(g)) * u
    return jnp.dot(h.astype(BF16), wd_ref[...], preferred_element_type=F32)


def _ffn_inproj_body(x_ref, wg_ref, wu_ref, wd_ref, g1_ref, b1_ref, win_ref,
                     x1_ref, naqk_ref, navt_ref, gqk_ref, gvg_ref, gdl_ref):
    tm = x_ref.shape[0]
    subs = [pl.ds(r, tm // SUB_TILES) for r in range(0, tm, tm // SUB_TILES)]
    zs = []
    for rows in subs:
        x = x_ref[rows, :]
        zs.append(ALPHA * x + 0.5 * _swiglu(x.astype(BF16), wg_ref, wu_ref, wd_ref))
    for g, (rows, z) in enumerate(zip(subs, zs)):
        x1 = _layer_norm(z, g1_ref[...], b1_ref[...])
        x1_ref[rows, :] = x1
        proj = jnp.dot(x1.astype(BF16), win_ref[...], preferred_element_type=F32)
        naqk_ref[rows, :NA_WIDTH] = (proj[:, :NA_WIDTH] * NA_Q_SCALE).astype(BF16)
        naqk_ref[rows, NA_WIDTH:] = proj[:, NA_WIDTH:2 * NA_WIDTH].astype(BF16)
        navt_ref[g] = proj[:, 2 * NA_WIDTH:OFF_NA_END].T.astype(BF16)
        gqk_ref[rows, :] = proj[:, OFF_NA_END:OFF_GQK_END].astype(BF16)
        gvg_ref[rows, :] = proj[:, OFF_GQK_END:OFF_GVG_END].astype(BF16)
        gdl_ref[rows, :] = proj[:, OFF_GVG_END:OFF_GDL_END].astype(BF16)


def _ffn_inproj(x2d, wg, wu, wd, g1, b1, win_a):
    t = x2d.shape[0]
    tm = min(TOKEN_TILE, t)
    assert tm // SUB_TILES == NA_QT
    row = lambda w: pl.BlockSpec((tm, w), lambda i: (i, 0))
    widths = (2 * NA_WIDTH, OFF_GQK_END - OFF_NA_END, OFF_GVG_END - OFF_GQK_END, OFF_GDL_END - OFF_GVG_END)
    rows_bf16 = [jax.ShapeDtypeStruct((t, w), BF16) for w in widths]
    return pl.pallas_call(
        _ffn_inproj_body,
        grid=(t // tm,),
        in_specs=[row(D_MODEL), _const_spec(wg.shape), _const_spec(wu.shape), _const_spec(wd.shape),
                  _const_spec(g1.shape), _const_spec(b1.shape), _const_spec(win_a.shape)],
        out_specs=[row(D_MODEL), row(widths[0]), pl.BlockSpec((SUB_TILES, NA_WIDTH, NA_QT), lambda i: (i, 0, 0))]
                  + [row(w) for w in widths[1:]],
        out_shape=[jax.ShapeDtypeStruct((t, D_MODEL), F32), rows_bf16[0],
                   jax.ShapeDtypeStruct((t // NA_QT, NA_WIDTH, NA_QT), BF16)] + rows_bf16[1:],
        compiler_params=pltpu.CompilerParams(dimension_semantics=("arbitrary",),
                                             vmem_limit_bytes=VMEM_LIMIT_BYTES),
        name="ffn1_ln1_inproj",
    )(x2d, wg, wu, wd, g1, b1, win_a)


NA_ROW_OFFSETS = 2 * NA_WIN_H - 1


def _na_bias_tiles(rpb):
    kw = NA_WIN_W
    c = np.arange(GRID_W)[None, :]
    kc = np.arange(GRID_W)[:, None]
    c_start = np.clip(c - kw // 2, 0, GRID_W - kw)
    col_ok = (kc >= c_start) & (kc < c_start + kw)
    dc = kc - c + (kw - 1)
    sel_c = ((dc[..., None] == np.arange(2 * kw - 1)) & col_ok[..., None]).astype(np.float32)
    tiles = jnp.einsum('hdw,kcw->hdkc', rpb, sel_c, precision=lax.Precision.HIGHEST)
    tiles = jnp.where(col_ok, tiles * LOG2E, NEG_BIG)
    tiles = jnp.concatenate([tiles, jnp.full_like(tiles[:, :1], NEG_BIG)], axis=1)
    return jnp.concatenate([tiles, tiles], axis=-1).astype(F32)


def _na_row_offsets(rows):
    kh = NA_WIN_H
    ri = np.arange(NA_RQ)[:, None]
    bj = np.arange(NA_BAND)[None, :]
    out = []
    for r0, b0 in ((0, 0), (NA_RQ, 0), (rows - NA_RQ, rows - NA_BAND)):
        r = r0 + ri
        kr = b0 + bj
        r_start = np.clip(r - kh // 2, 0, rows - kh)
        row_ok = (kr >= r_start) & (kr < r_start + kh)
        out.append(np.where(row_ok, kr - r + (kh - 1), NA_ROW_OFFSETS).tolist())
    return out


def _na_body(q_ref, k_ref, vt_ref, tile_ref, o_ref, s_ref, bias_ref, *, rows):
    nblk = rows // NA_RQ
    lane = lax.broadcasted_iota(jnp.int32, (1, LANES), 1)
    first_head = lane < NA_HEAD_DIM

    @pl.when(pl.program_id(1) == 0)
    def _():
        for kind, offsets in enumerate(_na_row_offsets(rows)):
            for hh in range(2):
                for bj in range(NA_BAND):
                    for ri in range(0, NA_RQ, 2):
                        tile = jnp.where(first_head, tile_ref[hh, offsets[ri][bj]], tile_ref[hh, offsets[ri + 1][bj]])
                        bias_ref[kind, pl.ds(bj * GRID_W, GRID_W), pl.ds(hh * NA_QT + ri * GRID_W, LANES)] = tile

    def band_row(i):
        return jnp.clip(NA_RQ * i - NA_WIN_H // 2, 0, rows - NA_BAND)

    def scores(i, slot):
        q0 = pl.multiple_of(i * NA_QT, NA_QT)
        k0 = pl.multiple_of(band_row(i) * GRID_W, NA_QT)
        kind = jnp.where(i == 0, 0, jnp.where(i == nblk - 1, 2, 1))
        q = q_ref[0, pl.ds(q0, NA_QT), :]
        k = k_ref[0, pl.ds(k0, NA_KT), :]
        zero = jnp.zeros((), BF16)
        q2 = jnp.concatenate([jnp.where(first_head, q, zero), jnp.where(first_head, zero, q)], axis=0)
        s = lax.dot_general(k, q2, (((1,), (1,)), ((), ())), preferred_element_type=F32)
        s_ref[slot] = s + bias_ref[kind]

    def finish(i, slot):
        q0 = pl.multiple_of(i * NA_QT, NA_QT)
        g0 = band_row(i) // NA_RQ
        s = s_ref[slot]
        p = jnp.exp2(s - jnp.max(s, axis=0, keepdims=True)).astype(BF16)
        ones = jnp.ones((NA_ONES_ROWS, NA_QT), BF16)
        heads = []
        for hh in range(2):
            acc = None
            for g in range(NA_BAND // NA_RQ):
                vt = jnp.concatenate([vt_ref[g0 + g, hh * NA_HEAD_DIM:(hh + 1) * NA_HEAD_DIM, :], ones], axis=0)
                part = jnp.dot(vt, p[g * NA_QT:(g + 1) * NA_QT, hh * NA_QT:(hh + 1) * NA_QT],
                               preferred_element_type=F32)
                acc = part if acc is None else acc + part
            heads.append(acc[:NA_HEAD_DIM] * (1.0 / acc[NA_HEAD_DIM:NA_HEAD_DIM + 1]))
        o_ref[0, pl.ds(q0, NA_QT), :] = jnp.concatenate(heads, axis=0).T.astype(BF16)

    scores(0, 0)

    def group(j, carry):
        for u in range(NA_UNROLL):
            i = NA_UNROLL * j + u
            scores(jnp.minimum(i + 1, nblk - 1), (u + 1) % 2)
            finish(i, u % 2)
        return carry

    lax.fori_loop(0, nblk // NA_UNROLL, group, 0)


def _neighborhood_attention(naqk, navt, tiles, rows):
    b, s, _ = naqk.shape
    npairs = NA_HEADS // 2
    tok = lambda off: pl.BlockSpec((1, s, LANES), lambda p, bi: (bi, 0, off + p))
    return pl.pallas_call(
        functools.partial(_na_body, rows=rows),
        grid=(npairs, b),
        in_specs=[tok(0), tok(npairs), pl.BlockSpec((s // NA_QT, LANES, NA_QT), lambda p, bi: (bi, p, 0)),
                  pl.BlockSpec((2, NA_ROW_OFFSETS + 1, GRID_W, LANES), lambda p, bi: (p, 0, 0, 0))],
        out_specs=tok(0),
        out_shape=jax.ShapeDtypeStruct((b, s, NA_WIDTH), BF16),
        scratch_shapes=[pltpu.VMEM((2, NA_KT, 2 * NA_QT), F32), pltpu.VMEM((3, NA_KT, 2 * NA_QT), F32)],
        compiler_params=pltpu.CompilerParams(dimension_semantics=("arbitrary", "arbitrary"),
                                             vmem_limit_bytes=VMEM_LIMIT_BYTES),
        name="neighborhood_attention",
    )(naqk, naqk, navt, tiles)


def _split3(x):
    top16 = lambda a: lax.bitcast_convert_type(
        lax.bitcast_convert_type(a, jnp.int32) & jnp.int32(-65536), F32)
    hi = top16(x)
    r1 = x - hi
    mid = top16(r1)
    lo = r1 - mid
    return hi.astype(BF16), mid.astype(BF16), lo.astype(BF16)


def _gla_cumsum_matrix(fwd, nchunks):
    G = nchunks * GLA_CHUNK
    ci = lax.broadcasted_iota(jnp.int32, (G, G), 0)
    si = lax.broadcasted_iota(jnp.int32, (G, G), 1)
    in_order = (si <= ci) if fwd else (si >= ci)
    return jnp.where(in_order & ((ci // GLA_CHUNK) == (si // GLA_CHUNK)), 1.0, 0.0).astype(BF16)


def _gla_group(qk_ref, vg_ref, dl_ref, wdec_ref, bdec_ref, tri, states, r0, fwd, nchunks):
    C, KW, H = GLA_CHUNK, GLA_KEY_WIDTH, GLA_HEADS
    G = nchunks * C
    nseq = len(states)
    d = 0 if fwd else 1
    rows = pl.ds(r0, G)
    last = C - 1 if fwd else 0
    chunk = lambda a, u: a[u * C:(u + 1) * C]
    nt = (((1,), (1,)), ((), ()))

    zs = [(jnp.dot(dl_ref[b, rows, :], wdec_ref[d], preferred_element_type=F32) + bdec_ref[d:d + 1, :]) * LOG2E
          for b in range(nseq)]
    log_a = [(jnp.minimum(x, 0.0) - jnp.log2(1.0 + jnp.exp2(-jnp.abs(x)))) * (1.0 / GLA_GATE_TAU) for x in zs]

    bc = [jnp.dot(tri, jnp.concatenate(_split3(la), axis=1), preferred_element_type=F32) for la in log_a]
    b_cum = [(x[:, 0:KW] + x[:, KW:2 * KW]) + x[:, 2 * KW:3 * KW] for x in bc]
    b_last_rows = [[bcb[u * C + last:u * C + last + 1, :] for u in range(nchunks)] for bcb in b_cum]
    b_last = [jnp.concatenate([jnp.broadcast_to(r, (C, KW)) for r in rws], axis=0) for rws in b_last_rows]

    lane_head = lax.broadcasted_iota(jnp.int32, (1, KW), 1) // GLA_DK
    zero = jnp.zeros((), BF16)
    block_diag = lambda a: jnp.concatenate([jnp.where(lane_head == h, a, zero) for h in range(H)], axis=0)
    ci = lax.broadcasted_iota(jnp.int32, (C, C), 0)
    si = lax.broadcasted_iota(jnp.int32, (C, C), 1)
    causal = (si <= ci) if fwd else (si > ci)

    q_bd, k_t, kend_bd, v_heads = [], [], [], []
    for b in range(nseq):
        q = qk_ref[b, rows, 0:KW].astype(F32)
        k = qk_ref[b, rows, KW:2 * KW].astype(F32)
        v = vg_ref[b, rows, 0:GLA_VAL_WIDTH]
        q_t = ((q * (GLA_DK ** -0.5)) * jnp.exp2(b_cum[b])).astype(BF16)
        k_t.append((k * jnp.exp2(-b_cum[b])).astype(BF16))
        k_end = (k * jnp.exp2(b_last[b] - b_cum[b])).astype(BF16)
        q_bd.append([block_diag(chunk(q_t, u)) for u in range(nchunks)])
        kend_bd.append([block_diag(chunk(k_end, u)) for u in range(nchunks)])
        v_heads.append([[chunk(v, u)[:, h * GLA_DV:(h + 1) * GLA_DV] for h in range(H)] for u in range(nchunks)])

    units = [(b, u) for b in range(nseq) for u in range(nchunks)]
    attn = {bu: lax.dot_general(q_bd[bu[0]][bu[1]], chunk(k_t[bu[0]], bu[1]), nt, preferred_element_type=F32)
            for bu in units}
    attn = {bu: jnp.concatenate([jnp.where(causal, chunk(a, h), 0.0) for h in range(H)], axis=0).astype(BF16)
            for bu, a in attn.items()}
    st_add = {(b, u): lax.dot_general(jnp.concatenate(v_heads[b][u], axis=0), kend_bd[b][u],
                                      (((0,), (0,)), ((), ())), preferred_element_type=F32) for b, u in units}
    o_intra = {(b, u): jnp.concatenate([jnp.dot(chunk(attn[b, u], h), v_heads[b][u][h],
                                                preferred_element_type=F32) for h in range(H)], axis=0)
               for b, u in units}
    outs = [[None] * nchunks for _ in range(nseq)]
    states = list(states)
    for u in (range(nchunks) if fwd else reversed(range(nchunks))):
        for b in range(nseq):
            st = states[b]
            outs[b][u] = o_intra[b, u] + lax.dot_general(q_bd[b][u], st.astype(BF16), nt,
                                                         preferred_element_type=F32)
            states[b] = jnp.exp2(b_last_rows[b][u]) * st + st_add[b, u]
    return outs, states


def _gla_body(qk_ref, vg_ref, dl_ref, wdec_ref, bdec_ref, ng_ref, o_ref, st_ref, ob_ref, *, cbk, nbk):
    C, H = GLA_CHUNK, GLA_HEADS
    nseq = qk_ref.shape[0]
    ph = pl.program_id(1)
    j = pl.program_id(2)
    chunks_per_block = cbk // C
    nchunks = min(GLA_GROUP, chunks_per_block)
    ngroups = chunks_per_block // nchunks

    @pl.when(j == 0)
    def _():
        st_ref[...] = jnp.zeros_like(st_ref)

    def sweep(fwd):
        blk = j if fwd else nbk - 1 - j
        tri = _gla_cumsum_matrix(fwd, nchunks)

        def group(gi, carry):
            g0 = gi if fwd else ngroups - 1 - gi
            r0 = pl.multiple_of(g0 * nchunks * C, nchunks * C)
            outs, states = _gla_group(qk_ref, vg_ref, dl_ref, wdec_ref, bdec_ref, tri,
                                      [st_ref[b] for b in range(nseq)], r0, fwd, nchunks)
            for b in range(nseq):
                st_ref[b] = states[b]
                for u in range(nchunks):
                    cg = blk * chunks_per_block + g0 * nchunks + u
                    if not fwd:
                        ob_ref[b, cg] = outs[b][u]
                        continue
                    c0 = pl.multiple_of(r0 + u * C, C)
                    o = outs[b][u] + ob_ref[b, cg]
                    o = o * lax.rsqrt(jnp.mean(o * o, axis=-1, keepdims=True) + RMS_EPS) * ng_ref[...]
                    gate = vg_ref[b, pl.ds(c0, C), GLA_VAL_WIDTH:2 * GLA_VAL_WIDTH].astype(F32)
                    gate = jnp.concatenate([gate[:, h * GLA_DV:(h + 1) * GLA_DV] for h in range(H)], axis=0)
                    o = o * (gate * jax.nn.sigmoid(gate))
                    o_ref[b, pl.ds(c0, C), :] = jnp.concatenate(
                        [o[h * C:(h + 1) * C] for h in range(H)], axis=1).astype(BF16)
            return carry

        lax.fori_loop(0, ngroups, group, 0)

    @pl.when(ph == 0)
    def _():
        sweep(False)

    @pl.when(ph == 1)
    def _():
        sweep(True)


def _gla(gqk, gvg, gdl, wdec_pad, bdec, norm_g):
    b, s, _ = gqk.shape
    cbk = min(GLA_BLOCK, s)
    nbk = s // cbk
    nseq = GLA_SEQS if b % GLA_SEQS == 0 else 1
    blk = lambda w: pl.BlockSpec((nseq, cbk, w), lambda bi, ph, j: (bi, jnp.where(ph == 0, nbk - 1 - j, j), 0))
    return pl.pallas_call(
        functools.partial(_gla_body, cbk=cbk, nbk=nbk),
        grid=(b // nseq, 2, nbk),
        in_specs=[blk(2 * GLA_KEY_WIDTH), blk(2 * GLA_VAL_WIDTH), blk(2 * GLA_GATE_RANK),
                  _const_spec(wdec_pad.shape), _const_spec(bdec.shape), _const_spec(norm_g.shape)],
        out_specs=pl.BlockSpec((nseq, cbk, GLA_VAL_WIDTH), lambda bi, ph, j: (bi, jnp.where(ph == 0, 0, j), 0)),
        out_shape=jax.ShapeDtypeStruct((b, s, GLA_VAL_WIDTH), BF16),
        scratch_shapes=[pltpu.VMEM((nseq, GLA_DV, GLA_KEY_WIDTH), F32),
                        pltpu.VMEM((nseq, s // GLA_CHUNK, GLA_HEADS * GLA_CHUNK, GLA_DV), F32)],
        compiler_params=pltpu.CompilerParams(dimension_semantics=("arbitrary", "arbitrary", "arbitrary"),
                                             vmem_limit_bytes=VMEM_LIMIT_BYTES),
        name="gla_bidirectional",
    )(gqk, gvg, gdl, wdec_pad, bdec, norm_g)


def _merge_ffn_body(x1_ref, na_ref, gla_ref, wgate_ref, wbn_ref, wbg_ref, wo_ref, g2_ref, b2_ref,
                    wg_ref, wu_ref, wd_ref, g3_ref, b3_ref, o_ref):
    tm = x1_ref.shape[0]
    subs = [pl.ds(r, tm // SUB_TILES) for r in range(0, tm, tm // SUB_TILES)]
    dot = functools.partial(jnp.dot, preferred_element_type=F32)
    x1s = [x1_ref[rows, :] for rows in subs]
    pre = [(dot(x1.astype(BF16), wgate_ref[...]), dot(na_ref[rows, :], wbn_ref[...]),
            dot(gla_ref[rows, :], wbg_ref[...])) for rows, x1 in zip(subs, x1s)]
    z2s = []
    for x1, (gates, y_na, y_gla) in zip(x1s, pre):
        merged = jax.nn.sigmoid(gates[:, :D_MODEL]) * y_na + jax.nn.sigmoid(gates[:, D_MODEL:]) * y_gla
        z2s.append(ALPHA * x1 + dot(merged.astype(BF16), wo_ref[...]))
    x2s = [_layer_norm(z2, g2_ref[...], b2_ref[...]) for z2 in z2s]
    z3s = [ALPHA * x2 + 0.5 * _swiglu(x2.astype(BF16), wg_ref, wu_ref, wd_ref) for x2 in x2s]
    for rows, z3 in zip(subs, z3s):
        o_ref[rows, :] = _layer_norm(z3, g3_ref[...], b3_ref[...])


def _merge_ffn(x1, na_o, gla_o, wgate, wbn, wbg, wo, g2, b2, wg, wu, wd, g3, b3):
    t = x1.shape[0]
    tm = min(TOKEN_TILE, t)
    row = lambda w: pl.BlockSpec((tm, w), lambda i: (i, 0))
    consts = (wgate, wbn, wbg, wo, g2, b2, wg, wu, wd, g3, b3)
    return pl.pallas_call(
        _merge_ffn_body,
        grid=(t // tm,),
        in_specs=[row(D_MODEL), row(NA_WIDTH), row(GLA_VAL_WIDTH)] + [_const_spec(c.shape) for c in consts],
        out_specs=row(D_MODEL),
        out_shape=jax.ShapeDtypeStruct((t, D_MODEL), F32),
        compiler_params=pltpu.CompilerParams(dimension_semantics=("arbitrary",),
                                             vmem_limit_bytes=VMEM_LIMIT_BYTES),
        name="merge_ln2_ffn2_ln3",
    )(x1, na_o, gla_o, *consts)


def _layer(x, ffn1_w_gate, ffn1_w_up, ffn1_w_down, ln1_g, ln1_b, w_in, na_rpb, gla_w_dec2, gla_b_dec,
           gla_norm_g, w_branch_na, w_branch_gla, w_out, ln2_g, ln2_b, ffn2_w_gate, ffn2_w_up, ffn2_w_down,
           ln3_g, ln3_b):
    b, s, d = x.shape
    assert d == D_MODEL and s % (NA_UNROLL * NA_QT) == 0 and s // GRID_W >= NA_BAND
    rows = s // GRID_W
    bf = lambda w: w.astype(BF16)
    vec = lambda p: p.reshape(1, -1)

    x1, naqk, navt, gqk, gvg, gdl = _ffn_inproj(
        x.reshape(b * s, d), bf(ffn1_w_gate), bf(ffn1_w_up), bf(ffn1_w_down), vec(ln1_g), vec(ln1_b),
        bf(w_in[:, :OFF_GDL_END]))

    na_o = _neighborhood_attention(naqk.reshape(b, s, -1), navt, _na_bias_tiles(na_rpb), rows)

    zr = jnp.zeros((GLA_GATE_RANK, GLA_KEY_WIDTH), gla_w_dec2.dtype)
    wdec_pad = bf(jnp.stack([jnp.concatenate([gla_w_dec2[0], zr]), jnp.concatenate([zr, gla_w_dec2[1]])]))
    gla_o = _gla(gqk.reshape(b, s, -1), gvg.reshape(b, s, -1), gdl.reshape(b, s, -1),
                 wdec_pad, gla_b_dec, vec(gla_norm_g))

    out = _merge_ffn(x1, na_o.reshape(b * s, -1), gla_o.reshape(b * s, -1), bf(w_in[:, OFF_GDL_END:]),
                     bf(w_branch_na), bf(w_branch_gla), bf(w_out), vec(ln2_g), vec(ln2_b),
                     bf(ffn2_w_gate), bf(ffn2_w_up), bf(ffn2_w_down), vec(ln3_g), vec(ln3_b))
    return out.reshape(b, s, d)


def kernel(x, ffn1_w_gate, ffn1_w_up, ffn1_w_down, ln1_g, ln1_b, w_in, na_rpb, gla_w_dec2, gla_b_dec, gla_norm_g,
           w_branch_na, w_branch_gla, w_out, ln2_g, ln2_b, ffn2_w_gate, ffn2_w_up, ffn2_w_down, ln3_g, ln3_b):
    params = (ffn1_w_gate, ffn1_w_up, ffn1_w_down, ln1_g, ln1_b, w_in, na_rpb, gla_w_dec2, gla_b_dec, gla_norm_g,
              w_branch_na, w_branch_gla, w_out, ln2_g, ln2_b, ffn2_w_gate, ffn2_w_up, ffn2_w_down, ln3_g, ln3_b)
    for l in range(DEPTH):
        x = _layer(x, *(p[l] for p in params))
    return x
```

```python
import functools

import jax
import jax.numpy as jnp
import numpy as np
from jax import lax
from jax.experimental import pallas as pl
from jax.experimental.pallas import tpu as pltpu

F32 = jnp.float32
BF16 = jnp.bfloat16

D_MODEL = 1024
D_FF = 2816
GRID_W = 64
NA_HEADS = 8
NA_HEAD_DIM = 64
NA_WIDTH = NA_HEADS * NA_HEAD_DIM
NA_WIN_H = 8
NA_WIN_W = 16
GLA_HEADS = 4
GLA_DK = 64
GLA_DV = 128
GLA_KEY_WIDTH = GLA_HEADS * GLA_DK
GLA_VAL_WIDTH = GLA_HEADS * GLA_DV
GLA_GATE_RANK = 16
GLA_GATE_TAU = 16.0
GLA_CHUNK = 64
DEPTH = 1
ALPHA = (2 * DEPTH) ** 0.25
LN_EPS = 1e-5
RMS_EPS = 1e-6

OFF_NA_END = 3 * NA_WIDTH
OFF_GQK_END = OFF_NA_END + 2 * GLA_KEY_WIDTH
OFF_GVG_END = OFF_GQK_END + 2 * GLA_VAL_WIDTH
OFF_GDL_END = OFF_GVG_END + 2 * GLA_GATE_RANK
N_IN = OFF_GDL_END + 2 * D_MODEL

LANES = 128
VMEM_LIMIT_BYTES = 56 * 1024 * 1024

NA_RQ = 4
NA_BAND = NA_RQ + NA_WIN_H
NA_QT = NA_RQ * GRID_W
NA_KT = NA_BAND * GRID_W
NA_UNROLL = 8
NA_ONES_ROWS = 16
NEG_BIG = -1e30
LOG2E = 1.4426950408889634
NA_Q_SCALE = LOG2E * NA_HEAD_DIM ** -0.5

GLA_GROUP = 4
GLA_BLOCK = 1024
GLA_SEQS = 2
TOKEN_TILE = 512
SUB_TILES = 2


def _const_spec(shape):
    nd = len(shape)
    return pl.BlockSpec(shape, lambda *_: (0,) * nd, pipeline_mode=pl.Buffered(1))


def _layer_norm(z, g, b):
    mu = jnp.mean(z, axis=-1, keepdims=True)
    zc = z - mu
    var = jnp.mean(zc * zc, axis=-1, keepdims=True)
    return zc * lax.rsqrt(var + LN_EPS) * g + b


def _swiglu(xb, wg_ref, wu_ref, wd_ref):
    g = jnp.dot(xb, wg_ref[...], preferred_element_type=F32)
    u = jnp.dot(xb, wu_ref[...], preferred_element_type=F32)
    h = (g * jax.nn.sigmoid(g)) * u
    return jnp.dot(h.astype(BF16), wd_ref[...], preferred_element_type=F32)


def _ffn_inproj_body(x_ref, wg_ref, wu_ref, wd_ref, g1_ref, b1_ref, win_ref,
                     x1_ref, nak_ref, naqt_ref, navt_ref, gqk_ref, gvg_ref, gdl_ref):
    tm = x_ref.shape[0]
    subs = [pl.ds(r, tm // SUB_TILES) for r in range(0, tm, tm // SUB_TILES)]
    zs = []
    for rows in subs:
        x = x_ref[rows, :]
        zs.append(ALPHA * x + 0.5 * _swiglu(x.astype(BF16), wg_ref, wu_ref, wd_ref))
    for g, (rows, z) in enumerate(zip(subs, zs)):
        x1 = _layer_norm(z, g1_ref[...], b1_ref[...])
        x1_ref[rows, :] = x1
        proj = jnp.dot(x1.astype(BF16), win_ref[...], preferred_element_type=F32)
        nak_ref[rows, :] = proj[:, NA_WIDTH:2 * NA_WIDTH].astype(BF16)
        naqt_ref[g] = (proj[:, :NA_WIDTH] * NA_Q_SCALE).T.astype(BF16)
        navt_ref[g] = proj[:, 2 * NA_WIDTH:OFF_NA_END].T.astype(BF16)
        gqk_ref[rows, :] = proj[:, OFF_NA_END:OFF_GQK_END].astype(BF16)
        gvg_ref[rows, :] = proj[:, OFF_GQK_END:OFF_GVG_END].astype(BF16)
        gdl_ref[rows, :] = proj[:, OFF_GVG_END:OFF_GDL_END].astype(BF16)


def _ffn_inproj(x2d, wg, wu, wd, g1, b1, win_a):
    t = x2d.shape[0]
    tm = min(TOKEN_TILE, t)
    assert tm // SUB_TILES == NA_QT
    row = lambda w: pl.BlockSpec((tm, w), lambda i: (i, 0))
    widths = (NA_WIDTH, OFF_GQK_END - OFF_NA_END, OFF_GVG_END - OFF_GQK_END, OFF_GDL_END - OFF_GVG_END)
    rows_bf16 = [jax.ShapeDtypeStruct((t, w), BF16) for w in widths]
    return pl.pallas_call(
        _ffn_inproj_body,
        grid=(t // tm,),
        in_specs=[row(D_MODEL), _const_spec(wg.shape), _const_spec(wu.shape), _const_spec(wd.shape),
                  _const_spec(g1.shape), _const_spec(b1.shape), _const_spec(win_a.shape)],
        out_specs=[row(D_MODEL), row(widths[0])] + [pl.BlockSpec((SUB_TILES, NA_WIDTH, NA_QT), lambda i: (i, 0, 0))] * 2
                  + [row(w) for w in widths[1:]],
        out_shape=[jax.ShapeDtypeStruct((t, D_MODEL), F32), rows_bf16[0]]
                  + [jax.ShapeDtypeStruct((t // NA_QT, NA_WIDTH, NA_QT), BF16)] * 2 + rows_bf16[1:],
        compiler_params=pltpu.CompilerParams(dimension_semantics=("arbitrary",),
                                             vmem_limit_bytes=VMEM_LIMIT_BYTES),
        name="ffn1_ln1_inproj",
    )(x2d, wg, wu, wd, g1, b1, win_a)


NA_ROW_OFFSETS = 2 * NA_WIN_H - 1


def _na_bias_tiles(rpb):
    kw = NA_WIN_W
    c = np.arange(GRID_W)[None, :]
    kc = np.arange(GRID_W)[:, None]
    c_start = np.clip(c - kw // 2, 0, GRID_W - kw)
    col_ok = (kc >= c_start) & (kc < c_start + kw)
    dc = kc - c + (kw - 1)
    sel_c = ((dc[..., None] == np.arange(2 * kw - 1)) & col_ok[..., None]).astype(np.float32)
    tiles = jnp.einsum('hdw,kcw->hdkc', rpb, sel_c, precision=lax.Precision.HIGHEST)
    tiles = jnp.where(col_ok, tiles * LOG2E, NEG_BIG)
    tiles = jnp.concatenate([tiles, jnp.full_like(tiles[:, :1], NEG_BIG)], axis=1)
    return jnp.concatenate([tiles, tiles], axis=-1).astype(F32)


def _na_row_offsets(rows):
    kh = NA_WIN_H
    ri = np.arange(NA_RQ)[:, None]
    bj = np.arange(NA_BAND)[None, :]
    out = []
    for r0, b0 in ((0, 0), (NA_RQ, 0), (rows - NA_RQ, rows - NA_BAND)):
        r = r0 + ri
        kr = b0 + bj
        r_start = np.clip(r - kh // 2, 0, rows - kh)
        row_ok = (kr >= r_start) & (kr < r_start + kh)
        out.append(np.where(row_ok, kr - r + (kh - 1), NA_ROW_OFFSETS).tolist())
    return out


def _na_body(qt_ref, k_ref, vt_ref, tile_ref, o_ref, s_ref, bias_ref, *, rows):
    nblk = rows // NA_RQ
    lane = lax.broadcasted_iota(jnp.int32, (1, LANES), 1)
    first_head = lane < NA_HEAD_DIM
    first_head_row = lax.broadcasted_iota(jnp.int32, (LANES, 1), 0) < NA_HEAD_DIM

    @pl.when(pl.program_id(1) == 0)
    def _():
        for kind, offsets in enumerate(_na_row_offsets(rows)):
            for hh in range(2):
                for bj in range(NA_BAND):
                    for ri in range(0, NA_RQ, 2):
                        tile = jnp.where(first_head, tile_ref[hh, offsets[ri][bj]], tile_ref[hh, offsets[ri + 1][bj]])
                        bias_ref[kind, pl.ds(bj * GRID_W, GRID_W), pl.ds(hh * NA_QT + ri * GRID_W, LANES)] = tile

    def band_row(i):
        return jnp.clip(NA_RQ * i - NA_WIN_H // 2, 0, rows - NA_BAND)

    def scores(i, slot):
        q0 = pl.multiple_of(i * NA_QT, NA_QT)
        k0 = pl.multiple_of(band_row(i) * GRID_W, NA_QT)
        kind = jnp.where(i == 0, 0, jnp.where(i == nblk - 1, 2, 1))
        qt = qt_ref[i]
        k = k_ref[0, pl.ds(k0, NA_KT), :]
        zero = jnp.zeros((), BF16)
        qt2 = jnp.concatenate([jnp.where(first_head_row, qt, zero), jnp.where(first_head_row, zero, qt)], axis=1)
        s = jnp.dot(k, qt2, preferred_element_type=F32)
        s_ref[slot] = s + bias_ref[kind]

    def finish(i, slot):
        q0 = pl.multiple_of(i * NA_QT, NA_QT)
        g0 = band_row(i) // NA_RQ
        s = s_ref[slot]
        p = jnp.exp2(s - jnp.max(s, axis=0, keepdims=True)).astype(BF16)
        ones = jnp.ones((NA_ONES_ROWS, NA_QT), BF16)
        heads = []
        for hh in range(2):
            acc = None
            for g in range(NA_BAND // NA_RQ):
                vt = jnp.concatenate([vt_ref[g0 + g, hh * NA_HEAD_DIM:(hh + 1) * NA_HEAD_DIM, :], ones], axis=0)
                part = jnp.dot(vt, p[g * NA_QT:(g + 1) * NA_QT, hh * NA_QT:(hh + 1) * NA_QT],
                               preferred_element_type=F32)
                acc = part if acc is None else acc + part
            heads.append(acc[:NA_HEAD_DIM] * (1.0 / acc[NA_HEAD_DIM:NA_HEAD_DIM + 1]))
        o_ref[0, pl.ds(q0, NA_QT), :] = jnp.concatenate(heads, axis=0).T.astype(BF16)

    scores(0, 0)

    def group(j, carry):
        for u in range(NA_UNROLL):
            i = NA_UNROLL * j + u
            scores(jnp.minimum(i + 1, nblk - 1), (u + 1) % 2)
            finish(i, u % 2)
        return carry

    lax.fori_loop(0, nblk // NA_UNROLL, group, 0)


def _neighborhood_attention(nak, naqt, navt, tiles, rows):
    b, s, _ = nak.shape
    npairs = NA_HEADS // 2
    tok = pl.BlockSpec((1, s, LANES), lambda p, bi: (bi, 0, p))
    slabs = pl.BlockSpec((s // NA_QT, LANES, NA_QT), lambda p, bi: (bi, p, 0))
    return pl.pallas_call(
        functools.partial(_na_body, rows=rows),
        grid=(npairs, b),
        in_specs=[slabs, tok, slabs,
                  pl.BlockSpec((2, NA_ROW_OFFSETS + 1, GRID_W, LANES), lambda p, bi: (p, 0, 0, 0))],
        out_specs=tok,
        out_shape=jax.ShapeDtypeStruct((b, s, NA_WIDTH), BF16),
        scratch_shapes=[pltpu.VMEM((2, NA_KT, 2 * NA_QT), F32), pltpu.VMEM((3, NA_KT, 2 * NA_QT), F32)],
        compiler_params=pltpu.CompilerParams(dimension_semantics=("arbitrary", "arbitrary"),
                                             vmem_limit_bytes=VMEM_LIMIT_BYTES),
        name="neighborhood_attention",
    )(naqt, nak, navt, tiles)


def _split3(x):
    top16 = lambda a: lax.bitcast_convert_type(
        lax.bitcast_convert_type(a, jnp.int32) & jnp.int32(-65536), F32)
    hi = top16(x)
    r1 = x - hi
    mid = top16(r1)
    lo = r1 - mid
    return hi.astype(BF16), mid.astype(BF16), lo.astype(BF16)


def _gla_cumsum_matrix(fwd, nchunks):
    G = nchunks * GLA_CHUNK
    ci = lax.broadcasted_iota(jnp.int32, (G, G), 0)
    si = lax.broadcasted_iota(jnp.int32, (G, G), 1)
    in_order = (si <= ci) if fwd else (si >= ci)
    return jnp.where(in_order & ((ci // GLA_CHUNK) == (si // GLA_CHUNK)), 1.0, 0.0).astype(BF16)


def _gla_group(qk_ref, vg_ref, dl_ref, wdec_ref, bdec_ref, tri, states, r0, fwd, nchunks):
    C, KW, H = GLA_CHUNK, GLA_KEY_WIDTH, GLA_HEADS
    G = nchunks * C
    nseq = len(states)
    d = 0 if fwd else 1
    rows = pl.ds(r0, G)
    last = C - 1 if fwd else 0
    chunk = lambda a, u: a[u * C:(u + 1) * C]
    nt = (((1,), (1,)), ((), ()))

    zs = [(jnp.dot(dl_ref[b, rows, :], wdec_ref[d], preferred_element_type=F32) + bdec_ref[d:d + 1, :]) * LOG2E
          for b in range(nseq)]
    log_a = [(jnp.minimum(x, 0.0) - jnp.log2(1.0 + jnp.exp2(-jnp.abs(x)))) * (1.0 / GLA_GATE_TAU) for x in zs]

    bc = [jnp.dot(tri, jnp.concatenate(_split3(la), axis=1), preferred_element_type=F32) for la in log_a]
    b_cum = [(x[:, 0:KW] + x[:, KW:2 * KW]) + x[:, 2 * KW:3 * KW] for x in bc]
    b_last_rows = [[bcb[u * C + last:u * C + last + 1, :] for u in range(nchunks)] for bcb in b_cum]
    b_last = [jnp.concatenate([jnp.broadcast_to(r, (C, KW)) for r in rws], axis=0) for rws in b_last_rows]

    lane_head = lax.broadcasted_iota(jnp.int32, (1, KW), 1) // GLA_DK
    zero = jnp.zeros((), BF16)
    block_diag = lambda a: jnp.concatenate([jnp.where(lane_head == h, a, zero) for h in range(H)], axis=0)
    ci = lax.broadcasted_iota(jnp.int32, (C, C), 0)
    si = lax.broadcasted_iota(jnp.int32, (C, C), 1)
    causal = (si <= ci) if fwd else (si > ci)

    q_bd, k_t, kend_bd, v_heads = [], [], [], []
    for b in range(nseq):
        q = qk_ref[b, rows, 0:KW].astype(F32)
        k = qk_ref[b, rows, KW:2 * KW].astype(F32)
        v = vg_ref[b, rows, 0:GLA_VAL_WIDTH]
        q_t = ((q * (GLA_DK ** -0.5)) * jnp.exp2(b_cum[b])).astype(BF16)
        k_t.append((k * jnp.exp2(-b_cum[b])).astype(BF16))
        k_end = (k * jnp.exp2(b_last[b] - b_cum[b])).astype(BF16)
        q_bd.append([block_diag(chunk(q_t, u)) for u in range(nchunks)])
        kend_bd.append([block_diag(chunk(k_end, u)) for u in range(nchunks)])
        v_heads.append([[chunk(v, u)[:, h * GLA_DV:(h + 1) * GLA_DV] for h in range(H)] for u in range(nchunks)])

    units = [(b, u) for b in range(nseq) for u in range(nchunks)]
    attn = {bu: lax.dot_general(q_bd[bu[0]][bu[1]], chunk(k_t[bu[0]], bu[1]), nt, preferred_element_type=F32)
            for bu in units}
    attn = {bu: jnp.concatenate([jnp.where(causal, chunk(a, h), 0.0) for h in range(H)], axis=0).astype(BF16)
            for bu, a in attn.items()}
    st_add = {(b, u): lax.dot_general(jnp.concatenate(v_heads[b][u], axis=0), kend_bd[b][u],
                                      (((0,), (0,)), ((), ())), preferred_element_type=F32) for b, u in units}
    o_intra = {(b, u): jnp.concatenate([jnp.dot(chunk(attn[b, u], h), v_heads[b][u][h],
                                                preferred_element_type=F32) for h in range(H)], axis=0)
               for b, u in units}
    outs = [[None] * nchunks for _ in range(nseq)]
    states = list(states)
    for u in (range(nchunks) if fwd else reversed(range(nchunks))):
        for b in range(nseq):
            st = states[b]
            outs[b][u] = o_intra[b, u] + lax.dot_general(q_bd[b][u], st.astype(BF16), nt,
                                                         preferred_element_type=F32)
            states[b] = jnp.exp2(b_last_rows[b][u]) * st + st_add[b, u]
    return outs, states


def _gla_body(qk_ref, vg_ref, dl_ref, wdec_ref, bdec_ref, ng_ref, o_ref, st_ref, ob_ref, *, cbk, nbk):
    C, H = GLA_CHUNK, GLA_HEADS
    nseq = qk_ref.shape[0]
    ph = pl.program_id(1)
    j = pl.program_id(2)
    chunks_per_block = cbk // C
    nchunks = min(GLA_GROUP, chunks_per_block)
    ngroups = chunks_per_block // nchunks

    @pl.when(j == 0)
    def _():
        st_ref[...] = jnp.zeros_like(st_ref)

    def sweep(fwd):
        blk = j if fwd else nbk - 1 - j
        tri = _gla_cumsum_matrix(fwd, nchunks)

        def group(gi, carry):
            g0 = gi if fwd else ngroups - 1 - gi
            r0 = pl.multiple_of(g0 * nchunks * C, nchunks * C)
            outs, states = _gla_group(qk_ref, vg_ref, dl_ref, wdec_ref, bdec_ref, tri,
                                      [st_ref[b] for b in range(nseq)], r0, fwd, nchunks)
            for b in range(nseq):
                st_ref[b] = states[b]
                for u in range(nchunks):
                    cg = blk * chunks_per_block + g0 * nchunks + u
                    if not fwd:
                        ob_ref[b, cg] = outs[b][u]
                        continue
                    c0 = pl.multiple_of(r0 + u * C, C)
                    o = outs[b][u] + ob_ref[b, cg]
                    o = o * lax.rsqrt(jnp.mean(o * o, axis=-1, keepdims=True) + RMS_EPS) * ng_ref[...]
                    gate = vg_ref[b, pl.ds(c0, C), GLA_VAL_WIDTH:2 * GLA_VAL_WIDTH].astype(F32)
                    gate = jnp.concatenate([gate[:, h * GLA_DV:(h + 1) * GLA_DV] for h in range(H)], axis=0)
                    o = o * (gate * jax.nn.sigmoid(gate))
                    o_ref[b, pl.ds(c0, C), :] = jnp.concatenate(
                        [o[h * C:(h + 1) * C] for h in range(H)], axis=1).astype(BF16)
            return carry

        lax.fori_loop(0, ngroups, group, 0)

    @pl.when(ph == 0)
    def _():
        sweep(False)

    @pl.when(ph == 1)
    def _():
        sweep(True)


def _gla(gqk, gvg, gdl, wdec_pad, bdec, norm_g):
    b, s, _ = gqk.shape
    cbk = min(GLA_BLOCK, s)
    nbk = s // cbk
    nseq = GLA_SEQS if b % GLA_SEQS == 0 else 1
    blk = lambda w: pl.BlockSpec((nseq, cbk, w), lambda bi, ph, j: (bi, jnp.where(ph == 0, nbk - 1 - j, j), 0))
    return pl.pallas_call(
        functools.partial(_gla_body, cbk=cbk, nbk=nbk),
        grid=(b // nseq, 2, nbk),
        in_specs=[blk(2 * GLA_KEY_WIDTH), blk(2 * GLA_VAL_WIDTH), blk(2 * GLA_GATE_RANK),
                  _const_spec(wdec_pad.shape), _const_spec(bdec.shape), _const_spec(norm_g.shape)],
        out_specs=pl.BlockSpec((nseq, cbk, GLA_VAL_WIDTH), lambda bi, ph, j: (bi, jnp.where(ph == 0, 0, j), 0)),
        out_shape=jax.ShapeDtypeStruct((b, s, GLA_VAL_WIDTH), BF16),
        scratch_shapes=[pltpu.VMEM((nseq, GLA_DV, GLA_KEY_WIDTH), F32),
                        pltpu.VMEM((nseq, s // GLA_CHUNK, GLA_HEADS * GLA_CHUNK, GLA_DV), F32)],
        compiler_params=pltpu.CompilerParams(dimension_semantics=("arbitrary", "arbitrary", "arbitrary"),
                                             vmem_limit_bytes=VMEM_LIMIT_BYTES),
        name="gla_bidirectional",
    )(gqk, gvg, gdl, wdec_pad, bdec, norm_g)


def _merge_ffn_body(x1_ref, na_ref, gla_ref, wgate_ref, wbn_ref, wbg_ref, wo_ref, g2_ref, b2_ref,
                    wg_ref, wu_ref, wd_ref, g3_ref, b3_ref, o_ref):
    tm = x1_ref.shape[0]
    subs = [pl.ds(r, tm // SUB_TILES) for r in range(0, tm, tm // SUB_TILES)]
    dot = functools.partial(jnp.dot, preferred_element_type=F32)
    x1s = [x1_ref[rows, :] for rows in subs]
    pre = [(dot(x1.astype(BF16), wgate_ref[...]), dot(na_ref[rows, :], wbn_ref[...]),
            dot(gla_ref[rows, :], wbg_ref[...])) for rows, x1 in zip(subs, x1s)]
    z2s = []
    for x1, (gates, y_na, y_gla) in zip(x1s, pre):
        merged = jax.nn.sigmoid(gates[:, :D_MODEL]) * y_na + jax.nn.sigmoid(gates[:, D_MODEL:]) * y_gla
        z2s.append(ALPHA * x1 + dot(merged.astype(BF16), wo_ref[...]))
    x2s = [_layer_norm(z2, g2_ref[...], b2_ref[...]) for z2 in z2s]
    z3s = [ALPHA * x2 + 0.5 * _swiglu(x2.astype(BF16), wg_ref, wu_ref, wd_ref) for x2 in x2s]
    for rows, z3 in zip(subs, z3s):
        o_ref[rows, :] = _layer_norm(z3, g3_ref[...], b3_ref[...])


def _merge_ffn(x1, na_o, gla_o, wgate, wbn, wbg, wo, g2, b2, wg, wu, wd, g3, b3):
    t = x1.shape[0]
    tm = min(TOKEN_TILE, t)
    row = lambda w: pl.BlockSpec((tm, w), lambda i: (i, 0))
    consts = (wgate, wbn, wbg, wo, g2, b2, wg, wu, wd, g3, b3)
    return pl.pallas_call(
        _merge_ffn_body,
        grid=(t // tm,),
        in_specs=[row(D_MODEL), row(NA_WIDTH), row(GLA_VAL_WIDTH)] + [_const_spec(c.shape) for c in consts],
        out_specs=row(D_MODEL),
        out_shape=jax.ShapeDtypeStruct((t, D_MODEL), F32),
        compiler_params=pltpu.CompilerParams(dimension_semantics=("arbitrary",),
                                             vmem_limit_bytes=VMEM_LIMIT_BYTES),
        name="merge_ln2_ffn2_ln3",
    )(x1, na_o, gla_o, *consts)


def _layer(x, ffn1_w_gate, ffn1_w_up, ffn1_w_down, ln1_g, ln1_b, w_in, na_rpb, gla_w_dec2, gla_b_dec,
           gla_norm_g, w_branch_na, w_branch_gla, w_out, ln2_g, ln2_b, ffn2_w_gate, ffn2_w_up, ffn2_w_down,
           ln3_g, ln3_b):
    b, s, d = x.shape
    assert d == D_MODEL and s % (NA_UNROLL * NA_QT) == 0 and s // GRID_W >= NA_BAND
    rows = s // GRID_W
    bf = lambda w: w.astype(BF16)
    vec = lambda p: p.reshape(1, -1)

    x1, nak, naqt, navt, gqk, gvg, gdl = _ffn_inproj(
        x.reshape(b * s, d), bf(ffn1_w_gate), bf(ffn1_w_up), bf(ffn1_w_down), vec(ln1_g), vec(ln1_b),
        bf(w_in[:, :OFF_GDL_END]))

    na_o = _neighborhood_attention(nak.reshape(b, s, -1), naqt, navt, _na_bias_tiles(na_rpb), rows)

    zr = jnp.zeros((GLA_GATE_RANK, GLA_KEY_WIDTH), gla_w_dec2.dtype)
    wdec_pad = bf(jnp.stack([jnp.concatenate([gla_w_dec2[0], zr]), jnp.concatenate([zr, gla_w_dec2[1]])]))
    gla_o = _gla(gqk.reshape(b, s, -1), gvg.reshape(b, s, -1), gdl.reshape(b, s, -1),
                 wdec_pad, gla_b_dec, vec(gla_norm_g))

    out = _merge_ffn(x1, na_o.reshape(b * s, -1), gla_o.reshape(b * s, -1), bf(w_in[:, OFF_GDL_END:]),
                     bf(w_branch_na), bf(w_branch_gla), bf(w_out), vec(ln2_g), vec(ln2_b),
                     bf(ffn2_w_gate), bf(ffn2_w_up), bf(ffn2_w_down), vec(ln3_g), vec(ln3_b))
    return out.reshape(b, s, d)


def kernel(x, ffn1_w_gate, ffn1_w_up, ffn1_w_down, ln1_g, ln1_b, w_in, na_rpb, gla_w_dec2, gla_b_dec, gla_norm_g,
           w_branch_na, w_branch_gla, w_out, ln2_g, ln2_b, ffn2_w_gate, ffn2_w_up, ffn2_w_down, ln3_g, ln3_b):
    params = (ffn1_w_gate, ffn1_w_up, ffn1_w_down, ln1_g, ln1_b, w_in, na_rpb, gla_w_dec2, gla_b_dec, gla_norm_g,
              w_branch_na, w_branch_gla, w_out, ln2_g, ln2_b, ffn2_w_gate, ffn2_w_up, ffn2_w_down, ln3_g, ln3_b)
    for l in range(DEPTH):
        x = _layer(x, *(p[l] for p in params))
    return x
```

```python
import functools

import jax
import jax.numpy as jnp
import numpy as np
from jax import lax
from jax.experimental import pallas as pl
from jax.experimental.pallas import tpu as pltpu

F32 = jnp.float32
BF16 = jnp.bfloat16

D_MODEL = 1024
D_FF = 2816
GRID_W = 64
NA_HEADS = 8
NA_HEAD_DIM = 64
NA_WIDTH = NA_HEADS * NA_HEAD_DIM
NA_WIN_H = 8
NA_WIN_W = 16
GLA_HEADS = 4
GLA_DK = 64
GLA_DV = 128
GLA_KEY_WIDTH = GLA_HEADS * GLA_DK
GLA_VAL_WIDTH = GLA_HEADS * GLA_DV
GLA_GATE_RANK = 16
GLA_GATE_TAU = 16.0
GLA_CHUNK = 64
DEPTH = 1
ALPHA = (2 * DEPTH) ** 0.25
LN_EPS = 1e-5
RMS_EPS = 1e-6

OFF_NA_END = 3 * NA_WIDTH
OFF_GQK_END = OFF_NA_END + 2 * GLA_KEY_WIDTH
OFF_GVG_END = OFF_GQK_END + 2 * GLA_VAL_WIDTH
OFF_GDL_END = OFF_GVG_END + 2 * GLA_GATE_RANK
N_IN = OFF_GDL_END + 2 * D_MODEL

LANES = 128
BF16_SUBLANES = 16
N_LATE_WEIGHTS = 7
VMEM_LIMIT_BYTES = 56 * 1024 * 1024

NA_RQ = 4
NA_BAND = NA_RQ + NA_WIN_H
NA_QT = NA_RQ * GRID_W
NA_KT = NA_BAND * GRID_W
NA_UNROLL = 8
NA_ONES_ROWS = 16
NEG_BIG = -1e30
LOG2E = 1.4426950408889634
NA_Q_SCALE = LOG2E * NA_HEAD_DIM ** -0.5

GLA_GROUP = 4
GLA_BLOCK = 1024
GLA_SEQS = 2
TOKEN_TILE = 512
SUB_TILES = 2


def _const_spec(shape):
    nd = len(shape)
    return pl.BlockSpec(shape, lambda *_: (0,) * nd, pipeline_mode=pl.Buffered(1))


def _layer_norm(z, g, b):
    mu = jnp.mean(z, axis=-1, keepdims=True)
    zc = z - mu
    var = jnp.mean(zc * zc, axis=-1, keepdims=True)
    return zc * lax.rsqrt(var + LN_EPS) * g + b


def _swiglu(xb, wg_ref, wu_ref, wd_ref):
    g = jnp.dot(xb, wg_ref[...], preferred_element_type=F32)
    u = jnp.dot(xb, wu_ref[...], preferred_element_type=F32)
    h = (g * jax.nn.sigmoid(g)) * u
    return jnp.dot(h.astype(BF16), wd_ref[...], preferred_element_type=F32)


def _ffn_inproj_body(x_ref, wg_ref, wu_ref, wd_ref, g1_ref, b1_ref, win_ref, *refs):
    late_f32, outs = refs[:N_LATE_WEIGHTS], refs[N_LATE_WEIGHTS:]
    x1_ref, nak_ref, naqt_ref, navt_ref, gqk_ref, gvg_ref, gdl_ref = outs[:-N_LATE_WEIGHTS]
    late_bf16 = outs[-N_LATE_WEIGHTS:]
    for src, dst in zip(late_f32[:-1], late_bf16[:-1]):
        dst[...] = src[...].astype(BF16)
    late_bf16[-1][...] = late_f32[-1][:, OFF_GDL_END:].astype(BF16)

    tm = x_ref.shape[0]
    subs = [pl.ds(r, tm // SUB_TILES) for r in range(0, tm, tm // SUB_TILES)]
    zs = []
    for rows in subs:
        x = x_ref[rows, :]
        zs.append(ALPHA * x + 0.5 * _swiglu(x.astype(BF16), wg_ref, wu_ref, wd_ref))
    for g, (rows, z) in enumerate(zip(subs, zs)):
        x1 = _layer_norm(z, g1_ref[...], b1_ref[...])
        x1_ref[rows, :] = x1
        proj = jnp.dot(x1.astype(BF16), win_ref[:, :OFF_GDL_END], preferred_element_type=F32)
        nak_ref[rows, :] = proj[:, NA_WIDTH:2 * NA_WIDTH].astype(BF16)
        naqt_ref[g] = (proj[:, :NA_WIDTH] * NA_Q_SCALE).T.astype(BF16)
        navt_ref[g] = proj[:, 2 * NA_WIDTH:OFF_NA_END].T.astype(BF16)
        gqk_ref[rows, :] = proj[:, OFF_NA_END:OFF_GQK_END].astype(BF16)
        gvg_ref[rows, :] = proj[:, OFF_GQK_END:OFF_GVG_END].astype(BF16)
        gdl_ref[rows, :] = proj[:, OFF_GVG_END:OFF_GDL_END].astype(BF16)


def _ffn_inproj(x2d, wg, wu, wd, g1, b1, win, late_weights):
    t = x2d.shape[0]
    tm = min(TOKEN_TILE, t)
    n = t // tm
    chunk = D_MODEL // n
    assert tm // SUB_TILES == NA_QT and chunk * n == D_MODEL and chunk % BF16_SUBLANES == 0
    assert len(late_weights) == N_LATE_WEIGHTS and all(w.shape[0] == D_MODEL for w in late_weights)
    row = lambda w: pl.BlockSpec((tm, w), lambda i: (i, 0))
    widths = (NA_WIDTH, OFF_GQK_END - OFF_NA_END, OFF_GVG_END - OFF_GQK_END, OFF_GDL_END - OFF_GVG_END)
    rows_bf16 = [jax.ShapeDtypeStruct((t, w), BF16) for w in widths]
    late_cols = [w.shape[1] for w in late_weights[:-1]] + [N_IN - OFF_GDL_END]
    rows_of = lambda cols: pl.BlockSpec((chunk, cols), lambda i: (i, 0))
    return pl.pallas_call(
        _ffn_inproj_body,
        grid=(n,),
        in_specs=[row(D_MODEL), _const_spec(wg.shape), _const_spec(wu.shape), _const_spec(wd.shape),
                  _const_spec(g1.shape), _const_spec(b1.shape), _const_spec(win.shape)]
                 + [rows_of(w.shape[1]) for w in late_weights],
        out_specs=[row(D_MODEL), row(widths[0])] + [pl.BlockSpec((SUB_TILES, NA_WIDTH, NA_QT), lambda i: (i, 0, 0))] * 2
                  + [row(w) for w in widths[1:]] + [rows_of(c) for c in late_cols],
        out_shape=[jax.ShapeDtypeStruct((t, D_MODEL), F32), rows_bf16[0]]
                  + [jax.ShapeDtypeStruct((t // NA_QT, NA_WIDTH, NA_QT), BF16)] * 2 + rows_bf16[1:]
                  + [jax.ShapeDtypeStruct((D_MODEL, c), BF16) for c in late_cols],
        compiler_params=pltpu.CompilerParams(dimension_semantics=("arbitrary",),
                                             vmem_limit_bytes=VMEM_LIMIT_BYTES),
        name="ffn1_ln1_inproj",
    )(x2d, wg, wu, wd, g1, b1, win, *late_weights)


NA_ROW_OFFSETS = 2 * NA_WIN_H - 1


def _na_bias_tiles(rpb):
    kw = NA_WIN_W
    c = np.arange(GRID_W)[None, :]
    kc = np.arange(GRID_W)[:, None]
    c_start = np.clip(c - kw // 2, 0, GRID_W - kw)
    col_ok = (kc >= c_start) & (kc < c_start + kw)
    dc = kc - c + (kw - 1)
    sel_c = ((dc[..., None] == np.arange(2 * kw - 1)) & col_ok[..., None]).astype(np.float32)
    tiles = jnp.einsum('hdw,kcw->hdkc', rpb, sel_c, precision=lax.Precision.HIGHEST)
    tiles = jnp.where(col_ok, tiles * LOG2E, NEG_BIG)
    tiles = jnp.concatenate([tiles, jnp.full_like(tiles[:, :1], NEG_BIG)], axis=1)
    return jnp.concatenate([tiles, tiles], axis=-1).astype(F32)


def _na_row_offsets(rows):
    kh = NA_WIN_H
    ri = np.arange(NA_RQ)[:, None]
    bj = np.arange(NA_BAND)[None, :]
    out = []
    for r0, b0 in ((0, 0), (NA_RQ, 0), (rows - NA_RQ, rows - NA_BAND)):
        r = r0 + ri
        kr = b0 + bj
        r_start = np.clip(r - kh // 2, 0, rows - kh)
        row_ok = (kr >= r_start) & (kr < r_start + kh)
        out.append(np.where(row_ok, kr - r + (kh - 1), NA_ROW_OFFSETS).tolist())
    return out


def _na_body(qt_ref, k_ref, vt_ref, tile_ref, o_ref, s_ref, bias_ref, *, rows):
    nblk = rows // NA_RQ
    lane = lax.broadcasted_iota(jnp.int32, (1, LANES), 1)
    first_head = lane < NA_HEAD_DIM
    first_head_row = lax.broadcasted_iota(jnp.int32, (LANES, 1), 0) < NA_HEAD_DIM

    @pl.when(pl.program_id(1) == 0)
    def _():
        for kind, offsets in enumerate(_na_row_offsets(rows)):
            for hh in range(2):
                for bj in range(NA_BAND):
                    for ri in range(0, NA_RQ, 2):
                        tile = jnp.where(first_head, tile_ref[hh, offsets[ri][bj]], tile_ref[hh, offsets[ri + 1][bj]])
                        bias_ref[kind, pl.ds(bj * GRID_W, GRID_W), pl.ds(hh * NA_QT + ri * GRID_W, LANES)] = tile

    def band_row(i):
        return jnp.clip(NA_RQ * i - NA_WIN_H // 2, 0, rows - NA_BAND)

    def scores(i, slot):
        q0 = pl.multiple_of(i * NA_QT, NA_QT)
        k0 = pl.multiple_of(band_row(i) * GRID_W, NA_QT)
        kind = jnp.where(i == 0, 0, jnp.where(i == nblk - 1, 2, 1))
        qt = qt_ref[i]
        k = k_ref[0, pl.ds(k0, NA_KT), :]
        zero = jnp.zeros((), BF16)
        qt2 = jnp.concatenate([jnp.where(first_head_row, qt, zero), jnp.where(first_head_row, zero, qt)], axis=1)
        s = jnp.dot(k, qt2, preferred_element_type=F32)
        s_ref[slot] = s + bias_ref[kind]

    def finish(i, slot):
        q0 = pl.multiple_of(i * NA_QT, NA_QT)
        g0 = band_row(i) // NA_RQ
        s = s_ref[slot]
        p = jnp.exp2(s - jnp.max(s, axis=0, keepdims=True)).astype(BF16)
        ones = jnp.ones((NA_ONES_ROWS, NA_QT), BF16)
        heads = []
        for hh in range(2):
            acc = None
            for g in range(NA_BAND // NA_RQ):
                vt = jnp.concatenate([vt_ref[g0 + g, hh * NA_HEAD_DIM:(hh + 1) * NA_HEAD_DIM, :], ones], axis=0)
                part = jnp.dot(vt, p[g * NA_QT:(g + 1) * NA_QT, hh * NA_QT:(hh + 1) * NA_QT],
                               preferred_element_type=F32)
                acc = part if acc is None else acc + part
            heads.append(acc[:NA_HEAD_DIM] * (1.0 / acc[NA_HEAD_DIM:NA_HEAD_DIM + 1]))
        o_ref[0, pl.ds(q0, NA_QT), :] = jnp.concatenate(heads, axis=0).T.astype(BF16)

    scores(0, 0)

    def group(j, carry):
        for u in range(NA_UNROLL):
            i = NA_UNROLL * j + u
            scores(jnp.minimum(i + 1, nblk - 1), (u + 1) % 2)
            finish(i, u % 2)
        return carry

    lax.fori_loop(0, nblk // NA_UNROLL, group, 0)


def _neighborhood_attention(nak, naqt, navt, tiles, rows):
    b, s, _ = nak.shape
    npairs = NA_HEADS // 2
    tok = pl.BlockSpec((1, s, LANES), lambda p, bi: (bi, 0, p))
    slabs = pl.BlockSpec((s // NA_QT, LANES, NA_QT), lambda p, bi: (bi, p, 0))
    return pl.pallas_call(
        functools.partial(_na_body, rows=rows),
        grid=(npairs, b),
        in_specs=[slabs, tok, slabs,
                  pl.BlockSpec((2, NA_ROW_OFFSETS + 1, GRID_W, LANES), lambda p, bi: (p, 0, 0, 0))],
        out_specs=tok,
        out_shape=jax.ShapeDtypeStruct((b, s, NA_WIDTH), BF16),
        scratch_shapes=[pltpu.VMEM((2, NA_KT, 2 * NA_QT), F32), pltpu.VMEM((3, NA_KT, 2 * NA_QT), F32)],
        compiler_params=pltpu.CompilerParams(dimension_semantics=("arbitrary", "arbitrary"),
                                             vmem_limit_bytes=VMEM_LIMIT_BYTES),
        name="neighborhood_attention",
    )(naqt, nak, navt, tiles)


def _split3(x):
    top16 = lambda a: lax.bitcast_convert_type(
        lax.bitcast_convert_type(a, jnp.int32) & jnp.int32(-65536), F32)
    hi = top16(x)
    r1 = x - hi
    mid = top16(r1)
    lo = r1 - mid
    return hi.astype(BF16), mid.astype(BF16), lo.astype(BF16)


def _gla_cumsum_matrix(fwd, nchunks):
    G = nchunks * GLA_CHUNK
    ci = lax.broadcasted_iota(jnp.int32, (G, G), 0)
    si = lax.broadcasted_iota(jnp.int32, (G, G), 1)
    in_order = (si <= ci) if fwd else (si >= ci)
    return jnp.where(in_order & ((ci // GLA_CHUNK) == (si // GLA_CHUNK)), 1.0, 0.0).astype(BF16)


def _gla_group(qk_ref, vg_ref, dl_ref, wdec_ref, bdec_ref, tri, states, r0, fwd, nchunks):
    C, KW, H = GLA_CHUNK, GLA_KEY_WIDTH, GLA_HEADS
    G = nchunks * C
    nseq = len(states)
    d = 0 if fwd else 1
    rows = pl.ds(r0, G)
    last = C - 1 if fwd else 0
    chunk = lambda a, u: a[u * C:(u + 1) * C]
    nt = (((1,), (1,)), ((), ()))

    zs = [(jnp.dot(dl_ref[b, rows, :], wdec_ref[d], preferred_element_type=F32) + bdec_ref[d:d + 1, :]) * LOG2E
          for b in range(nseq)]
    log_a = [(jnp.minimum(x, 0.0) - jnp.log2(1.0 + jnp.exp2(-jnp.abs(x)))) * (1.0 / GLA_GATE_TAU) for x in zs]

    bc = [jnp.dot(tri, jnp.concatenate(_split3(la), axis=1), preferred_element_type=F32) for la in log_a]
    b_cum = [(x[:, 0:KW] + x[:, KW:2 * KW]) + x[:, 2 * KW:3 * KW] for x in bc]
    b_last_rows = [[bcb[u * C + last:u * C + last + 1, :] for u in range(nchunks)] for bcb in b_cum]
    b_last = [jnp.concatenate([jnp.broadcast_to(r, (C, KW)) for r in rws], axis=0) for rws in b_last_rows]

    lane_head = lax.broadcasted_iota(jnp.int32, (1, KW), 1) // GLA_DK
    zero = jnp.zeros((), BF16)
    block_diag = lambda a: jnp.concatenate([jnp.where(lane_head == h, a, zero) for h in range(H)], axis=0)
    ci = lax.broadcasted_iota(jnp.int32, (C, C), 0)
    si = lax.broadcasted_iota(jnp.int32, (C, C), 1)
    causal = (si <= ci) if fwd else (si > ci)

    q_bd, k_t, kend_bd, v_heads = [], [], [], []
    for b in range(nseq):
        q = qk_ref[b, rows, 0:KW].astype(F32)
        k = qk_ref[b, rows, KW:2 * KW].astype(F32)
        v = vg_ref[b, rows, 0:GLA_VAL_WIDTH]
        q_t = ((q * (GLA_DK ** -0.5)) * jnp.exp2(b_cum[b])).astype(BF16)
        k_t.append((k * jnp.exp2(-b_cum[b])).astype(BF16))
        k_end = (k * jnp.exp2(b_last[b] - b_cum[b])).astype(BF16)
        q_bd.append([block_diag(chunk(q_t, u)) for u in range(nchunks)])
        kend_bd.append([block_diag(chunk(k_end, u)) for u in range(nchunks)])
        v_heads.append([[chunk(v, u)[:, h * GLA_DV:(h + 1) * GLA_DV] for h in range(H)] for u in range(nchunks)])

    units = [(b, u) for b in range(nseq) for u in range(nchunks)]
    attn = {bu: lax.dot_general(q_bd[bu[0]][bu[1]], chunk(k_t[bu[0]], bu[1]), nt, preferred_element_type=F32)
            for bu in units}
    attn = {bu: jnp.concatenate([jnp.where(causal, chunk(a, h), 0.0) for h in range(H)], axis=0).astype(BF16)
            for bu, a in attn.items()}
    st_add = {(b, u): lax.dot_general(jnp.concatenate(v_heads[b][u], axis=0), kend_bd[b][u],
                                      (((0,), (0,)), ((), ())), preferred_element_type=F32) for b, u in units}
    o_intra = {(b, u): jnp.concatenate([jnp.dot(chunk(attn[b, u], h), v_heads[b][u][h],
                                                preferred_element_type=F32) for h in range(H)], axis=0)
               for b, u in units}
    outs = [[None] * nchunks for _ in range(nseq)]
    states = list(states)
    for u in (range(nchunks) if fwd else reversed(range(nchunks))):
        for b in range(nseq):
            st = states[b]
            outs[b][u] = o_intra[b, u] + lax.dot_general(q_bd[b][u], st.astype(BF16), nt,
                                                         preferred_element_type=F32)
            states[b] = jnp.exp2(b_last_rows[b][u]) * st + st_add[b, u]
    return outs, states


def _gla_body(qk_ref, vg_ref, dl_ref, wdec_ref, bdec_ref, ng_ref, o_ref, st_ref, ob_ref, *, cbk, nbk):
    C, H = GLA_CHUNK, GLA_HEADS
    nseq = qk_ref.shape[0]
    ph = pl.program_id(1)
    j = pl.program_id(2)
    chunks_per_block = cbk // C
    nchunks = min(GLA_GROUP, chunks_per_block)
    ngroups = chunks_per_block // nchunks

    @pl.when(j == 0)
    def _():
        st_ref[...] = jnp.zeros_like(st_ref)

    def sweep(fwd):
        blk = j if fwd else nbk - 1 - j
        tri = _gla_cumsum_matrix(fwd, nchunks)

        def group(gi, carry):
            g0 = gi if fwd else ngroups - 1 - gi
            r0 = pl.multiple_of(g0 * nchunks * C, nchunks * C)
            outs, states = _gla_group(qk_ref, vg_ref, dl_ref, wdec_ref, bdec_ref, tri,
                                      [st_ref[b] for b in range(nseq)], r0, fwd, nchunks)
            for b in range(nseq):
                st_ref[b] = states[b]
                for u in range(nchunks):
                    cg = blk * chunks_per_block + g0 * nchunks + u
                    if not fwd:
                        ob_ref[b, cg] = outs[b][u]
                        continue
                    c0 = pl.multiple_of(r0 + u * C, C)
                    o = outs[b][u] + ob_ref[b, cg]
                    o = o * lax.rsqrt(jnp.mean(o * o, axis=-1, keepdims=True) + RMS_EPS) * ng_ref[...]
                    gate = vg_ref[b, pl.ds(c0, C), GLA_VAL_WIDTH:2 * GLA_VAL_WIDTH].astype(F32)
                    gate = jnp.concatenate([gate[:, h * GLA_DV:(h + 1) * GLA_DV] for h in range(H)], axis=0)
                    o = o * (gate * jax.nn.sigmoid(gate))
                    o_ref[b, pl.ds(c0, C), :] = jnp.concatenate(
                        [o[h * C:(h + 1) * C] for h in range(H)], axis=1).astype(BF16)
            return carry

        lax.fori_loop(0, ngroups, group, 0)

    @pl.when(ph == 0)
    def _():
        sweep(False)

    @pl.when(ph == 1)
    def _():
        sweep(True)


def _gla(gqk, gvg, gdl, wdec_pad, bdec, norm_g):
    b, s, _ = gqk.shape
    cbk = min(GLA_BLOCK, s)
    nbk = s // cbk
    nseq = GLA_SEQS if b % GLA_SEQS == 0 else 1
    blk = lambda w: pl.BlockSpec((nseq, cbk, w), lambda bi, ph, j: (bi, jnp.where(ph == 0, nbk - 1 - j, j), 0))
    return pl.pallas_call(
        functools.partial(_gla_body, cbk=cbk, nbk=nbk),
        grid=(b // nseq, 2, nbk),
        in_specs=[blk(2 * GLA_KEY_WIDTH), blk(2 * GLA_VAL_WIDTH), blk(2 * GLA_GATE_RANK),
                  _const_spec(wdec_pad.shape), _const_spec(bdec.shape), _const_spec(norm_g.shape)],
        out_specs=pl.BlockSpec((nseq, cbk, GLA_VAL_WIDTH), lambda bi, ph, j: (bi, jnp.where(ph == 0, 0, j), 0)),
        out_shape=jax.ShapeDtypeStruct((b, s, GLA_VAL_WIDTH), BF16),
        scratch_shapes=[pltpu.VMEM((nseq, GLA_DV, GLA_KEY_WIDTH), F32),
                        pltpu.VMEM((nseq, s // GLA_CHUNK, GLA_HEADS * GLA_CHUNK, GLA_DV), F32)],
        compiler_params=pltpu.CompilerParams(dimension_semantics=("arbitrary", "arbitrary", "arbitrary"),
                                             vmem_limit_bytes=VMEM_LIMIT_BYTES),
        name="gla_bidirectional",
    )(gqk, gvg, gdl, wdec_pad, bdec, norm_g)


def _merge_ffn_body(x1_ref, na_ref, gla_ref, wgate_ref, wbn_ref, wbg_ref, wo_ref, g2_ref, b2_ref,
                    wg_ref, wu_ref, wd_ref, g3_ref, b3_ref, o_ref):
    tm = x1_ref.shape[0]
    subs = [pl.ds(r, tm // SUB_TILES) for r in range(0, tm, tm // SUB_TILES)]
    dot = functools.partial(jnp.dot, preferred_element_type=F32)
    x1s = [x1_ref[rows, :] for rows in subs]
    pre = [(dot(x1.astype(BF16), wgate_ref[...]), dot(na_ref[rows, :], wbn_ref[...]),
            dot(gla_ref[rows, :], wbg_ref[...])) for rows, x1 in zip(subs, x1s)]
    z2s = []
    for x1, (gates, y_na, y_gla) in zip(x1s, pre):
        merged = jax.nn.sigmoid(gates[:, :D_MODEL]) * y_na + jax.nn.sigmoid(gates[:, D_MODEL:]) * y_gla
        z2s.append(ALPHA * x1 + dot(merged.astype(BF16), wo_ref[...]))
    x2s = [_layer_norm(z2, g2_ref[...], b2_ref[...]) for z2 in z2s]
    z3s = [ALPHA * x2 + 0.5 * _swiglu(x2.astype(BF16), wg_ref, wu_ref, wd_ref) for x2 in x2s]
    for rows, z3 in zip(subs, z3s):
        o_ref[rows, :] = _layer_norm(z3, g3_ref[...], b3_ref[...])


def _merge_ffn(x1, na_o, gla_o, wgate, wbn, wbg, wo, g2, b2, wg, wu, wd, g3, b3):
    t = x1.shape[0]
    tm = min(TOKEN_TILE, t)
    row = lambda w: pl.BlockSpec((tm, w), lambda i: (i, 0))
    consts = (wgate, wbn, wbg, wo, g2, b2, wg, wu, wd, g3, b3)
    return pl.pallas_call(
        _merge_ffn_body,
        grid=(t // tm,),
        in_specs=[row(D_MODEL), row(NA_WIDTH), row(GLA_VAL_WIDTH)] + [_const_spec(c.shape) for c in consts],
        out_specs=row(D_MODEL),
        out_shape=jax.ShapeDtypeStruct((t, D_MODEL), F32),
        compiler_params=pltpu.CompilerParams(dimension_semantics=("arbitrary",),
                                             vmem_limit_bytes=VMEM_LIMIT_BYTES),
        name="merge_ln2_ffn2_ln3",
    )(x1, na_o, gla_o, *consts)


def _layer(x, ffn1_w_gate, ffn1_w_up, ffn1_w_down, ln1_g, ln1_b, w_in, na_rpb, gla_w_dec2, gla_b_dec,
           gla_norm_g, w_branch_na, w_branch_gla, w_out, ln2_g, ln2_b, ffn2_w_gate, ffn2_w_up, ffn2_w_down,
           ln3_g, ln3_b):
    b, s, d = x.shape
    assert d == D_MODEL and s % (NA_UNROLL * NA_QT) == 0 and s // GRID_W >= NA_BAND
    rows = s // GRID_W
    bf = lambda w: w.astype(BF16)
    vec = lambda p: p.reshape(1, -1)

    late = (ffn2_w_gate, ffn2_w_up, ffn2_w_down, w_out, w_branch_na, w_branch_gla)
    x1, nak, naqt, navt, gqk, gvg, gdl, *late_bf16 = _ffn_inproj(
        x.reshape(b * s, d), bf(ffn1_w_gate), bf(ffn1_w_up), bf(ffn1_w_down), vec(ln1_g), vec(ln1_b), bf(w_in),
        [w.reshape(D_MODEL, -1) for w in late] + [w_in])
    w2_gate, w2_up, w2_down, wo, wbn, wbg = (c.reshape(w.shape) for c, w in zip(late_bf16, late))
    w_merge_gates = late_bf16[-1]

    na_o = _neighborhood_attention(nak.reshape(b, s, -1), naqt, navt, _na_bias_tiles(na_rpb), rows)

    zr = jnp.zeros((GLA_GATE_RANK, GLA_KEY_WIDTH), gla_w_dec2.dtype)
    wdec_pad = bf(jnp.stack([jnp.concatenate([gla_w_dec2[0], zr]), jnp.concatenate([zr, gla_w_dec2[1]])]))
    gla_o = _gla(gqk.reshape(b, s, -1), gvg.reshape(b, s, -1), gdl.reshape(b, s, -1),
                 wdec_pad, gla_b_dec, vec(gla_norm_g))

    out = _merge_ffn(x1, na_o.reshape(b * s, -1), gla_o.reshape(b * s, -1), w_merge_gates, wbn, wbg, wo,
                     vec(ln2_g), vec(ln2_b), w2_gate, w2_up, w2_down, vec(ln3_g), vec(ln3_b))
    return out.reshape(b, s, d)


def kernel(x, ffn1_w_gate, ffn1_w_up, ffn1_w_down, ln1_g, ln1_b, w_in, na_rpb, gla_w_dec2, gla_b_dec, gla_norm_g,
           w_branch_na, w_branch_gla, w_out, ln2_g, ln2_b, ffn2_w_gate, ffn2_w_up, ffn2_w_down, ln3_g, ln3_b):
    params = (ffn1_w_gate, ffn1_w_up, ffn1_w_down, ln1_g, ln1_b, w_in, na_rpb, gla_w_dec2, gla_b_dec, gla_norm_g,
              w_branch_na, w_branch_gla, w_out, ln2_g, ln2_b, ffn2_w_gate, ffn2_w_up, ffn2_w_down, ln3_g, ln3_b)
    for l in range(DEPTH):
        x = _layer(x, *(p[l] for p in params))
    return x
```

```python
import functools

import jax
import jax.numpy as jnp
import numpy as np
from jax import lax
from jax.experimental import pallas as pl
from jax.experimental.pallas import tpu as pltpu

F32 = jnp.float32
BF16 = jnp.bfloat16

D_MODEL = 1024
D_FF = 2816
GRID_W = 64
NA_HEADS = 8
NA_HEAD_DIM = 64
NA_WIDTH = NA_HEADS * NA_HEAD_DIM
NA_WIN_H = 8
NA_WIN_W = 16
GLA_HEADS = 4
GLA_DK = 64
GLA_DV = 128
GLA_KEY_WIDTH = GLA_HEADS * GLA_DK
GLA_VAL_WIDTH = GLA_HEADS * GLA_DV
GLA_GATE_RANK = 16
GLA_GATE_TAU = 16.0
GLA_CHUNK = 64
DEPTH = 1
ALPHA = (2 * DEPTH) ** 0.25
LN_EPS = 1e-5
RMS_EPS = 1e-6

OFF_NA_END = 3 * NA_WIDTH
OFF_GQK_END = OFF_NA_END + 2 * GLA_KEY_WIDTH
OFF_GVG_END = OFF_GQK_END + 2 * GLA_VAL_WIDTH
OFF_GDL_END = OFF_GVG_END + 2 * GLA_GATE_RANK
N_IN = OFF_GDL_END + 2 * D_MODEL

LANES = 128
BF16_SUBLANES = 16
N_LATE_WEIGHTS = 7
VMEM_LIMIT_BYTES = 56 * 1024 * 1024

NA_RQ = 4
NA_BAND = NA_RQ + NA_WIN_H
NA_QT = NA_RQ * GRID_W
NA_KT = NA_BAND * GRID_W
NA_UNROLL = 8
NA_ONES_ROWS = 16
NEG_BIG = -1e30
LOG2E = 1.4426950408889634
NA_Q_SCALE = LOG2E * NA_HEAD_DIM ** -0.5

GLA_GROUP = 4
GLA_BLOCK = 1024
GLA_SEQS = 2
TOKEN_TILE = 512
SUB_TILES = 2


def _const_spec(shape):
    nd = len(shape)
    return pl.BlockSpec(shape, lambda *_: (0,) * nd, pipeline_mode=pl.Buffered(1))


def _layer_norm(z, g, b):
    mu = jnp.mean(z, axis=-1, keepdims=True)
    zc = z - mu
    var = jnp.mean(zc * zc, axis=-1, keepdims=True)
    return zc * lax.rsqrt(var + LN_EPS) * g + b


def _swiglu(xb, wg_ref, wu_ref, wd_ref):
    g = jnp.dot(xb, wg_ref[...], preferred_element_type=F32)
    u = jnp.dot(xb, wu_ref[...], preferred_element_type=F32)
    h = (g * jax.nn.sigmoid(g)) * u
    return jnp.dot(h.astype(BF16), wd_ref[...], preferred_element_type=F32)


def _ffn_inproj_body(x_ref, wg_ref, wu_ref, wd_ref, g1_ref, b1_ref, win_ref, *refs):
    late_f32, outs = refs[:N_LATE_WEIGHTS], refs[N_LATE_WEIGHTS:]
    x1_ref, nak_ref, naqt_ref, navt_ref, gqk_ref, gvg_ref, gdl_ref = outs[:-N_LATE_WEIGHTS]
    late_bf16 = outs[-N_LATE_WEIGHTS:]
    for src, dst in zip(late_f32[:-1], late_bf16[:-1]):
        dst[...] = src[...].astype(BF16)
    late_bf16[-1][...] = late_f32[-1][:, OFF_GDL_END:].astype(BF16)

    tm = x_ref.shape[0]
    subs = [pl.ds(r, tm // SUB_TILES) for r in range(0, tm, tm // SUB_TILES)]
    zs = []
    for rows in subs:
        x = x_ref[rows, :]
        zs.append(ALPHA * x + 0.5 * _swiglu(x.astype(BF16), wg_ref, wu_ref, wd_ref))
    for g, (rows, z) in enumerate(zip(subs, zs)):
        x1 = _layer_norm(z, g1_ref[...], b1_ref[...])
        x1_ref[rows, :] = x1
        proj = jnp.dot(x1.astype(BF16), win_ref[:, :OFF_GDL_END], preferred_element_type=F32)
        nak_ref[rows, :] = proj[:, NA_WIDTH:2 * NA_WIDTH].astype(BF16)
        naqt_ref[g] = (proj[:, :NA_WIDTH] * NA_Q_SCALE).T.astype(BF16)
        navt_ref[g] = proj[:, 2 * NA_WIDTH:OFF_NA_END].T.astype(BF16)
        gqk_ref[rows, :] = proj[:, OFF_NA_END:OFF_GQK_END].astype(BF16)
        gvg_ref[rows, :] = proj[:, OFF_GQK_END:OFF_GVG_END].astype(BF16)
        gdl_ref[rows, :] = proj[:, OFF_GVG_END:OFF_GDL_END].astype(BF16)


def _row_chunk(nrows, nsteps):
    return next(c for c in range(BF16_SUBLANES, nrows + 1, BF16_SUBLANES) if nrows % c == 0 and c * nsteps >= nrows)


def _ffn_inproj(x2d, wg, wu, wd, g1, b1, win, late_weights, layer):
    t = x2d.shape[0]
    tm = min(TOKEN_TILE, t)
    n = t // tm
    assert tm // SUB_TILES == NA_QT and len(late_weights) == N_LATE_WEIGHTS
    row = lambda w: pl.BlockSpec((tm, w), lambda i: (i, 0))
    widths = (NA_WIDTH, OFF_GQK_END - OFF_NA_END, OFF_GVG_END - OFF_GQK_END, OFF_GDL_END - OFF_GVG_END)
    rows_bf16 = [jax.ShapeDtypeStruct((t, w), BF16) for w in widths]
    late_shapes = [w.shape[1:] for w in late_weights[:-1]] + [(D_MODEL, N_IN - OFF_GDL_END)]

    def chunk_specs(w_rows, cols_in, cols_out):
        c = _row_chunk(w_rows, n)
        last = w_rows // c - 1
        return (pl.BlockSpec((None, c, cols_in), lambda i: (layer, jnp.minimum(i, last), 0)),
                pl.BlockSpec((c, cols_out), lambda i: (jnp.minimum(i, last), 0)))

    late_specs = [chunk_specs(w.shape[1], w.shape[2], so[1]) for w, so in zip(late_weights, late_shapes)]
    return pl.pallas_call(
        _ffn_inproj_body,
        grid=(n,),
        in_specs=[row(D_MODEL), _const_spec(wg.shape), _const_spec(wu.shape), _const_spec(wd.shape),
                  _const_spec(g1.shape), _const_spec(b1.shape), _const_spec(win.shape)]
                 + [si for si, _ in late_specs],
        out_specs=[row(D_MODEL), row(widths[0])] + [pl.BlockSpec((SUB_TILES, NA_WIDTH, NA_QT), lambda i: (i, 0, 0))] * 2
                  + [row(w) for w in widths[1:]] + [so for _, so in late_specs],
        out_shape=[jax.ShapeDtypeStruct((t, D_MODEL), F32), rows_bf16[0]]
                  + [jax.ShapeDtypeStruct((t // NA_QT, NA_WIDTH, NA_QT), BF16)] * 2 + rows_bf16[1:]
                  + [jax.ShapeDtypeStruct(so, BF16) for so in late_shapes],
        compiler_params=pltpu.CompilerParams(dimension_semantics=("arbitrary",),
                                             vmem_limit_bytes=VMEM_LIMIT_BYTES),
        name="ffn1_ln1_inproj",
    )(x2d, wg, wu, wd, g1, b1, win, *late_weights)


NA_ROW_OFFSETS = 2 * NA_WIN_H - 1


def _na_bias_tiles(rpb):
    kw = NA_WIN_W
    c = np.arange(GRID_W)[None, :]
    kc = np.arange(GRID_W)[:, None]
    c_start = np.clip(c - kw // 2, 0, GRID_W - kw)
    col_ok = (kc >= c_start) & (kc < c_start + kw)
    dc = kc - c + (kw - 1)
    sel_c = ((dc[..., None] == np.arange(2 * kw - 1)) & col_ok[..., None]).astype(np.float32)
    tiles = jnp.einsum('hdw,kcw->hdkc', rpb, sel_c, precision=lax.Precision.HIGHEST)
    tiles = jnp.where(col_ok, tiles * LOG2E, NEG_BIG)
    tiles = jnp.concatenate([tiles, jnp.full_like(tiles[:, :1], NEG_BIG)], axis=1)
    return jnp.concatenate([tiles, tiles], axis=-1).astype(F32)


def _na_row_offsets(rows):
    kh = NA_WIN_H
    ri = np.arange(NA_RQ)[:, None]
    bj = np.arange(NA_BAND)[None, :]
    out = []
    for r0, b0 in ((0, 0), (NA_RQ, 0), (rows - NA_RQ, rows - NA_BAND)):
        r = r0 + ri
        kr = b0 + bj
        r_start = np.clip(r - kh // 2, 0, rows - kh)
        row_ok = (kr >= r_start) & (kr < r_start + kh)
        out.append(np.where(row_ok, kr - r + (kh - 1), NA_ROW_OFFSETS).tolist())
    return out


def _na_body(qt_ref, k_ref, vt_ref, tile_ref, o_ref, s_ref, bias_ref, *, rows):
    nblk = rows // NA_RQ
    lane = lax.broadcasted_iota(jnp.int32, (1, LANES), 1)
    first_head = lane < NA_HEAD_DIM
    first_head_row = lax.broadcasted_iota(jnp.int32, (LANES, 1), 0) < NA_HEAD_DIM

    @pl.when(pl.program_id(1) == 0)
    def _():
        for kind, offsets in enumerate(_na_row_offsets(rows)):
            for hh in range(2):
                for bj in range(NA_BAND):
                    for ri in range(0, NA_RQ, 2):
                        tile = jnp.where(first_head, tile_ref[hh, offsets[ri][bj]], tile_ref[hh, offsets[ri + 1][bj]])
                        bias_ref[kind, pl.ds(bj * GRID_W, GRID_W), pl.ds(hh * NA_QT + ri * GRID_W, LANES)] = tile

    def band_row(i):
        return jnp.clip(NA_RQ * i - NA_WIN_H // 2, 0, rows - NA_BAND)

    def scores(i, slot):
        q0 = pl.multiple_of(i * NA_QT, NA_QT)
        k0 = pl.multiple_of(band_row(i) * GRID_W, NA_QT)
        kind = jnp.where(i == 0, 0, jnp.where(i == nblk - 1, 2, 1))
        qt = qt_ref[i]
        k = k_ref[0, pl.ds(k0, NA_KT), :]
        zero = jnp.zeros((), BF16)
        qt2 = jnp.concatenate([jnp.where(first_head_row, qt, zero), jnp.where(first_head_row, zero, qt)], axis=1)
        s = jnp.dot(k, qt2, preferred_element_type=F32)
        s_ref[slot] = s + bias_ref[kind]

    def finish(i, slot):
        q0 = pl.multiple_of(i * NA_QT, NA_QT)
        g0 = band_row(i) // NA_RQ
        s = s_ref[slot]
        p = jnp.exp2(s - jnp.max(s, axis=0, keepdims=True)).astype(BF16)
        ones = jnp.ones((NA_ONES_ROWS, NA_QT), BF16)
        heads = []
        for hh in range(2):
            acc = None
            for g in range(NA_BAND // NA_RQ):
                vt = jnp.concatenate([vt_ref[g0 + g, hh * NA_HEAD_DIM:(hh + 1) * NA_HEAD_DIM, :], ones], axis=0)
                part = jnp.dot(vt, p[g * NA_QT:(g + 1) * NA_QT, hh * NA_QT:(hh + 1) * NA_QT],
                               preferred_element_type=F32)
                acc = part if acc is None else acc + part
            heads.append(acc[:NA_HEAD_DIM] * (1.0 / acc[NA_HEAD_DIM:NA_HEAD_DIM + 1]))
        o_ref[0, pl.ds(q0, NA_QT), :] = jnp.concatenate(heads, axis=0).T.astype(BF16)

    scores(0, 0)

    def group(j, carry):
        for u in range(NA_UNROLL):
            i = NA_UNROLL * j + u
            scores(jnp.minimum(i + 1, nblk - 1), (u + 1) % 2)
            finish(i, u % 2)
        return carry

    lax.fori_loop(0, nblk // NA_UNROLL, group, 0)


def _neighborhood_attention(nak, naqt, navt, tiles, rows):
    b, s, _ = nak.shape
    npairs = NA_HEADS // 2
    tok = pl.BlockSpec((1, s, LANES), lambda p, bi: (bi, 0, p))
    slabs = pl.BlockSpec((s // NA_QT, LANES, NA_QT), lambda p, bi: (bi, p, 0))
    return pl.pallas_call(
        functools.partial(_na_body, rows=rows),
        grid=(npairs, b),
        in_specs=[slabs, tok, slabs,
                  pl.BlockSpec((2, NA_ROW_OFFSETS + 1, GRID_W, LANES), lambda p, bi: (p, 0, 0, 0))],
        out_specs=tok,
        out_shape=jax.ShapeDtypeStruct((b, s, NA_WIDTH), BF16),
        scratch_shapes=[pltpu.VMEM((2, NA_KT, 2 * NA_QT), F32), pltpu.VMEM((3, NA_KT, 2 * NA_QT), F32)],
        compiler_params=pltpu.CompilerParams(dimension_semantics=("arbitrary", "arbitrary"),
                                             vmem_limit_bytes=VMEM_LIMIT_BYTES),
        name="neighborhood_attention",
    )(naqt, nak, navt, tiles)


def _split3(x):
    top16 = lambda a: lax.bitcast_convert_type(
        lax.bitcast_convert_type(a, jnp.int32) & jnp.int32(-65536), F32)
    hi = top16(x)
    r1 = x - hi
    mid = top16(r1)
    lo = r1 - mid
    return hi.astype(BF16), mid.astype(BF16), lo.astype(BF16)


def _gla_cumsum_matrix(fwd, nchunks):
    G = nchunks * GLA_CHUNK
    ci = lax.broadcasted_iota(jnp.int32, (G, G), 0)
    si = lax.broadcasted_iota(jnp.int32, (G, G), 1)
    in_order = (si <= ci) if fwd else (si >= ci)
    return jnp.where(in_order & ((ci // GLA_CHUNK) == (si // GLA_CHUNK)), 1.0, 0.0).astype(BF16)


def _gla_group(qk_ref, vg_ref, dl_ref, wdec_ref, bdec_ref, tri, states, r0, fwd, nchunks):
    C, KW, H = GLA_CHUNK, GLA_KEY_WIDTH, GLA_HEADS
    G = nchunks * C
    nseq = len(states)
    d = 0 if fwd else 1
    rows = pl.ds(r0, G)
    last = C - 1 if fwd else 0
    chunk = lambda a, u: a[u * C:(u + 1) * C]
    nt = (((1,), (1,)), ((), ()))

    zs = [(jnp.dot(dl_ref[b, rows, :], wdec_ref[d], preferred_element_type=F32) + bdec_ref[d:d + 1, :]) * LOG2E
          for b in range(nseq)]
    log_a = [(jnp.minimum(x, 0.0) - jnp.log2(1.0 + jnp.exp2(-jnp.abs(x)))) * (1.0 / GLA_GATE_TAU) for x in zs]

    bc = [jnp.dot(tri, jnp.concatenate(_split3(la), axis=1), preferred_element_type=F32) for la in log_a]
    b_cum = [(x[:, 0:KW] + x[:, KW:2 * KW]) + x[:, 2 * KW:3 * KW] for x in bc]
    b_last_rows = [[bcb[u * C + last:u * C + last + 1, :] for u in range(nchunks)] for bcb in b_cum]
    b_last = [jnp.concatenate([jnp.broadcast_to(r, (C, KW)) for r in rws], axis=0) for rws in b_last_rows]

    lane_head = lax.broadcasted_iota(jnp.int32, (1, KW), 1) // GLA_DK
    zero = jnp.zeros((), BF16)
    block_diag = lambda a: jnp.concatenate([jnp.where(lane_head == h, a, zero) for h in range(H)], axis=0)
    ci = lax.broadcasted_iota(jnp.int32, (C, C), 0)
    si = lax.broadcasted_iota(jnp.int32, (C, C), 1)
    causal = (si <= ci) if fwd else (si > ci)

    q_bd, k_t, kend_bd, v_heads = [], [], [], []
    for b in range(nseq):
        q = qk_ref[b, rows, 0:KW].astype(F32)
        k = qk_ref[b, rows, KW:2 * KW].astype(F32)
        v = vg_ref[b, rows, 0:GLA_VAL_WIDTH]
        q_t = ((q * (GLA_DK ** -0.5)) * jnp.exp2(b_cum[b])).astype(BF16)
        k_t.append((k * jnp.exp2(-b_cum[b])).astype(BF16))
        k_end = (k * jnp.exp2(b_last[b] - b_cum[b])).astype(BF16)
        q_bd.append([block_diag(chunk(q_t, u)) for u in range(nchunks)])
        kend_bd.append([block_diag(chunk(k_end, u)) for u in range(nchunks)])
        v_heads.append([[chunk(v, u)[:, h * GLA_DV:(h + 1) * GLA_DV] for h in range(H)] for u in range(nchunks)])

    units = [(b, u) for b in range(nseq) for u in range(nchunks)]
    attn = {bu: lax.dot_general(q_bd[bu[0]][bu[1]], chunk(k_t[bu[0]], bu[1]), nt, preferred_element_type=F32)
            for bu in units}
    attn = {bu: jnp.concatenate([jnp.where(causal, chunk(a, h), 0.0) for h in range(H)], axis=0).astype(BF16)
            for bu, a in attn.items()}
    st_add = {(b, u): lax.dot_general(jnp.concatenate(v_heads[b][u], axis=0), kend_bd[b][u],
                                      (((0,), (0,)), ((), ())), preferred_element_type=F32) for b, u in units}
    o_intra = {(b, u): jnp.concatenate([jnp.dot(chunk(attn[b, u], h), v_heads[b][u][h],
                                                preferred_element_type=F32) for h in range(H)], axis=0)
               for b, u in units}
    outs = [[None] * nchunks for _ in range(nseq)]
    states = list(states)
    for u in (range(nchunks) if fwd else reversed(range(nchunks))):
        for b in range(nseq):
            st = states[b]
            outs[b][u] = o_intra[b, u] + lax.dot_general(q_bd[b][u], st.astype(BF16), nt,
                                                         preferred_element_type=F32)
            states[b] = jnp.exp2(b_last_rows[b][u]) * st + st_add[b, u]
    return outs, states


def _gla_body(qk_ref, vg_ref, dl_ref, wdec_ref, bdec_ref, ng_ref, o_ref, st_ref, ob_ref, *, cbk, nbk):
    C, H = GLA_CHUNK, GLA_HEADS
    nseq = qk_ref.shape[0]
    ph = pl.program_id(1)
    j = pl.program_id(2)
    chunks_per_block = cbk // C
    nchunks = min(GLA_GROUP, chunks_per_block)
    ngroups = chunks_per_block // nchunks

    @pl.when(j == 0)
    def _():
        st_ref[...] = jnp.zeros_like(st_ref)

    def sweep(fwd):
        blk = j if fwd else nbk - 1 - j
        tri = _gla_cumsum_matrix(fwd, nchunks)

        def group(gi, carry):
            g0 = gi if fwd else ngroups - 1 - gi
            r0 = pl.multiple_of(g0 * nchunks * C, nchunks * C)
            outs, states = _gla_group(qk_ref, vg_ref, dl_ref, wdec_ref, bdec_ref, tri,
                                      [st_ref[b] for b in range(nseq)], r0, fwd, nchunks)
            for b in range(nseq):
                st_ref[b] = states[b]
                for u in range(nchunks):
                    cg = blk * chunks_per_block + g0 * nchunks + u
                    if not fwd:
                        ob_ref[b, cg] = outs[b][u]
                        continue
                    c0 = pl.multiple_of(r0 + u * C, C)
                    o = outs[b][u] + ob_ref[b, cg]
                    o = o * lax.rsqrt(jnp.mean(o * o, axis=-1, keepdims=True) + RMS_EPS) * ng_ref[...]
                    gate = vg_ref[b, pl.ds(c0, C), GLA_VAL_WIDTH:2 * GLA_VAL_WIDTH].astype(F32)
                    gate = jnp.concatenate([gate[:, h * GLA_DV:(h + 1) * GLA_DV] for h in range(H)], axis=0)
                    o = o * (gate * jax.nn.sigmoid(gate))
                    o_ref[b, pl.ds(c0, C), :] = jnp.concatenate(
                        [o[h * C:(h + 1) * C] for h in range(H)], axis=1).astype(BF16)
            return carry

        lax.fori_loop(0, ngroups, group, 0)

    @pl.when(ph == 0)
    def _():
        sweep(False)

    @pl.when(ph == 1)
    def _():
        sweep(True)


def _gla(gqk, gvg, gdl, wdec_pad, bdec, norm_g):
    b, s, _ = gqk.shape
    cbk = min(GLA_BLOCK, s)
    nbk = s // cbk
    nseq = GLA_SEQS if b % GLA_SEQS == 0 else 1
    blk = lambda w: pl.BlockSpec((nseq, cbk, w), lambda bi, ph, j: (bi, jnp.where(ph == 0, nbk - 1 - j, j), 0))
    return pl.pallas_call(
        functools.partial(_gla_body, cbk=cbk, nbk=nbk),
        grid=(b // nseq, 2, nbk),
        in_specs=[blk(2 * GLA_KEY_WIDTH), blk(2 * GLA_VAL_WIDTH), blk(2 * GLA_GATE_RANK),
                  _const_spec(wdec_pad.shape), _const_spec(bdec.shape), _const_spec(norm_g.shape)],
        out_specs=pl.BlockSpec((nseq, cbk, GLA_VAL_WIDTH), lambda bi, ph, j: (bi, jnp.where(ph == 0, 0, j), 0)),
        out_shape=jax.ShapeDtypeStruct((b, s, GLA_VAL_WIDTH), BF16),
        scratch_shapes=[pltpu.VMEM((nseq, GLA_DV, GLA_KEY_WIDTH), F32),
                        pltpu.VMEM((nseq, s // GLA_CHUNK, GLA_HEADS * GLA_CHUNK, GLA_DV), F32)],
        compiler_params=pltpu.CompilerParams(dimension_semantics=("arbitrary", "arbitrary", "arbitrary"),
                                             vmem_limit_bytes=VMEM_LIMIT_BYTES),
        name="gla_bidirectional",
    )(gqk, gvg, gdl, wdec_pad, bdec, norm_g)


def _merge_ffn_body(x1_ref, na_ref, gla_ref, wgate_ref, wbn_ref, wbg_ref, wo_ref, g2_ref, b2_ref,
                    wg_ref, wu_ref, wd_ref, g3_ref, b3_ref, o_ref):
    tm = x1_ref.shape[0]
    subs = [pl.ds(r, tm // SUB_TILES) for r in range(0, tm, tm // SUB_TILES)]
    dot = functools.partial(jnp.dot, preferred_element_type=F32)
    x1s = [x1_ref[rows, :] for rows in subs]
    pre = [(dot(x1.astype(BF16), wgate_ref[...]), dot(na_ref[rows, :], wbn_ref[...]),
            dot(gla_ref[rows, :], wbg_ref[...])) for rows, x1 in zip(subs, x1s)]
    z2s = []
    for x1, (gates, y_na, y_gla) in zip(x1s, pre):
        merged = jax.nn.sigmoid(gates[:, :D_MODEL]) * y_na + jax.nn.sigmoid(gates[:, D_MODEL:]) * y_gla
        z2s.append(ALPHA * x1 + dot(merged.astype(BF16), wo_ref[...]))
    x2s = [_layer_norm(z2, g2_ref[...], b2_ref[...]) for z2 in z2s]
    z3s = [ALPHA * x2 + 0.5 * _swiglu(x2.astype(BF16), wg_ref, wu_ref, wd_ref) for x2 in x2s]
    for rows, z3 in zip(subs, z3s):
        o_ref[rows, :] = _layer_norm(z3, g3_ref[...], b3_ref[...])


def _merge_ffn(x1, na_o, gla_o, wgate, wbn, wbg, wo, g2, b2, wg, wu, wd, g3, b3):
    t = x1.shape[0]
    tm = min(TOKEN_TILE, t)
    row = lambda w: pl.BlockSpec((tm, w), lambda i: (i, 0))
    consts = (wgate, wbn, wbg, wo, g2, b2, wg, wu, wd, g3, b3)
    return pl.pallas_call(
        _merge_ffn_body,
        grid=(t // tm,),
        in_specs=[row(D_MODEL), row(NA_WIDTH), row(GLA_VAL_WIDTH)] + [_const_spec(c.shape) for c in consts],
        out_specs=row(D_MODEL),
        out_shape=jax.ShapeDtypeStruct((t, D_MODEL), F32),
        compiler_params=pltpu.CompilerParams(dimension_semantics=("arbitrary",),
                                             vmem_limit_bytes=VMEM_LIMIT_BYTES),
        name="merge_ln2_ffn2_ln3",
    )(x1, na_o, gla_o, *consts)


def _layer(x, layer, late_stacked, ffn1_w_gate, ffn1_w_up, ffn1_w_down, ln1_g, ln1_b, w_in, na_rpb, gla_w_dec2,
           gla_b_dec, gla_norm_g, ln2_g, ln2_b, ln3_g, ln3_b):
    b, s, d = x.shape
    assert d == D_MODEL and s % (NA_UNROLL * NA_QT) == 0 and s // GRID_W >= NA_BAND
    rows = s // GRID_W
    bf = lambda w: w.astype(BF16)
    vec = lambda p: p.reshape(1, -1)

    x1, nak, naqt, navt, gqk, gvg, gdl, w2_gate, w2_up, w2_down, wo, wbn, wbg, w_merge_gates = _ffn_inproj(
        x.reshape(b * s, d), bf(ffn1_w_gate), bf(ffn1_w_up), bf(ffn1_w_down), vec(ln1_g), vec(ln1_b), bf(w_in),
        late_stacked, layer)

    na_o = _neighborhood_attention(nak.reshape(b, s, -1), naqt, navt, _na_bias_tiles(na_rpb), rows)

    zr = jnp.zeros((GLA_GATE_RANK, GLA_KEY_WIDTH), gla_w_dec2.dtype)
    wdec_pad = bf(jnp.stack([jnp.concatenate([gla_w_dec2[0], zr]), jnp.concatenate([zr, gla_w_dec2[1]])]))
    gla_o = _gla(gqk.reshape(b, s, -1), gvg.reshape(b, s, -1), gdl.reshape(b, s, -1),
                 wdec_pad, gla_b_dec, vec(gla_norm_g))

    out = _merge_ffn(x1, na_o.reshape(b * s, -1), gla_o.reshape(b * s, -1), w_merge_gates, wbn, wbg, wo,
                     vec(ln2_g), vec(ln2_b), w2_gate, w2_up, w2_down, vec(ln3_g), vec(ln3_b))
    return out.reshape(b, s, d)


def kernel(x, ffn1_w_gate, ffn1_w_up, ffn1_w_down, ln1_g, ln1_b, w_in, na_rpb, gla_w_dec2, gla_b_dec, gla_norm_g,
           w_branch_na, w_branch_gla, w_out, ln2_g, ln2_b, ffn2_w_gate, ffn2_w_up, ffn2_w_down, ln3_g, ln3_b):
    params = (ffn1_w_gate, ffn1_w_up, ffn1_w_down, ln1_g, ln1_b, w_in, na_rpb, gla_w_dec2, gla_b_dec, gla_norm_g,
              ln2_g, ln2_b, ln3_g, ln3_b)
    late_stacked = (ffn2_w_gate, ffn2_w_up, ffn2_w_down, w_out, w_branch_na, w_branch_gla, w_in)
    for l in range(DEPTH):
        x = _layer(x, l, late_stacked, *(p[l] for p in params))
    return x
```

```python
import functools

import jax
import jax.numpy as jnp
import numpy as np
from jax import lax
from jax.experimental import pallas as pl
from jax.experimental.pallas import tpu as pltpu

F32 = jnp.float32
BF16 = jnp.bfloat16

D_MODEL = 1024
D_FF = 2816
GRID_W = 64
NA_HEADS = 8
NA_HEAD_DIM = 64
NA_WIDTH = NA_HEADS * NA_HEAD_DIM
NA_WIN_H = 8
NA_WIN_W = 16
GLA_HEADS = 4
GLA_DK = 64
GLA_DV = 128
GLA_KEY_WIDTH = GLA_HEADS * GLA_DK
GLA_VAL_WIDTH = GLA_HEADS * GLA_DV
GLA_GATE_RANK = 16
GLA_GATE_TAU = 16.0
GLA_CHUNK = 64
DEPTH = 1
ALPHA = (2 * DEPTH) ** 0.25
LN_EPS = 1e-5
RMS_EPS = 1e-6

OFF_NA_END = 3 * NA_WIDTH
OFF_GQK_END = OFF_NA_END + 2 * GLA_KEY_WIDTH
OFF_GVG_END = OFF_GQK_END + 2 * GLA_VAL_WIDTH
OFF_GDL_END = OFF_GVG_END + 2 * GLA_GATE_RANK
N_IN = OFF_GDL_END + 2 * D_MODEL

LANES = 128
BF16_SUBLANES = 16
N_LATE_WEIGHTS = 7
VMEM_LIMIT_BYTES = 56 * 1024 * 1024

NA_RQ = 4
NA_BAND = NA_RQ + NA_WIN_H
NA_QT = NA_RQ * GRID_W
NA_KT = NA_BAND * GRID_W
NA_UNROLL = 8
NA_ONES_ROWS = 16
NEG_BIG = -1e30
LOG2E = 1.4426950408889634
NA_Q_SCALE = LOG2E * NA_HEAD_DIM ** -0.5

GLA_GROUP = 4
GLA_BLOCK = 1024
GLA_SEQS = 2
TOKEN_TILE = 512
SUB_TILES = 2


def _const_spec(shape):
    nd = len(shape)
    return pl.BlockSpec(shape, lambda *_: (0,) * nd, pipeline_mode=pl.Buffered(1))


def _layer_norm(z, g, b):
    mu = jnp.mean(z, axis=-1, keepdims=True)
    zc = z - mu
    var = jnp.mean(zc * zc, axis=-1, keepdims=True)
    return zc * lax.rsqrt(var + LN_EPS) * g + b


def _swiglu(xb, wg_ref, wu_ref, wd_ref):
    g = jnp.dot(xb, wg_ref[...], preferred_element_type=F32)
    u = jnp.dot(xb, wu_ref[...], preferred_element_type=F32)
    h = (g * jax.nn.sigmoid(g)) * u
    return jnp.dot(h.astype(BF16), wd_ref[...], preferred_element_type=F32)


def _ffn_inproj_body(x_ref, wg_ref, wu_ref, wd_ref, g1_ref, b1_ref, win_ref, *refs):
    late_f32, outs = refs[:N_LATE_WEIGHTS], refs[N_LATE_WEIGHTS:]
    x1_ref, nak_ref, naqt_ref, navt_ref, gqk_ref, gvg_ref, gdl_ref = outs[:-N_LATE_WEIGHTS]
    late_bf16 = outs[-N_LATE_WEIGHTS:]
    for src, dst in zip(late_f32[:-1], late_bf16[:-1]):
        dst[...] = src[...].astype(BF16)
    late_bf16[-1][...] = late_f32[-1][:, OFF_GDL_END:].astype(BF16)

    tm = x_ref.shape[0]
    subs = [pl.ds(r, tm // SUB_TILES) for r in range(0, tm, tm // SUB_TILES)]
    zs = []
    for rows in subs:
        x = x_ref[rows, :]
        zs.append(ALPHA * x + 0.5 * _swiglu(x.astype(BF16), wg_ref, wu_ref, wd_ref))
    for g, (rows, z) in enumerate(zip(subs, zs)):
        x1 = _layer_norm(z, g1_ref[...], b1_ref[...])
        x1_ref[rows, :] = x1
        proj = jnp.dot(x1.astype(BF16), win_ref[:, :OFF_GDL_END], preferred_element_type=F32)
        nak_ref[rows, :] = proj[:, NA_WIDTH:2 * NA_WIDTH].astype(BF16)
        naqt_ref[g] = (proj[:, :NA_WIDTH] * NA_Q_SCALE).T.astype(BF16)
        navt_ref[g] = proj[:, 2 * NA_WIDTH:OFF_NA_END].T.astype(BF16)
        gqk_ref[rows, :] = proj[:, OFF_NA_END:OFF_GQK_END].astype(BF16)
        gvg_ref[rows, :] = proj[:, OFF_GQK_END:OFF_GVG_END].astype(BF16)
        gdl_ref[rows, :] = proj[:, OFF_GVG_END:OFF_GDL_END].astype(BF16)


def _row_chunk(nrows, nsteps):
    return next(c for c in range(BF16_SUBLANES, nrows + 1, BF16_SUBLANES) if nrows % c == 0 and c * nsteps >= nrows)


def _ffn_inproj(x2d, wg, wu, wd, g1, b1, win, late_weights, layer):
    t = x2d.shape[0]
    tm = min(TOKEN_TILE, t)
    n = t // tm
    assert tm // SUB_TILES == NA_QT and len(late_weights) == N_LATE_WEIGHTS
    row = lambda w: pl.BlockSpec((tm, w), lambda i: (i, 0))
    widths = (NA_WIDTH, OFF_GQK_END - OFF_NA_END, OFF_GVG_END - OFF_GQK_END, OFF_GDL_END - OFF_GVG_END)
    rows_bf16 = [jax.ShapeDtypeStruct((t, w), BF16) for w in widths]
    late_shapes = [w.shape[1:] for w in late_weights[:-1]] + [(D_MODEL, N_IN - OFF_GDL_END)]

    def chunk_specs(w_rows, cols_in, cols_out):
        c = _row_chunk(w_rows, n)
        last = w_rows // c - 1
        return (pl.BlockSpec((None, c, cols_in), lambda i: (layer, jnp.minimum(i, last), 0)),
                pl.BlockSpec((c, cols_out), lambda i: (jnp.minimum(i, last), 0)))

    late_specs = [chunk_specs(w.shape[1], w.shape[2], so[1]) for w, so in zip(late_weights, late_shapes)]
    return pl.pallas_call(
        _ffn_inproj_body,
        grid=(n,),
        in_specs=[row(D_MODEL), _const_spec(wg.shape), _const_spec(wu.shape), _const_spec(wd.shape),
                  _const_spec(g1.shape), _const_spec(b1.shape), _const_spec(win.shape)]
                 + [si for si, _ in late_specs],
        out_specs=[row(D_MODEL), row(widths[0])] + [pl.BlockSpec((SUB_TILES, NA_WIDTH, NA_QT), lambda i: (i, 0, 0))] * 2
                  + [row(w) for w in widths[1:]] + [so for _, so in late_specs],
        out_shape=[jax.ShapeDtypeStruct((t, D_MODEL), F32), rows_bf16[0]]
                  + [jax.ShapeDtypeStruct((t // NA_QT, NA_WIDTH, NA_QT), BF16)] * 2 + rows_bf16[1:]
                  + [jax.ShapeDtypeStruct(so, BF16) for so in late_shapes],
        compiler_params=pltpu.CompilerParams(dimension_semantics=("arbitrary",),
                                             vmem_limit_bytes=VMEM_LIMIT_BYTES),
        name="ffn1_ln1_inproj",
    )(x2d, wg, wu, wd, g1, b1, win, *late_weights)


NA_ROW_OFFSETS = 2 * NA_WIN_H - 1


def _na_bias_tiles(rpb):
    kw = NA_WIN_W
    c = np.arange(GRID_W)[None, :]
    kc = np.arange(GRID_W)[:, None]
    c_start = np.clip(c - kw // 2, 0, GRID_W - kw)
    col_ok = (kc >= c_start) & (kc < c_start + kw)
    dc = kc - c + (kw - 1)
    sel_c = ((dc[..., None] == np.arange(2 * kw - 1)) & col_ok[..., None]).astype(np.float32)
    tiles = jnp.einsum('hdw,kcw->hdkc', rpb, sel_c, precision=lax.Precision.HIGHEST)
    tiles = jnp.where(col_ok, tiles * LOG2E, NEG_BIG)
    tiles = jnp.concatenate([tiles, jnp.full_like(tiles[:, :1], NEG_BIG)], axis=1)
    return jnp.concatenate([tiles, tiles], axis=-1).astype(F32)


def _na_row_offsets(rows):
    kh = NA_WIN_H
    ri = np.arange(NA_RQ)[:, None]
    bj = np.arange(NA_BAND)[None, :]
    out = []
    for r0, b0 in ((0, 0), (NA_RQ, 0), (rows - NA_RQ, rows - NA_BAND)):
        r = r0 + ri
        kr = b0 + bj
        r_start = np.clip(r - kh // 2, 0, rows - kh)
        row_ok = (kr >= r_start) & (kr < r_start + kh)
        out.append(np.where(row_ok, kr - r + (kh - 1), NA_ROW_OFFSETS).tolist())
    return out


def _na_body(qt_ref, k_ref, vt_ref, tile_ref, o_ref, s_ref, bias_ref, m_ref, *, rows):
    nblk = rows // NA_RQ
    lane = lax.broadcasted_iota(jnp.int32, (1, LANES), 1)
    first_head = lane < NA_HEAD_DIM
    first_head_row = lax.broadcasted_iota(jnp.int32, (LANES, 1), 0) < NA_HEAD_DIM

    @pl.when(pl.program_id(1) == 0)
    def _():
        for kind, offsets in enumerate(_na_row_offsets(rows)):
            for hh in range(2):
                for bj in range(NA_BAND):
                    for ri in range(0, NA_RQ, 2):
                        tile = jnp.where(first_head, tile_ref[hh, offsets[ri][bj]], tile_ref[hh, offsets[ri + 1][bj]])
                        bias_ref[kind, pl.ds(bj * GRID_W, GRID_W), pl.ds(hh * NA_QT + ri * GRID_W, LANES)] = tile

    def band_row(i):
        return jnp.clip(NA_RQ * i - NA_WIN_H // 2, 0, rows - NA_BAND)

    def scores(i, slot):
        q0 = pl.multiple_of(i * NA_QT, NA_QT)
        k0 = pl.multiple_of(band_row(i) * GRID_W, NA_QT)
        kind = jnp.where(i == 0, 0, jnp.where(i == nblk - 1, 2, 1))
        qt = qt_ref[i]
        k = k_ref[0, pl.ds(k0, NA_KT), :]
        zero = jnp.zeros((), BF16)
        qt2 = jnp.concatenate([jnp.where(first_head_row, qt, zero), jnp.where(first_head_row, zero, qt)], axis=1)
        s = jnp.dot(k, qt2, preferred_element_type=F32)
        s = s + bias_ref[kind]
        s_ref[slot] = s
        m_ref[slot] = jnp.max(s, axis=0, keepdims=True)

    def finish(i, slot):
        q0 = pl.multiple_of(i * NA_QT, NA_QT)
        g0 = band_row(i) // NA_RQ
        s = s_ref[slot]
        p = jnp.exp2(s - m_ref[slot]).astype(BF16)
        ones = jnp.ones((NA_ONES_ROWS, NA_QT), BF16)
        heads = []
        for hh in range(2):
            acc = None
            for g in range(NA_BAND // NA_RQ):
                vt = jnp.concatenate([vt_ref[g0 + g, hh * NA_HEAD_DIM:(hh + 1) * NA_HEAD_DIM, :], ones], axis=0)
                part = jnp.dot(vt, p[g * NA_QT:(g + 1) * NA_QT, hh * NA_QT:(hh + 1) * NA_QT],
                               preferred_element_type=F32)
                acc = part if acc is None else acc + part
            heads.append(acc[:NA_HEAD_DIM] * (1.0 / acc[NA_HEAD_DIM:NA_HEAD_DIM + 1]))
        o_ref[0, pl.ds(q0, NA_QT), :] = jnp.concatenate(heads, axis=0).T.astype(BF16)

    scores(0, 0)

    def group(j, carry):
        for u in range(NA_UNROLL):
            i = NA_UNROLL * j + u
            scores(jnp.minimum(i + 1, nblk - 1), (u + 1) % 2)
            finish(i, u % 2)
        return carry

    lax.fori_loop(0, nblk // NA_UNROLL, group, 0)


def _neighborhood_attention(nak, naqt, navt, tiles, rows):
    b, s, _ = nak.shape
    npairs = NA_HEADS // 2
    tok = pl.BlockSpec((1, s, LANES), lambda p, bi: (bi, 0, p))
    slabs = pl.BlockSpec((s // NA_QT, LANES, NA_QT), lambda p, bi: (bi, p, 0))
    return pl.pallas_call(
        functools.partial(_na_body, rows=rows),
        grid=(npairs, b),
        in_specs=[slabs, tok, slabs,
                  pl.BlockSpec((2, NA_ROW_OFFSETS + 1, GRID_W, LANES), lambda p, bi: (p, 0, 0, 0))],
        out_specs=tok,
        out_shape=jax.ShapeDtypeStruct((b, s, NA_WIDTH), BF16),
        scratch_shapes=[pltpu.VMEM((2, NA_KT, 2 * NA_QT), F32), pltpu.VMEM((3, NA_KT, 2 * NA_QT), F32),
                        pltpu.VMEM((2, 1, 2 * NA_QT), F32)],
        compiler_params=pltpu.CompilerParams(dimension_semantics=("arbitrary", "arbitrary"),
                                             vmem_limit_bytes=VMEM_LIMIT_BYTES),
        name="neighborhood_attention",
    )(naqt, nak, navt, tiles)


def _split3(x):
    top16 = lambda a: lax.bitcast_convert_type(
        lax.bitcast_convert_type(a, jnp.int32) & jnp.int32(-65536), F32)
    hi = top16(x)
    r1 = x - hi
    mid = top16(r1)
    lo = r1 - mid
    return hi.astype(BF16), mid.astype(BF16), lo.astype(BF16)


def _gla_cumsum_matrix(fwd, nchunks):
    G = nchunks * GLA_CHUNK
    ci = lax.broadcasted_iota(jnp.int32, (G, G), 0)
    si = lax.broadcasted_iota(jnp.int32, (G, G), 1)
    in_order = (si <= ci) if fwd else (si >= ci)
    return jnp.where(in_order & ((ci // GLA_CHUNK) == (si // GLA_CHUNK)), 1.0, 0.0).astype(BF16)


def _gla_group(qk_ref, vg_ref, dl_ref, wdec_ref, bdec_ref, tri, states, r0, fwd, nchunks):
    C, KW, H = GLA_CHUNK, GLA_KEY_WIDTH, GLA_HEADS
    G = nchunks * C
    nseq = len(states)
    d = 0 if fwd else 1
    rows = pl.ds(r0, G)
    last = C - 1 if fwd else 0
    chunk = lambda a, u: a[u * C:(u + 1) * C]
    nt = (((1,), (1,)), ((), ()))

    zs = [(jnp.dot(dl_ref[b, rows, :], wdec_ref[d], preferred_element_type=F32) + bdec_ref[d:d + 1, :]) * LOG2E
          for b in range(nseq)]
    log_a = [(jnp.minimum(x, 0.0) - jnp.log2(1.0 + jnp.exp2(-jnp.abs(x)))) * (1.0 / GLA_GATE_TAU) for x in zs]

    bc = [jnp.dot(tri, jnp.concatenate(_split3(la), axis=1), preferred_element_type=F32) for la in log_a]
    b_cum = [(x[:, 0:KW] + x[:, KW:2 * KW]) + x[:, 2 * KW:3 * KW] for x in bc]
    b_last_rows = [[bcb[u * C + last:u * C + last + 1, :] for u in range(nchunks)] for bcb in b_cum]
    b_last = [jnp.concatenate([jnp.broadcast_to(r, (C, KW)) for r in rws], axis=0) for rws in b_last_rows]

    lane_head = lax.broadcasted_iota(jnp.int32, (1, KW), 1) // GLA_DK
    zero = jnp.zeros((), BF16)
    block_diag = lambda a: jnp.concatenate([jnp.where(lane_head == h, a, zero) for h in range(H)], axis=0)
    ci = lax.broadcasted_iota(jnp.int32, (C, C), 0)
    si = lax.broadcasted_iota(jnp.int32, (C, C), 1)
    causal = (si <= ci) if fwd else (si > ci)

    q_bd, k_t, kend_bd, v_heads = [], [], [], []
    for b in range(nseq):
        q = qk_ref[b, rows, 0:KW].astype(F32)
        k = qk_ref[b, rows, KW:2 * KW].astype(F32)
        v = vg_ref[b, rows, 0:GLA_VAL_WIDTH]
        q_t = ((q * (GLA_DK ** -0.5)) * jnp.exp2(b_cum[b])).astype(BF16)
        k_t.append((k * jnp.exp2(-b_cum[b])).astype(BF16))
        k_end = (k * jnp.exp2(b_last[b] - b_cum[b])).astype(BF16)
        q_bd.append([block_diag(chunk(q_t, u)) for u in range(nchunks)])
        kend_bd.append([block_diag(chunk(k_end, u)) for u in range(nchunks)])
        v_heads.append([[chunk(v, u)[:, h * GLA_DV:(h + 1) * GLA_DV] for h in range(H)] for u in range(nchunks)])

    units = [(b, u) for b in range(nseq) for u in range(nchunks)]
    attn = {bu: lax.dot_general(q_bd[bu[0]][bu[1]], chunk(k_t[bu[0]], bu[1]), nt, preferred_element_type=F32)
            for bu in units}
    attn = {bu: jnp.concatenate([jnp.where(causal, chunk(a, h), 0.0) for h in range(H)], axis=0).astype(BF16)
            for bu, a in attn.items()}
    st_add = {(b, u): lax.dot_general(jnp.concatenate(v_heads[b][u], axis=0), kend_bd[b][u],
                                      (((0,), (0,)), ((), ())), preferred_element_type=F32) for b, u in units}
    o_intra = {(b, u): jnp.concatenate([jnp.dot(chunk(attn[b, u], h), v_heads[b][u][h],
                                                preferred_element_type=F32) for h in range(H)], axis=0)
               for b, u in units}
    outs = [[None] * nchunks for _ in range(nseq)]
    states = list(states)
    for u in (range(nchunks) if fwd else reversed(range(nchunks))):
        for b in range(nseq):
            st = states[b]
            outs[b][u] = o_intra[b, u] + lax.dot_general(q_bd[b][u], st.astype(BF16), nt,
                                                         preferred_element_type=F32)
            states[b] = jnp.exp2(b_last_rows[b][u]) * st + st_add[b, u]
    return outs, states


def _gla_body(qk_ref, vg_ref, dl_ref, wdec_ref, bdec_ref, ng_ref, o_ref, st_ref, ob_ref, *, cbk, nbk):
    C, H = GLA_CHUNK, GLA_HEADS
    nseq = qk_ref.shape[0]
    ph = pl.program_id(1)
    j = pl.program_id(2)
    chunks_per_block = cbk // C
    nchunks = min(GLA_GROUP, chunks_per_block)
    ngroups = chunks_per_block // nchunks

    @pl.when(j == 0)
    def _():
        st_ref[...] = jnp.zeros_like(st_ref)

    def sweep(fwd):
        blk = j if fwd else nbk - 1 - j
        tri = _gla_cumsum_matrix(fwd, nchunks)

        def group(gi, carry):
            g0 = gi if fwd else ngroups - 1 - gi
            r0 = pl.multiple_of(g0 * nchunks * C, nchunks * C)
            outs, states = _gla_group(qk_ref, vg_ref, dl_ref, wdec_ref, bdec_ref, tri,
                                      [st_ref[b] for b in range(nseq)], r0, fwd, nchunks)
            for b in range(nseq):
                st_ref[b] = states[b]
                for u in range(nchunks):
                    cg = blk * chunks_per_block + g0 * nchunks + u
                    if not fwd:
                        ob_ref[b, cg] = outs[b][u]
                        continue
                    c0 = pl.multiple_of(r0 + u * C, C)
                    o = outs[b][u] + ob_ref[b, cg]
                    o = o * lax.rsqrt(jnp.mean(o * o, axis=-1, keepdims=True) + RMS_EPS) * ng_ref[...]
                    gate = vg_ref[b, pl.ds(c0, C), GLA_VAL_WIDTH:2 * GLA_VAL_WIDTH].astype(F32)
                    gate = jnp.concatenate([gate[:, h * GLA_DV:(h + 1) * GLA_DV] for h in range(H)], axis=0)
                    o = o * (gate * jax.nn.sigmoid(gate))
                    o_ref[b, pl.ds(c0, C), :] = jnp.concatenate(
                        [o[h * C:(h + 1) * C] for h in range(H)], axis=1).astype(BF16)
            return carry

        lax.fori_loop(0, ngroups, group, 0)

    @pl.when(ph == 0)
    def _():
        sweep(False)

    @pl.when(ph == 1)
    def _():
        sweep(True)


def _gla(gqk, gvg, gdl, wdec_pad, bdec, norm_g):
    b, s, _ = gqk.shape
    cbk = min(GLA_BLOCK, s)
    nbk = s // cbk
    nseq = GLA_SEQS if b % GLA_SEQS == 0 else 1
    blk = lambda w: pl.BlockSpec((nseq, cbk, w), lambda bi, ph, j: (bi, jnp.where(ph == 0, nbk - 1 - j, j), 0))
    return pl.pallas_call(
        functools.partial(_gla_body, cbk=cbk, nbk=nbk),
        grid=(b // nseq, 2, nbk),
        in_specs=[blk(2 * GLA_KEY_WIDTH), blk(2 * GLA_VAL_WIDTH), blk(2 * GLA_GATE_RANK),
                  _const_spec(wdec_pad.shape), _const_spec(bdec.shape), _const_spec(norm_g.shape)],
        out_specs=pl.BlockSpec((nseq, cbk, GLA_VAL_WIDTH), lambda bi, ph, j: (bi, jnp.where(ph == 0, 0, j), 0)),
        out_shape=jax.ShapeDtypeStruct((b, s, GLA_VAL_WIDTH), BF16),
        scratch_shapes=[pltpu.VMEM((nseq, GLA_DV, GLA_KEY_WIDTH), F32),
                        pltpu.VMEM((nseq, s // GLA_CHUNK, GLA_HEADS * GLA_CHUNK, GLA_DV), F32)],
        compiler_params=pltpu.CompilerParams(dimension_semantics=("arbitrary", "arbitrary", "arbitrary"),
                                             vmem_limit_bytes=VMEM_LIMIT_BYTES),
        name="gla_bidirectional",
    )(gqk, gvg, gdl, wdec_pad, bdec, norm_g)


def _merge_ffn_body(x1_ref, na_ref, gla_ref, wgate_ref, wbn_ref, wbg_ref, wo_ref, g2_ref, b2_ref,
                    wg_ref, wu_ref, wd_ref, g3_ref, b3_ref, o_ref):
    tm = x1_ref.shape[0]
    subs = [pl.ds(r, tm // SUB_TILES) for r in range(0, tm, tm // SUB_TILES)]
    dot = functools.partial(jnp.dot, preferred_element_type=F32)
    x1s = [x1_ref[rows, :] for rows in subs]
    pre = [(dot(x1.astype(BF16), wgate_ref[...]), dot(na_ref[rows, :], wbn_ref[...]),
            dot(gla_ref[rows, :], wbg_ref[...])) for rows, x1 in zip(subs, x1s)]
    z2s = []
    for x1, (gates, y_na, y_gla) in zip(x1s, pre):
        merged = jax.nn.sigmoid(gates[:, :D_MODEL]) * y_na + jax.nn.sigmoid(gates[:, D_MODEL:]) * y_gla
        z2s.append(ALPHA * x1 + dot(merged.astype(BF16), wo_ref[...]))
    x2s = [_layer_norm(z2, g2_ref[...], b2_ref[...]) for z2 in z2s]
    z3s = [ALPHA * x2 + 0.5 * _swiglu(x2.astype(BF16), wg_ref, wu_ref, wd_ref) for x2 in x2s]
    for rows, z3 in zip(subs, z3s):
        o_ref[rows, :] = _layer_norm(z3, g3_ref[...], b3_ref[...])


def _merge_ffn(x1, na_o, gla_o, wgate, wbn, wbg, wo, g2, b2, wg, wu, wd, g3, b3):
    t = x1.shape[0]
    tm = min(TOKEN_TILE, t)
    row = lambda w: pl.BlockSpec((tm, w), lambda i: (i, 0))
    consts = (wgate, wbn, wbg, wo, g2, b2, wg, wu, wd, g3, b3)
    return pl.pallas_call(
        _merge_ffn_body,
        grid=(t // tm,),
        in_specs=[row(D_MODEL), row(NA_WIDTH), row(GLA_VAL_WIDTH)] + [_const_spec(c.shape) for c in consts],
        out_specs=row(D_MODEL),
        out_shape=jax.ShapeDtypeStruct((t, D_MODEL), F32),
        compiler_params=pltpu.CompilerParams(dimension_semantics=("arbitrary",),
                                             vmem_limit_bytes=VMEM_LIMIT_BYTES),
        name="merge_ln2_ffn2_ln3",
    )(x1, na_o, gla_o, *consts)


def _layer(x, layer, late_stacked, ffn1_w_gate, ffn1_w_up, ffn1_w_down, ln1_g, ln1_b, w_in, na_rpb, gla_w_dec2,
           gla_b_dec, gla_norm_g, ln2_g, ln2_b, ln3_g, ln3_b):
    b, s, d = x.shape
    assert d == D_MODEL and s % (NA_UNROLL * NA_QT) == 0 and s // GRID_W >= NA_BAND
    rows = s // GRID_W
    bf = lambda w: w.astype(BF16)
    vec = lambda p: p.reshape(1, -1)

    x1, nak, naqt, navt, gqk, gvg, gdl, w2_gate, w2_up, w2_down, wo, wbn, wbg, w_merge_gates = _ffn_inproj(
        x.reshape(b * s, d), bf(ffn1_w_gate), bf(ffn1_w_up), bf(ffn1_w_down), vec(ln1_g), vec(ln1_b), bf(w_in),
        late_stacked, layer)

    na_o = _neighborhood_attention(nak.reshape(b, s, -1), naqt, navt, _na_bias_tiles(na_rpb), rows)

    zr = jnp.zeros((GLA_GATE_RANK, GLA_KEY_WIDTH), gla_w_dec2.dtype)
    wdec_pad = bf(jnp.stack([jnp.concatenate([gla_w_dec2[0], zr]), jnp.concatenate([zr, gla_w_dec2[1]])]))
    gla_o = _gla(gqk.reshape(b, s, -1), gvg.reshape(b, s, -1), gdl.reshape(b, s, -1),
                 wdec_pad, gla_b_dec, vec(gla_norm_g))

    out = _merge_ffn(x1, na_o.reshape(b * s, -1), gla_o.reshape(b * s, -1), w_merge_gates, wbn, wbg, wo,
                     vec(ln2_g), vec(ln2_b), w2_gate, w2_up, w2_down, vec(ln3_g), vec(ln3_b))
    return out.reshape(b, s, d)


def kernel(x, ffn1_w_gate, ffn1_w_up, ffn1_w_down, ln1_g, ln1_b, w_in, na_rpb, gla_w_dec2, gla_b_dec, gla_norm_g,
           w_branch_na, w_branch_gla, w_out, ln2_g, ln2_b, ffn2_w_gate, ffn2_w_up, ffn2_w_down, ln3_g, ln3_b):
    params = (ffn1_w_gate, ffn1_w_up, ffn1_w_down, ln1_g, ln1_b, w_in, na_rpb, gla_w_dec2, gla_b_dec, gla_norm_g,
              ln2_g, ln2_b, ln3_g, ln3_b)
    late_stacked = (ffn2_w_gate, ffn2_w_up, ffn2_w_down, w_out, w_branch_na, w_branch_gla, w_in)
    for l in range(DEPTH):
        x = _layer(x, l, late_stacked, *(p[l] for p in params))
    return x
```

```python
import functools

import jax
import jax.numpy as jnp
import numpy as np
from jax import lax
from jax.experimental import pallas as pl
from jax.experimental.pallas import tpu as pltpu

F32 = jnp.float32
BF16 = jnp.bfloat16

D_MODEL = 1024
D_FF = 2816
GRID_W = 64
NA_HEADS = 8
NA_HEAD_DIM = 64
NA_WIDTH = NA_HEADS * NA_HEAD_DIM
NA_WIN_H = 8
NA_WIN_W = 16
GLA_HEADS = 4
GLA_DK = 64
GLA_DV = 128
GLA_KEY_WIDTH = GLA_HEADS * GLA_DK
GLA_VAL_WIDTH = GLA_HEADS * GLA_DV
GLA_GATE_RANK = 16
GLA_GATE_TAU = 16.0
GLA_CHUNK = 64
DEPTH = 1
ALPHA = (2 * DEPTH) ** 0.25
LN_EPS = 1e-5
RMS_EPS = 1e-6

OFF_NA_END = 3 * NA_WIDTH
OFF_GQK_END = OFF_NA_END + 2 * GLA_KEY_WIDTH
OFF_GVG_END = OFF_GQK_END + 2 * GLA_VAL_WIDTH
OFF_GDL_END = OFF_GVG_END + 2 * GLA_GATE_RANK
N_IN = OFF_GDL_END + 2 * D_MODEL

LANES = 128
BF16_SUBLANES = 16
N_LATE_WEIGHTS = 7
CAST_STEPS = 8
VMEM_LIMIT_BYTES = 56 * 1024 * 1024

NA_RQ = 4
NA_BAND = NA_RQ + NA_WIN_H
NA_QT = NA_RQ * GRID_W
NA_KT = NA_BAND * GRID_W
NA_UNROLL = 8
NA_ONES_ROWS = 16
NEG_BIG = -1e30
LOG2E = 1.4426950408889634
NA_Q_SCALE = LOG2E * NA_HEAD_DIM ** -0.5

GLA_GROUP = 4
GLA_BLOCK = 1024
GLA_SEQS = 2
TOKEN_TILE = 512
SUB_TILES = 2


def _const_spec(shape):
    nd = len(shape)
    return pl.BlockSpec(shape, lambda *_: (0,) * nd, pipeline_mode=pl.Buffered(1))


def _layer_norm(z, g, b):
    mu = jnp.mean(z, axis=-1, keepdims=True)
    zc = z - mu
    var = jnp.mean(zc * zc, axis=-1, keepdims=True)
    return zc * lax.rsqrt(var + LN_EPS) * g + b


def _swiglu(xb, wg_ref, wu_ref, wd_ref):
    g = jnp.dot(xb, wg_ref[...], preferred_element_type=F32)
    u = jnp.dot(xb, wu_ref[...], preferred_element_type=F32)
    h = (g * jax.nn.sigmoid(g)) * u
    return jnp.dot(h.astype(BF16), wd_ref[...], preferred_element_type=F32)


def _ffn_inproj_body(x_ref, wg_ref, wu_ref, wd_ref, g1_ref, b1_ref, win_ref, *refs):
    late_f32, outs = refs[:N_LATE_WEIGHTS], refs[N_LATE_WEIGHTS:]
    x1_ref, nak_ref, naqt_ref, navt_ref, gqk_ref, gvg_ref, gdl_ref = outs[:-N_LATE_WEIGHTS]
    late_bf16 = outs[-N_LATE_WEIGHTS:]
    for src, dst in zip(late_f32[:-1], late_bf16[:-1]):
        dst[...] = src[...].astype(BF16)
    late_bf16[-1][...] = late_f32[-1][:, OFF_GDL_END:].astype(BF16)

    tm = x_ref.shape[0]
    subs = [pl.ds(r, tm // SUB_TILES) for r in range(0, tm, tm // SUB_TILES)]
    zs = []
    for rows in subs:
        x = x_ref[rows, :]
        zs.append(ALPHA * x + 0.5 * _swiglu(x.astype(BF16), wg_ref, wu_ref, wd_ref))
    for g, (rows, z) in enumerate(zip(subs, zs)):
        x1 = _layer_norm(z, g1_ref[...], b1_ref[...])
        x1_ref[rows, :] = x1
        proj = jnp.dot(x1.astype(BF16), win_ref[:, :OFF_GDL_END], preferred_element_type=F32)
        nak_ref[rows, :] = proj[:, NA_WIDTH:2 * NA_WIDTH].astype(BF16)
        naqt_ref[g] = (proj[:, :NA_WIDTH] * NA_Q_SCALE).T.astype(BF16)
        navt_ref[g] = proj[:, 2 * NA_WIDTH:OFF_NA_END].T.astype(BF16)
        gqk_ref[rows, :] = proj[:, OFF_NA_END:OFF_GQK_END].astype(BF16)
        gvg_ref[rows, :] = proj[:, OFF_GQK_END:OFF_GVG_END].astype(BF16)
        gdl_ref[rows, :] = proj[:, OFF_GVG_END:OFF_GDL_END].astype(BF16)


def _row_chunk(nrows, nsteps):
    return next(c for c in range(BF16_SUBLANES, nrows + 1, BF16_SUBLANES) if nrows % c == 0 and c * nsteps >= nrows)


def _to_bf16_body(*refs):
    for src, dst in zip(refs[:len(refs) // 2], refs[len(refs) // 2:]):
        dst[...] = src[...].astype(BF16)


def _to_bf16(stacked, layer):
    def specs(w):
        c = _row_chunk(w.shape[1], CAST_STEPS)
        last = w.shape[1] // c - 1
        return (pl.BlockSpec((None, c, w.shape[2]), lambda i: (layer, jnp.minimum(i, last), 0)),
                pl.BlockSpec((c, w.shape[2]), lambda i: (jnp.minimum(i, last), 0)))

    in_specs, out_specs = zip(*(specs(w) for w in stacked))
    return pl.pallas_call(
        _to_bf16_body,
        grid=(CAST_STEPS,),
        in_specs=list(in_specs),
        out_specs=list(out_specs),
        out_shape=[jax.ShapeDtypeStruct(w.shape[1:], BF16) for w in stacked],
        compiler_params=pltpu.CompilerParams(dimension_semantics=("arbitrary",),
                                             vmem_limit_bytes=VMEM_LIMIT_BYTES),
        name="weights_to_bf16",
    )(*stacked)


def _ffn_inproj(x2d, wg, wu, wd, g1, b1, win, late_weights, layer):
    t = x2d.shape[0]
    tm = min(TOKEN_TILE, t)
    n = t // tm
    assert tm // SUB_TILES == NA_QT and len(late_weights) == N_LATE_WEIGHTS
    row = lambda w: pl.BlockSpec((tm, w), lambda i: (i, 0))
    widths = (NA_WIDTH, OFF_GQK_END - OFF_NA_END, OFF_GVG_END - OFF_GQK_END, OFF_GDL_END - OFF_GVG_END)
    rows_bf16 = [jax.ShapeDtypeStruct((t, w), BF16) for w in widths]
    late_shapes = [w.shape[1:] for w in late_weights[:-1]] + [(D_MODEL, N_IN - OFF_GDL_END)]

    def chunk_specs(w_rows, cols_in, cols_out):
        c = _row_chunk(w_rows, n)
        last = w_rows // c - 1
        return (pl.BlockSpec((None, c, cols_in), lambda i: (layer, jnp.minimum(i, last), 0)),
                pl.BlockSpec((c, cols_out), lambda i: (jnp.minimum(i, last), 0)))

    late_specs = [chunk_specs(w.shape[1], w.shape[2], so[1]) for w, so in zip(late_weights, late_shapes)]
    return pl.pallas_call(
        _ffn_inproj_body,
        grid=(n,),
        in_specs=[row(D_MODEL), _const_spec(wg.shape), _const_spec(wu.shape), _const_spec(wd.shape),
                  _const_spec(g1.shape), _const_spec(b1.shape), _const_spec(win.shape)]
                 + [si for si, _ in late_specs],
        out_specs=[row(D_MODEL), row(widths[0])] + [pl.BlockSpec((SUB_TILES, NA_WIDTH, NA_QT), lambda i: (i, 0, 0))] * 2
                  + [row(w) for w in widths[1:]] + [so for _, so in late_specs],
        out_shape=[jax.ShapeDtypeStruct((t, D_MODEL), F32), rows_bf16[0]]
                  + [jax.ShapeDtypeStruct((t // NA_QT, NA_WIDTH, NA_QT), BF16)] * 2 + rows_bf16[1:]
                  + [jax.ShapeDtypeStruct(so, BF16) for so in late_shapes],
        compiler_params=pltpu.CompilerParams(dimension_semantics=("arbitrary",),
                                             vmem_limit_bytes=VMEM_LIMIT_BYTES),
        name="ffn1_ln1_inproj",
    )(x2d, wg, wu, wd, g1, b1, win, *late_weights)


NA_ROW_OFFSETS = 2 * NA_WIN_H - 1


def _na_bias_tiles(rpb):
    kw = NA_WIN_W
    c = np.arange(GRID_W)[None, :]
    kc = np.arange(GRID_W)[:, None]
    c_start = np.clip(c - kw // 2, 0, GRID_W - kw)
    col_ok = (kc >= c_start) & (kc < c_start + kw)
    dc = kc - c + (kw - 1)
    sel_c = ((dc[..., None] == np.arange(2 * kw - 1)) & col_ok[..., None]).astype(np.float32)
    tiles = jnp.einsum('hdw,kcw->hdkc', rpb, sel_c, precision=lax.Precision.HIGHEST)
    tiles = jnp.where(col_ok, tiles * LOG2E, NEG_BIG)
    tiles = jnp.concatenate([tiles, jnp.full_like(tiles[:, :1], NEG_BIG)], axis=1)
    return jnp.concatenate([tiles, tiles], axis=-1).astype(F32)


def _na_row_offsets(rows):
    kh = NA_WIN_H
    ri = np.arange(NA_RQ)[:, None]
    bj = np.arange(NA_BAND)[None, :]
    out = []
    for r0, b0 in ((0, 0), (NA_RQ, 0), (rows - NA_RQ, rows - NA_BAND)):
        r = r0 + ri
        kr = b0 + bj
        r_start = np.clip(r - kh // 2, 0, rows - kh)
        row_ok = (kr >= r_start) & (kr < r_start + kh)
        out.append(np.where(row_ok, kr - r + (kh - 1), NA_ROW_OFFSETS).tolist())
    return out


def _na_body(qt_ref, k_ref, vt_ref, tile_ref, o_ref, s_ref, bias_ref, *, rows):
    nblk = rows // NA_RQ
    lane = lax.broadcasted_iota(jnp.int32, (1, LANES), 1)
    first_head = lane < NA_HEAD_DIM
    first_head_row = lax.broadcasted_iota(jnp.int32, (LANES, 1), 0) < NA_HEAD_DIM

    @pl.when(pl.program_id(1) == 0)
    def _():
        for kind, offsets in enumerate(_na_row_offsets(rows)):
            for hh in range(2):
                for bj in range(NA_BAND):
                    for ri in range(0, NA_RQ, 2):
                        tile = jnp.where(first_head, tile_ref[hh, offsets[ri][bj]], tile_ref[hh, offsets[ri + 1][bj]])
                        bias_ref[kind, pl.ds(bj * GRID_W, GRID_W), pl.ds(hh * NA_QT + ri * GRID_W, LANES)] = tile

    def band_row(i):
        return jnp.clip(NA_RQ * i - NA_WIN_H // 2, 0, rows - NA_BAND)

    def scores(i, slot):
        q0 = pl.multiple_of(i * NA_QT, NA_QT)
        k0 = pl.multiple_of(band_row(i) * GRID_W, NA_QT)
        kind = jnp.where(i == 0, 0, jnp.where(i == nblk - 1, 2, 1))
        qt = qt_ref[i]
        k = k_ref[0, pl.ds(k0, NA_KT), :]
        zero = jnp.zeros((), BF16)
        qt2 = jnp.concatenate([jnp.where(first_head_row, qt, zero), jnp.where(first_head_row, zero, qt)], axis=1)
        s = jnp.dot(k, qt2, preferred_element_type=F32)
        s_ref[slot] = s + bias_ref[kind]

    def finish(i, slot):
        q0 = pl.multiple_of(i * NA_QT, NA_QT)
        g0 = band_row(i) // NA_RQ
        s = s_ref[slot]
        p = jnp.exp2(s - jnp.max(s, axis=0, keepdims=True)).astype(BF16)
        ones = jnp.ones((NA_ONES_ROWS, NA_QT), BF16)
        heads = []
        for hh in range(2):
            acc = None
            for g in range(NA_BAND // NA_RQ):
                vt = jnp.concatenate([vt_ref[g0 + g, hh * NA_HEAD_DIM:(hh + 1) * NA_HEAD_DIM, :], ones], axis=0)
                part = jnp.dot(vt, p[g * NA_QT:(g + 1) * NA_QT, hh * NA_QT:(hh + 1) * NA_QT],
                               preferred_element_type=F32)
                acc = part if acc is None else acc + part
            heads.append(acc[:NA_HEAD_DIM] * (1.0 / acc[NA_HEAD_DIM:NA_HEAD_DIM + 1]))
        o_ref[0, pl.ds(q0, NA_QT), :] = jnp.concatenate(heads, axis=0).T.astype(BF16)

    scores(0, 0)

    def group(j, carry):
        for u in range(NA_UNROLL):
            i = NA_UNROLL * j + u
            scores(jnp.minimum(i + 1, nblk - 1), (u + 1) % 2)
            finish(i, u % 2)
        return carry

    lax.fori_loop(0, nblk // NA_UNROLL, group, 0)


def _neighborhood_attention(nak, naqt, navt, tiles, rows):
    b, s, _ = nak.shape
    npairs = NA_HEADS // 2
    tok = pl.BlockSpec((1, s, LANES), lambda p, bi: (bi, 0, p))
    slabs = pl.BlockSpec((s // NA_QT, LANES, NA_QT), lambda p, bi: (bi, p, 0))
    return pl.pallas_call(
        functools.partial(_na_body, rows=rows),
        grid=(npairs, b),
        in_specs=[slabs, tok, slabs,
                  pl.BlockSpec((2, NA_ROW_OFFSETS + 1, GRID_W, LANES), lambda p, bi: (p, 0, 0, 0))],
        out_specs=tok,
        out_shape=jax.ShapeDtypeStruct((b, s, NA_WIDTH), BF16),
        scratch_shapes=[pltpu.VMEM((2, NA_KT, 2 * NA_QT), F32), pltpu.VMEM((3, NA_KT, 2 * NA_QT), F32)],
        compiler_params=pltpu.CompilerParams(dimension_semantics=("arbitrary", "arbitrary"),
                                             vmem_limit_bytes=VMEM_LIMIT_BYTES),
        name="neighborhood_attention",
    )(naqt, nak, navt, tiles)


def _split3(x):
    top16 = lambda a: lax.bitcast_convert_type(
        lax.bitcast_convert_type(a, jnp.int32) & jnp.int32(-65536), F32)
    hi = top16(x)
    r1 = x - hi
    mid = top16(r1)
    lo = r1 - mid
    return hi.astype(BF16), mid.astype(BF16), lo.astype(BF16)


def _gla_cumsum_matrix(fwd, nchunks):
    G = nchunks * GLA_CHUNK
    ci = lax.broadcasted_iota(jnp.int32, (G, G), 0)
    si = lax.broadcasted_iota(jnp.int32, (G, G), 1)
    in_order = (si <= ci) if fwd else (si >= ci)
    return jnp.where(in_order & ((ci // GLA_CHUNK) == (si // GLA_CHUNK)), 1.0, 0.0).astype(BF16)


def _gla_group(qk_ref, vg_ref, dl_ref, wdec_ref, bdec_ref, tri, states, r0, fwd, nchunks):
    C, KW, H = GLA_CHUNK, GLA_KEY_WIDTH, GLA_HEADS
    G = nchunks * C
    nseq = len(states)
    d = 0 if fwd else 1
    rows = pl.ds(r0, G)
    last = C - 1 if fwd else 0
    chunk = lambda a, u: a[u * C:(u + 1) * C]
    nt = (((1,), (1,)), ((), ()))

    zs = [(jnp.dot(dl_ref[b, rows, :], wdec_ref[d], preferred_element_type=F32) + bdec_ref[d:d + 1, :]) * LOG2E
          for b in range(nseq)]
    log_a = [(jnp.minimum(x, 0.0) - jnp.log2(1.0 + jnp.exp2(-jnp.abs(x)))) * (1.0 / GLA_GATE_TAU) for x in zs]

    bc = [jnp.dot(tri, jnp.concatenate(_split3(la), axis=1), preferred_element_type=F32) for la in log_a]
    b_cum = [(x[:, 0:KW] + x[:, KW:2 * KW]) + x[:, 2 * KW:3 * KW] for x in bc]
    b_last_rows = [[bcb[u * C + last:u * C + last + 1, :] for u in range(nchunks)] for bcb in b_cum]
    b_last = [jnp.concatenate([jnp.broadcast_to(r, (C, KW)) for r in rws], axis=0) for rws in b_last_rows]

    lane_head = lax.broadcasted_iota(jnp.int32, (1, KW), 1) // GLA_DK
    zero = jnp.zeros((), BF16)
    block_diag = lambda a: jnp.concatenate([jnp.where(lane_head == h, a, zero) for h in range(H)], axis=0)
    ci = lax.broadcasted_iota(jnp.int32, (C, C), 0)
    si = lax.broadcasted_iota(jnp.int32, (C, C), 1)
    causal = (si <= ci) if fwd else (si > ci)

    q_bd, k_t, kend_bd, v_heads = [], [], [], []
    for b in range(nseq):
        q = qk_ref[b, rows, 0:KW].astype(F32)
        k = qk_ref[b, rows, KW:2 * KW].astype(F32)
        v = vg_ref[b, rows, 0:GLA_VAL_WIDTH]
        q_t = ((q * (GLA_DK ** -0.5)) * jnp.exp2(b_cum[b])).astype(BF16)
        k_t.append((k * jnp.exp2(-b_cum[b])).astype(BF16))
        k_end = (k * jnp.exp2(b_last[b] - b_cum[b])).astype(BF16)
        q_bd.append([block_diag(chunk(q_t, u)) for u in range(nchunks)])
        kend_bd.append([block_diag(chunk(k_end, u)) for u in range(nchunks)])
        v_heads.append([[chunk(v, u)[:, h * GLA_DV:(h + 1) * GLA_DV] for h in range(H)] for u in range(nchunks)])

    units = [(b, u) for b in range(nseq) for u in range(nchunks)]
    attn = {bu: lax.dot_general(q_bd[bu[0]][bu[1]], chunk(k_t[bu[0]], bu[1]), nt, preferred_element_type=F32)
            for bu in units}
    attn = {bu: jnp.concatenate([jnp.where(causal, chunk(a, h), 0.0) for h in range(H)], axis=0).astype(BF16)
            for bu, a in attn.items()}
    st_add = {(b, u): lax.dot_general(jnp.concatenate(v_heads[b][u], axis=0), kend_bd[b][u],
                                      (((0,), (0,)), ((), ())), preferred_element_type=F32) for b, u in units}
    o_intra = {(b, u): jnp.concatenate([jnp.dot(chunk(attn[b, u], h), v_heads[b][u][h],
                                                preferred_element_type=F32) for h in range(H)], axis=0)
               for b, u in units}
    outs = [[None] * nchunks for _ in range(nseq)]
    states = list(states)
    for u in (range(nchunks) if fwd else reversed(range(nchunks))):
        for b in range(nseq):
            st = states[b]
            outs[b][u] = o_intra[b, u] + lax.dot_general(q_bd[b][u], st.astype(BF16), nt,
                                                         preferred_element_type=F32)
            states[b] = jnp.exp2(b_last_rows[b][u]) * st + st_add[b, u]
    return outs, states


def _gla_body(qk_ref, vg_ref, dl_ref, wdec_ref, bdec_ref, ng_ref, o_ref, st_ref, ob_ref, *, cbk, nbk):
    C, H = GLA_CHUNK, GLA_HEADS
    nseq = qk_ref.shape[0]
    ph = pl.program_id(1)
    j = pl.program_id(2)
    chunks_per_block = cbk // C
    nchunks = min(GLA_GROUP, chunks_per_block)
    ngroups = chunks_per_block // nchunks

    @pl.when(j == 0)
    def _():
        st_ref[...] = jnp.zeros_like(st_ref)

    def sweep(fwd):
        blk = j if fwd else nbk - 1 - j
        tri = _gla_cumsum_matrix(fwd, nchunks)

        def group(gi, carry):
            g0 = gi if fwd else ngroups - 1 - gi
            r0 = pl.multiple_of(g0 * nchunks * C, nchunks * C)
            outs, states = _gla_group(qk_ref, vg_ref, dl_ref, wdec_ref, bdec_ref, tri,
                                      [st_ref[b] for b in range(nseq)], r0, fwd, nchunks)
            for b in range(nseq):
                st_ref[b] = states[b]
                for u in range(nchunks):
                    cg = blk * chunks_per_block + g0 * nchunks + u
                    if not fwd:
                        ob_ref[b, cg] = outs[b][u]
                        continue
                    c0 = pl.multiple_of(r0 + u * C, C)
                    o = outs[b][u] + ob_ref[b, cg]
                    o = o * lax.rsqrt(jnp.mean(o * o, axis=-1, keepdims=True) + RMS_EPS) * ng_ref[...]
                    gate = vg_ref[b, pl.ds(c0, C), GLA_VAL_WIDTH:2 * GLA_VAL_WIDTH].astype(F32)
                    gate = jnp.concatenate([gate[:, h * GLA_DV:(h + 1) * GLA_DV] for h in range(H)], axis=0)
                    o = o * (gate * jax.nn.sigmoid(gate))
                    o_ref[b, pl.ds(c0, C), :] = jnp.concatenate(
                        [o[h * C:(h + 1) * C] for h in range(H)], axis=1).astype(BF16)
            return carry

        lax.fori_loop(0, ngroups, group, 0)

    @pl.when(ph == 0)
    def _():
        sweep(False)

    @pl.when(ph == 1)
    def _():
        sweep(True)


def _gla(gqk, gvg, gdl, wdec_pad, bdec, norm_g):
    b, s, _ = gqk.shape
    cbk = min(GLA_BLOCK, s)
    nbk = s // cbk
    nseq = GLA_SEQS if b % GLA_SEQS == 0 else 1
    blk = lambda w: pl.BlockSpec((nseq, cbk, w), lambda bi, ph, j: (bi, jnp.where(ph == 0, nbk - 1 - j, j), 0))
    return pl.pallas_call(
        functools.partial(_gla_body, cbk=cbk, nbk=nbk),
        grid=(b // nseq, 2, nbk),
        in_specs=[blk(2 * GLA_KEY_WIDTH), blk(2 * GLA_VAL_WIDTH), blk(2 * GLA_GATE_RANK),
                  _const_spec(wdec_pad.shape), _const_spec(bdec.shape), _const_spec(norm_g.shape)],
        out_specs=pl.BlockSpec((nseq, cbk, GLA_VAL_WIDTH), lambda bi, ph, j: (bi, jnp.where(ph == 0, 0, j), 0)),
        out_shape=jax.ShapeDtypeStruct((b, s, GLA_VAL_WIDTH), BF16),
        scratch_shapes=[pltpu.VMEM((nseq, GLA_DV, GLA_KEY_WIDTH), F32),
                        pltpu.VMEM((nseq, s // GLA_CHUNK, GLA_HEADS * GLA_CHUNK, GLA_DV), F32)],
        compiler_params=pltpu.CompilerParams(dimension_semantics=("arbitrary", "arbitrary", "arbitrary"),
                                             vmem_limit_bytes=VMEM_LIMIT_BYTES),
        name="gla_bidirectional",
    )(gqk, gvg, gdl, wdec_pad, bdec, norm_g)


def _merge_ffn_body(x1_ref, na_ref, gla_ref, wgate_ref, wbn_ref, wbg_ref, wo_ref, g2_ref, b2_ref,
                    wg_ref, wu_ref, wd_ref, g3_ref, b3_ref, o_ref):
    tm = x1_ref.shape[0]
    subs = [pl.ds(r, tm // SUB_TILES) for r in range(0, tm, tm // SUB_TILES)]
    dot = functools.partial(jnp.dot, preferred_element_type=F32)
    x1s = [x1_ref[rows, :] for rows in subs]
    pre = [(dot(x1.astype(BF16), wgate_ref[...]), dot(na_ref[rows, :], wbn_ref[...]),
            dot(gla_ref[rows, :], wbg_ref[...])) for rows, x1 in zip(subs, x1s)]
    z2s = []
    for x1, (gates, y_na, y_gla) in zip(x1s, pre):
        merged = jax.nn.sigmoid(gates[:, :D_MODEL]) * y_na + jax.nn.sigmoid(gates[:, D_MODEL:]) * y_gla
        z2s.append(ALPHA * x1 + dot(merged.astype(BF16), wo_ref[...]))
    x2s = [_layer_norm(z2, g2_ref[...], b2_ref[...]) for z2 in z2s]
    z3s = [ALPHA * x2 + 0.5 * _swiglu(x2.astype(BF16), wg_ref, wu_ref, wd_ref) for x2 in x2s]
    for rows, z3 in zip(subs, z3s):
        o_ref[rows, :] = _layer_norm(z3, g3_ref[...], b3_ref[...])


def _merge_ffn(x1, na_o, gla_o, wgate, wbn, wbg, wo, g2, b2, wg, wu, wd, g3, b3):
    t = x1.shape[0]
    tm = min(TOKEN_TILE, t)
    row = lambda w: pl.BlockSpec((tm, w), lambda i: (i, 0))
    consts = (wgate, wbn, wbg, wo, g2, b2, wg, wu, wd, g3, b3)
    return pl.pallas_call(
        _merge_ffn_body,
        grid=(t // tm,),
        in_specs=[row(D_MODEL), row(NA_WIDTH), row(GLA_VAL_WIDTH)] + [_const_spec(c.shape) for c in consts],
        out_specs=row(D_MODEL),
        out_shape=jax.ShapeDtypeStruct((t, D_MODEL), F32),
        compiler_params=pltpu.CompilerParams(dimension_semantics=("arbitrary",),
                                             vmem_limit_bytes=VMEM_LIMIT_BYTES),
        name="merge_ln2_ffn2_ln3",
    )(x1, na_o, gla_o, *consts)


def _layer(x, layer, early_stacked, late_stacked, ln1_g, ln1_b, na_rpb, gla_w_dec2, gla_b_dec, gla_norm_g,
           ln2_g, ln2_b, ln3_g, ln3_b):
    b, s, d = x.shape
    assert d == D_MODEL and s % (NA_UNROLL * NA_QT) == 0 and s // GRID_W >= NA_BAND
    rows = s // GRID_W
    bf = lambda w: w.astype(BF16)
    vec = lambda p: p.reshape(1, -1)

    w1_gate, w1_up, w1_down, w_in_bf16 = _to_bf16(early_stacked, layer)
    x1, nak, naqt, navt, gqk, gvg, gdl, w2_gate, w2_up, w2_down, wo, wbn, wbg, w_merge_gates = _ffn_inproj(
        x.reshape(b * s, d), w1_gate, w1_up, w1_down, vec(ln1_g), vec(ln1_b), w_in_bf16, late_stacked, layer)

    na_o = _neighborhood_attention(nak.reshape(b, s, -1), naqt, navt, _na_bias_tiles(na_rpb), rows)

    zr = jnp.zeros((GLA_GATE_RANK, GLA_KEY_WIDTH), gla_w_dec2.dtype)
    wdec_pad = bf(jnp.stack([jnp.concatenate([gla_w_dec2[0], zr]), jnp.concatenate([zr, gla_w_dec2[1]])]))
    gla_o = _gla(gqk.reshape(b, s, -1), gvg.reshape(b, s, -1), gdl.reshape(b, s, -1),
                 wdec_pad, gla_b_dec, vec(gla_norm_g))

    out = _merge_ffn(x1, na_o.reshape(b * s, -1), gla_o.reshape(b * s, -1), w_merge_gates, wbn, wbg, wo,
                     vec(ln2_g), vec(ln2_b), w2_gate, w2_up, w2_down, vec(ln3_g), vec(ln3_b))
    return out.reshape(b, s, d)


def kernel(x, ffn1_w_gate, ffn1_w_up, ffn1_w_down, ln1_g, ln1_b, w_in, na_rpb, gla_w_dec2, gla_b_dec, gla_norm_g,
           w_branch_na, w_branch_gla, w_out, ln2_g, ln2_b, ffn2_w_gate, ffn2_w_up, ffn2_w_down, ln3_g, ln3_b):
    params = (ln1_g, ln1_b, na_rpb, gla_w_dec2, gla_b_dec, gla_norm_g, ln2_g, ln2_b, ln3_g, ln3_b)
    early_stacked = (ffn1_w_gate, ffn1_w_up, ffn1_w_down, w_in)
    late_stacked = (ffn2_w_gate, ffn2_w_up, ffn2_w_down, w_out, w_branch_na, w_branch_gla, w_in)
    for l in range(DEPTH):
        x = _layer(x, l, early_stacked, late_stacked, *(p[l] for p in params))
    return x
```

```python
import functools

import jax
import jax.numpy as jnp
import numpy as np
from jax import lax
from jax.experimental import pallas as pl
from jax.experimental.pallas import tpu as pltpu

F32 = jnp.float32
BF16 = jnp.bfloat16

D_MODEL = 1024
D_FF = 2816
GRID_W = 64
NA_HEADS = 8
NA_HEAD_DIM = 64
NA_WIDTH = NA_HEADS * NA_HEAD_DIM
NA_WIN_H = 8
NA_WIN_W = 16
GLA_HEADS = 4
GLA_DK = 64
GLA_DV = 128
GLA_KEY_WIDTH = GLA_HEADS * GLA_DK
GLA_VAL_WIDTH = GLA_HEADS * GLA_DV
GLA_GATE_RANK = 16
GLA_GATE_TAU = 16.0
GLA_CHUNK = 64
DEPTH = 1
ALPHA = (2 * DEPTH) ** 0.25
LN_EPS = 1e-5
RMS_EPS = 1e-6

OFF_NA_END = 3 * NA_WIDTH
OFF_GQK_END = OFF_NA_END + 2 * GLA_KEY_WIDTH
OFF_GVG_END = OFF_GQK_END + 2 * GLA_VAL_WIDTH
OFF_GDL_END = OFF_GVG_END + 2 * GLA_GATE_RANK
N_IN = OFF_GDL_END + 2 * D_MODEL

LANES = 128
BF16_SUBLANES = 16
N_LATE_WEIGHTS = 7
VMEM_LIMIT_BYTES = 56 * 1024 * 1024

NA_RQ = 4
NA_BAND = NA_RQ + NA_WIN_H
NA_QT = NA_RQ * GRID_W
NA_KT = NA_BAND * GRID_W
NA_UNROLL = 16
NA_ONES_ROWS = 16
NEG_BIG = -1e30
LOG2E = 1.4426950408889634
NA_Q_SCALE = LOG2E * NA_HEAD_DIM ** -0.5

GLA_GROUP = 4
GLA_BLOCK = 1024
GLA_SEQS = 2
TOKEN_TILE = 512
SUB_TILES = 2
SUB_TILE_ROWS = TOKEN_TILE // SUB_TILES
MERGE_TILE = 1024
MERGE_VMEM_LIMIT_BYTES = 60 * 1024 * 1024


def _const_spec(shape):
    nd = len(shape)
    return pl.BlockSpec(shape, lambda *_: (0,) * nd, pipeline_mode=pl.Buffered(1))


def _layer_norm(z, g, b):
    mu = jnp.mean(z, axis=-1, keepdims=True)
    zc = z - mu
    var = jnp.mean(zc * zc, axis=-1, keepdims=True)
    return zc * lax.rsqrt(var + LN_EPS) * g + b


def _swiglu(xb, wg_ref, wu_ref, wd_ref):
    g = jnp.dot(xb, wg_ref[...], preferred_element_type=F32)
    u = jnp.dot(xb, wu_ref[...], preferred_element_type=F32)
    h = (g * jax.nn.sigmoid(g)) * u
    return jnp.dot(h.astype(BF16), wd_ref[...], preferred_element_type=F32)


def _ffn_inproj_body(x_ref, wg_ref, wu_ref, wd_ref, g1_ref, b1_ref, win_ref, *refs):
    late_f32, outs = refs[:N_LATE_WEIGHTS], refs[N_LATE_WEIGHTS:]
    x1_ref, nak_ref, naqt_ref, navt_ref, gqk_ref, gvg_ref, gdl_ref = outs[:-N_LATE_WEIGHTS]
    late_bf16 = outs[-N_LATE_WEIGHTS:]
    for src, dst in zip(late_f32[:-1], late_bf16[:-1]):
        dst[...] = src[...].astype(BF16)
    late_bf16[-1][...] = late_f32[-1][:, OFF_GDL_END:].astype(BF16)

    tm = x_ref.shape[0]
    subs = [pl.ds(r, tm // SUB_TILES) for r in range(0, tm, tm // SUB_TILES)]
    zs = []
    for rows in subs:
        x = x_ref[rows, :]
        zs.append(ALPHA * x + 0.5 * _swiglu(x.astype(BF16), wg_ref, wu_ref, wd_ref))
    for g, (rows, z) in enumerate(zip(subs, zs)):
        x1 = _layer_norm(z, g1_ref[...], b1_ref[...])
        x1_ref[rows, :] = x1
        proj = jnp.dot(x1.astype(BF16), win_ref[:, :OFF_GDL_END], preferred_element_type=F32)
        nak_ref[rows, :] = proj[:, NA_WIDTH:2 * NA_WIDTH].astype(BF16)
        naqt_ref[g] = (proj[:, :NA_WIDTH] * NA_Q_SCALE).T.astype(BF16)
        navt_ref[g] = proj[:, 2 * NA_WIDTH:OFF_NA_END].T.astype(BF16)
        gqk_ref[rows, :] = proj[:, OFF_NA_END:OFF_GQK_END].astype(BF16)
        gvg_ref[rows, :] = proj[:, OFF_GQK_END:OFF_GVG_END].astype(BF16)
        gdl_ref[rows, :] = proj[:, OFF_GVG_END:OFF_GDL_END].astype(BF16)


def _row_chunk(nrows, nsteps):
    return next(c for c in range(BF16_SUBLANES, nrows + 1, BF16_SUBLANES) if nrows % c == 0 and c * nsteps >= nrows)


def _ffn_inproj(x2d, wg, wu, wd, g1, b1, win, late_weights, layer):
    t = x2d.shape[0]
    tm = min(TOKEN_TILE, t)
    n = t // tm
    assert tm // SUB_TILES == NA_QT and len(late_weights) == N_LATE_WEIGHTS
    row = lambda w: pl.BlockSpec((tm, w), lambda i: (i, 0))
    widths = (NA_WIDTH, OFF_GQK_END - OFF_NA_END, OFF_GVG_END - OFF_GQK_END, OFF_GDL_END - OFF_GVG_END)
    rows_bf16 = [jax.ShapeDtypeStruct((t, w), BF16) for w in widths]
    late_shapes = [w.shape[1:] for w in late_weights[:-1]] + [(D_MODEL, N_IN - OFF_GDL_END)]

    def chunk_specs(w_rows, cols_in, cols_out):
        c = _row_chunk(w_rows, n)
        last = w_rows // c - 1
        return (pl.BlockSpec((None, c, cols_in), lambda i: (layer, jnp.minimum(i, last), 0)),
                pl.BlockSpec((c, cols_out), lambda i: (jnp.minimum(i, last), 0)))

    late_specs = [chunk_specs(w.shape[1], w.shape[2], so[1]) for w, so in zip(late_weights, late_shapes)]
    return pl.pallas_call(
        _ffn_inproj_body,
        grid=(n,),
        in_specs=[row(D_MODEL), _const_spec(wg.shape), _const_spec(wu.shape), _const_spec(wd.shape),
                  _const_spec(g1.shape), _const_spec(b1.shape), _const_spec(win.shape)]
                 + [si for si, _ in late_specs],
        out_specs=[row(D_MODEL), row(widths[0])] + [pl.BlockSpec((SUB_TILES, NA_WIDTH, NA_QT), lambda i: (i, 0, 0))] * 2
                  + [row(w) for w in widths[1:]] + [so for _, so in late_specs],
        out_shape=[jax.ShapeDtypeStruct((t, D_MODEL), F32), rows_bf16[0]]
                  + [jax.ShapeDtypeStruct((t // NA_QT, NA_WIDTH, NA_QT), BF16)] * 2 + rows_bf16[1:]
                  + [jax.ShapeDtypeStruct(so, BF16) for so in late_shapes],
        compiler_params=pltpu.CompilerParams(dimension_semantics=("arbitrary",),
                                             vmem_limit_bytes=VMEM_LIMIT_BYTES),
        name="ffn1_ln1_inproj",
    )(x2d, wg, wu, wd, g1, b1, win, *late_weights)


NA_ROW_OFFSETS = 2 * NA_WIN_H - 1


def _na_bias_tiles(rpb):
    kw = NA_WIN_W
    c = np.arange(GRID_W)[None, :]
    kc = np.arange(GRID_W)[:, None]
    c_start = np.clip(c - kw // 2, 0, GRID_W - kw)
    col_ok = (kc >= c_start) & (kc < c_start + kw)
    dc = kc - c + (kw - 1)
    sel_c = ((dc[..., None] == np.arange(2 * kw - 1)) & col_ok[..., None]).astype(np.float32)
    tiles = jnp.einsum('hdw,kcw->hdkc', rpb, sel_c, precision=lax.Precision.HIGHEST)
    tiles = jnp.where(col_ok, tiles * LOG2E, NEG_BIG)
    tiles = jnp.concatenate([tiles, jnp.full_like(tiles[:, :1], NEG_BIG)], axis=1)
    return jnp.concatenate([tiles, tiles], axis=-1).astype(F32)


def _na_row_offsets(rows):
    kh = NA_WIN_H
    ri = np.arange(NA_RQ)[:, None]
    bj = np.arange(NA_BAND)[None, :]
    out = []
    for r0, b0 in ((0, 0), (NA_RQ, 0), (rows - NA_RQ, rows - NA_BAND)):
        r = r0 + ri
        kr = b0 + bj
        r_start = np.clip(r - kh // 2, 0, rows - kh)
        row_ok = (kr >= r_start) & (kr < r_start + kh)
        out.append(np.where(row_ok, kr - r + (kh - 1), NA_ROW_OFFSETS).tolist())
    return out


def _na_body(qt_ref, k_ref, vt_ref, tile_ref, o_ref, s_ref, bias_ref, *, rows):
    nblk = rows // NA_RQ
    lane = lax.broadcasted_iota(jnp.int32, (1, LANES), 1)
    first_head = lane < NA_HEAD_DIM
    first_head_row = lax.broadcasted_iota(jnp.int32, (LANES, 1), 0) < NA_HEAD_DIM

    @pl.when(pl.program_id(1) == 0)
    def _():
        for kind, offsets in enumerate(_na_row_offsets(rows)):
            for hh in range(2):
                for bj in range(NA_BAND):
                    for ri in range(0, NA_RQ, 2):
                        tile = jnp.where(first_head, tile_ref[hh, offsets[ri][bj]], tile_ref[hh, offsets[ri + 1][bj]])
                        bias_ref[kind, pl.ds(bj * GRID_W, GRID_W), pl.ds(hh * NA_QT + ri * GRID_W, LANES)] = tile

    def band_row(i):
        return jnp.clip(NA_RQ * i - NA_WIN_H // 2, 0, rows - NA_BAND)

    def scores(i, slot):
        q0 = pl.multiple_of(i * NA_QT, NA_QT)
        k0 = pl.multiple_of(band_row(i) * GRID_W, NA_QT)
        kind = jnp.where(i == 0, 0, jnp.where(i == nblk - 1, 2, 1))
        qt = qt_ref[i]
        k = k_ref[0, pl.ds(k0, NA_KT), :]
        zero = jnp.zeros((), BF16)
        qt2 = jnp.concatenate([jnp.where(first_head_row, qt, zero), jnp.where(first_head_row, zero, qt)], axis=1)
        s = jnp.dot(k, qt2, preferred_element_type=F32)
        s_ref[slot] = s + bias_ref[kind]

    def finish(i, slot):
        q0 = pl.multiple_of(i * NA_QT, NA_QT)
        g0 = band_row(i) // NA_RQ
        s = s_ref[slot]
        p = jnp.exp2(s - jnp.max(s, axis=0, keepdims=True)).astype(BF16)
        ones = jnp.ones((NA_ONES_ROWS, NA_QT), BF16)
        heads = []
        for hh in range(2):
            acc = None
            for g in range(NA_BAND // NA_RQ):
                vt = jnp.concatenate([vt_ref[g0 + g, hh * NA_HEAD_DIM:(hh + 1) * NA_HEAD_DIM, :], ones], axis=0)
                part = jnp.dot(vt, p[g * NA_QT:(g + 1) * NA_QT, hh * NA_QT:(hh + 1) * NA_QT],
                               preferred_element_type=F32)
                acc = part if acc is None else acc + part
            heads.append(acc[:NA_HEAD_DIM] * (1.0 / acc[NA_HEAD_DIM:NA_HEAD_DIM + 1]))
        o_ref[0, pl.ds(q0, NA_QT), :] = jnp.concatenate(heads, axis=0).T.astype(BF16)

    scores(0, 0)

    def group(j, carry):
        for u in range(NA_UNROLL):
            i = NA_UNROLL * j + u
            scores(jnp.minimum(i + 1, nblk - 1), (u + 1) % 2)
            finish(i, u % 2)
        return carry

    lax.fori_loop(0, nblk // NA_UNROLL, group, 0)


def _neighborhood_attention(nak, naqt, navt, tiles, rows):
    b, s, _ = nak.shape
    npairs = NA_HEADS // 2
    tok = pl.BlockSpec((1, s, LANES), lambda p, bi: (bi, 0, p))
    slabs = pl.BlockSpec((s // NA_QT, LANES, NA_QT), lambda p, bi: (bi, p, 0))
    return pl.pallas_call(
        functools.partial(_na_body, rows=rows),
        grid=(npairs, b),
        in_specs=[slabs, tok, slabs,
                  pl.BlockSpec((2, NA_ROW_OFFSETS + 1, GRID_W, LANES), lambda p, bi: (p, 0, 0, 0))],
        out_specs=tok,
        out_shape=jax.ShapeDtypeStruct((b, s, NA_WIDTH), BF16),
        scratch_shapes=[pltpu.VMEM((2, NA_KT, 2 * NA_QT), F32), pltpu.VMEM((3, NA_KT, 2 * NA_QT), F32)],
        compiler_params=pltpu.CompilerParams(dimension_semantics=("arbitrary", "arbitrary"),
                                             vmem_limit_bytes=VMEM_LIMIT_BYTES),
        name="neighborhood_attention",
    )(naqt, nak, navt, tiles)


def _split3(x):
    top16 = lambda a: lax.bitcast_convert_type(
        lax.bitcast_convert_type(a, jnp.int32) & jnp.int32(-65536), F32)
    hi = top16(x)
    r1 = x - hi
    mid = top16(r1)
    lo = r1 - mid
    return hi.astype(BF16), mid.astype(BF16), lo.astype(BF16)


def _gla_cumsum_matrix(fwd, nchunks):
    G = nchunks * GLA_CHUNK
    ci = lax.broadcasted_iota(jnp.int32, (G, G), 0)
    si = lax.broadcasted_iota(jnp.int32, (G, G), 1)
    in_order = (si <= ci) if fwd else (si >= ci)
    return jnp.where(in_order & ((ci // GLA_CHUNK) == (si // GLA_CHUNK)), 1.0, 0.0).astype(BF16)


def _gla_group(qk_ref, vg_ref, dl_ref, wdec_ref, bdec_ref, tri, states, r0, fwd, nchunks):
    C, KW, H = GLA_CHUNK, GLA_KEY_WIDTH, GLA_HEADS
    G = nchunks * C
    nseq = len(states)
    d = 0 if fwd else 1
    rows = pl.ds(r0, G)
    last = C - 1 if fwd else 0
    chunk = lambda a, u: a[u * C:(u + 1) * C]
    nt = (((1,), (1,)), ((), ()))

    zs = [(jnp.dot(dl_ref[b, rows, :], wdec_ref[d], preferred_element_type=F32) + bdec_ref[d:d + 1, :]) * LOG2E
          for b in range(nseq)]
    log_a = [(jnp.minimum(x, 0.0) - jnp.log2(1.0 + jnp.exp2(-jnp.abs(x)))) * (1.0 / GLA_GATE_TAU) for x in zs]

    bc = [jnp.dot(tri, jnp.concatenate(_split3(la), axis=1), preferred_element_type=F32) for la in log_a]
    b_cum = [(x[:, 0:KW] + x[:, KW:2 * KW]) + x[:, 2 * KW:3 * KW] for x in bc]
    b_last_rows = [[bcb[u * C + last:u * C + last + 1, :] for u in range(nchunks)] for bcb in b_cum]
    b_last = [jnp.concatenate([jnp.broadcast_to(r, (C, KW)) for r in rws], axis=0) for rws in b_last_rows]

    lane_head = lax.broadcasted_iota(jnp.int32, (1, KW), 1) // GLA_DK
    zero = jnp.zeros((), BF16)
    block_diag = lambda a: jnp.concatenate([jnp.where(lane_head == h, a, zero) for h in range(H)], axis=0)
    ci = lax.broadcasted_iota(jnp.int32, (C, C), 0)
    si = lax.broadcasted_iota(jnp.int32, (C, C), 1)
    causal = (si <= ci) if fwd else (si > ci)

    q_bd, k_t, kend_bd, v_heads = [], [], [], []
    for b in range(nseq):
        q = qk_ref[b, rows, 0:KW].astype(F32)
        k = qk_ref[b, rows, KW:2 * KW].astype(F32)
        v = vg_ref[b, rows, 0:GLA_VAL_WIDTH]
        q_t = ((q * (GLA_DK ** -0.5)) * jnp.exp2(b_cum[b])).astype(BF16)
        k_t.append((k * jnp.exp2(-b_cum[b])).astype(BF16))
        k_end = (k * jnp.exp2(b_last[b] - b_cum[b])).astype(BF16)
        q_bd.append([block_diag(chunk(q_t, u)) for u in range(nchunks)])
        kend_bd.append([block_diag(chunk(k_end, u)) for u in range(nchunks)])
        v_heads.append([[chunk(v, u)[:, h * GLA_DV:(h + 1) * GLA_DV] for h in range(H)] for u in range(nchunks)])

    units = [(b, u) for b in range(nseq) for u in range(nchunks)]
    attn = {bu: lax.dot_general(q_bd[bu[0]][bu[1]], chunk(k_t[bu[0]], bu[1]), nt, preferred_element_type=F32)
            for bu in units}
    attn = {bu: jnp.concatenate([jnp.where(causal, chunk(a, h), 0.0) for h in range(H)], axis=0).astype(BF16)
            for bu, a in attn.items()}
    st_add = {(b, u): lax.dot_general(jnp.concatenate(v_heads[b][u], axis=0), kend_bd[b][u],
                                      (((0,), (0,)), ((), ())), preferred_element_type=F32) for b, u in units}
    o_intra = {(b, u): jnp.concatenate([jnp.dot(chunk(attn[b, u], h), v_heads[b][u][h],
                                                preferred_element_type=F32) for h in range(H)], axis=0)
               for b, u in units}
    outs = [[None] * nchunks for _ in range(nseq)]
    states = list(states)
    for u in (range(nchunks) if fwd else reversed(range(nchunks))):
        for b in range(nseq):
            st = states[b]
            outs[b][u] = o_intra[b, u] + lax.dot_general(q_bd[b][u], st.astype(BF16), nt,
                                                         preferred_element_type=F32)
            states[b] = jnp.exp2(b_last_rows[b][u]) * st + st_add[b, u]
    return outs, states


def _gla_body(qk_ref, vg_ref, dl_ref, wdec_ref, bdec_ref, ng_ref, o_ref, st_ref, ob_ref, *, cbk, nbk):
    C, H = GLA_CHUNK, GLA_HEADS
    nseq = qk_ref.shape[0]
    ph = pl.program_id(1)
    j = pl.program_id(2)
    chunks_per_block = cbk // C
    nchunks = min(GLA_GROUP, chunks_per_block)
    ngroups = chunks_per_block // nchunks

    @pl.when(j == 0)
    def _():
        st_ref[...] = jnp.zeros_like(st_ref)

    def sweep(fwd):
        blk = j if fwd else nbk - 1 - j
        tri = _gla_cumsum_matrix(fwd, nchunks)

        def group(gi, carry):
            g0 = gi if fwd else ngroups - 1 - gi
            r0 = pl.multiple_of(g0 * nchunks * C, nchunks * C)
            outs, states = _gla_group(qk_ref, vg_ref, dl_ref, wdec_ref, bdec_ref, tri,
                                      [st_ref[b] for b in range(nseq)], r0, fwd, nchunks)
            for b in range(nseq):
                st_ref[b] = states[b]
                for u in range(nchunks):
                    cg = blk * chunks_per_block + g0 * nchunks + u
                    if not fwd:
                        ob_ref[b, cg] = outs[b][u]
                        continue
                    c0 = pl.multiple_of(r0 + u * C, C)
                    o = outs[b][u] + ob_ref[b, cg]
                    o = o * lax.rsqrt(jnp.mean(o * o, axis=-1, keepdims=True) + RMS_EPS) * ng_ref[...]
                    gate = vg_ref[b, pl.ds(c0, C), GLA_VAL_WIDTH:2 * GLA_VAL_WIDTH].astype(F32)
                    gate = jnp.concatenate([gate[:, h * GLA_DV:(h + 1) * GLA_DV] for h in range(H)], axis=0)
                    o = o * (gate * jax.nn.sigmoid(gate))
                    o_ref[b, pl.ds(c0, C), :] = jnp.concatenate(
                        [o[h * C:(h + 1) * C] for h in range(H)], axis=1).astype(BF16)
            return carry

        lax.fori_loop(0, ngroups, group, 0)

    @pl.when(ph == 0)
    def _():
        sweep(False)

    @pl.when(ph == 1)
    def _():
        sweep(True)


def _gla(gqk, gvg, gdl, wdec_pad, bdec, norm_g):
    b, s, _ = gqk.shape
    cbk = min(GLA_BLOCK, s)
    nbk = s // cbk
    nseq = GLA_SEQS if b % GLA_SEQS == 0 else 1
    blk = lambda w: pl.BlockSpec((nseq, cbk, w), lambda bi, ph, j: (bi, jnp.where(ph == 0, nbk - 1 - j, j), 0))
    return pl.pallas_call(
        functools.partial(_gla_body, cbk=cbk, nbk=nbk),
        grid=(b // nseq, 2, nbk),
        in_specs=[blk(2 * GLA_KEY_WIDTH), blk(2 * GLA_VAL_WIDTH), blk(2 * GLA_GATE_RANK),
                  _const_spec(wdec_pad.shape), _const_spec(bdec.shape), _const_spec(norm_g.shape)],
        out_specs=pl.BlockSpec((nseq, cbk, GLA_VAL_WIDTH), lambda bi, ph, j: (bi, jnp.where(ph == 0, 0, j), 0)),
        out_shape=jax.ShapeDtypeStruct((b, s, GLA_VAL_WIDTH), BF16),
        scratch_shapes=[pltpu.VMEM((nseq, GLA_DV, GLA_KEY_WIDTH), F32),
                        pltpu.VMEM((nseq, s // GLA_CHUNK, GLA_HEADS * GLA_CHUNK, GLA_DV), F32)],
        compiler_params=pltpu.CompilerParams(dimension_semantics=("arbitrary", "arbitrary", "arbitrary"),
                                             vmem_limit_bytes=VMEM_LIMIT_BYTES),
        name="gla_bidirectional",
    )(gqk, gvg, gdl, wdec_pad, bdec, norm_g)


def _merge_ffn_body(x1_ref, na_ref, gla_ref, wgate_ref, wbn_ref, wbg_ref, wo_ref, g2_ref, b2_ref,
                    wg_ref, wu_ref, wd_ref, g3_ref, b3_ref, o_ref):
    tm = x1_ref.shape[0]
    subs = [pl.ds(r, SUB_TILE_ROWS) for r in range(0, tm, SUB_TILE_ROWS)]
    dot = functools.partial(jnp.dot, preferred_element_type=F32)
    x1s = [x1_ref[rows, :] for rows in subs]
    pre = [(dot(x1.astype(BF16), wgate_ref[...]), dot(na_ref[rows, :], wbn_ref[...]),
            dot(gla_ref[rows, :], wbg_ref[...])) for rows, x1 in zip(subs, x1s)]
    z2s = []
    for x1, (gates, y_na, y_gla) in zip(x1s, pre):
        merged = jax.nn.sigmoid(gates[:, :D_MODEL]) * y_na + jax.nn.sigmoid(gates[:, D_MODEL:]) * y_gla
        z2s.append(ALPHA * x1 + dot(merged.astype(BF16), wo_ref[...]))
    x2s = [_layer_norm(z2, g2_ref[...], b2_ref[...]) for z2 in z2s]
    z3s = [ALPHA * x2 + 0.5 * _swiglu(x2.astype(BF16), wg_ref, wu_ref, wd_ref) for x2 in x2s]
    for rows, z3 in zip(subs, z3s):
        o_ref[rows, :] = _layer_norm(z3, g3_ref[...], b3_ref[...])


def _merge_ffn(x1, na_o, gla_o, wgate, wbn, wbg, wo, g2, b2, wg, wu, wd, g3, b3):
    t = x1.shape[0]
    tm = min(MERGE_TILE, t)
    row = lambda w: pl.BlockSpec((tm, w), lambda i: (i, 0))
    consts = (wgate, wbn, wbg, wo, g2, b2, wg, wu, wd, g3, b3)
    return pl.pallas_call(
        _merge_ffn_body,
        grid=(t // tm,),
        in_specs=[row(D_MODEL), row(NA_WIDTH), row(GLA_VAL_WIDTH)] + [_const_spec(c.shape) for c in consts],
        out_specs=row(D_MODEL),
        out_shape=jax.ShapeDtypeStruct((t, D_MODEL), F32),
        compiler_params=pltpu.CompilerParams(dimension_semantics=("arbitrary",),
                                             vmem_limit_bytes=MERGE_VMEM_LIMIT_BYTES),
        name="merge_ln2_ffn2_ln3",
    )(x1, na_o, gla_o, *consts)


def _layer(x, layer, late_stacked, ffn1_w_gate, ffn1_w_up, ffn1_w_down, ln1_g, ln1_b, w_in, na_rpb, gla_w_dec2,
           gla_b_dec, gla_norm_g, ln2_g, ln2_b, ln3_g, ln3_b):
    b, s, d = x.shape
    assert d == D_MODEL and s % (NA_UNROLL * NA_QT) == 0 and s // GRID_W >= NA_BAND
    rows = s // GRID_W
    bf = lambda w: w.astype(BF16)
    vec = lambda p: p.reshape(1, -1)

    x1, nak, naqt, navt, gqk, gvg, gdl, w2_gate, w2_up, w2_down, wo, wbn, wbg, w_merge_gates = _ffn_inproj(
        x.reshape(b * s, d), bf(ffn1_w_gate), bf(ffn1_w_up), bf(ffn1_w_down), vec(ln1_g), vec(ln1_b), bf(w_in),
        late_stacked, layer)

    na_o = _neighborhood_attention(nak.reshape(b, s, -1), naqt, navt, _na_bias_tiles(na_rpb), rows)

    zr = jnp.zeros((GLA_GATE_RANK, GLA_KEY_WIDTH), gla_w_dec2.dtype)
    wdec_pad = bf(jnp.stack([jnp.concatenate([gla_w_dec2[0], zr]), jnp.concatenate([zr, gla_w_dec2[1]])]))
    gla_o = _gla(gqk.reshape(b, s, -1), gvg.reshape(b, s, -1), gdl.reshape(b, s, -1),
                 wdec_pad, gla_b_dec, vec(gla_norm_g))

    out = _merge_ffn(x1, na_o.reshape(b * s, -1), gla_o.reshape(b * s, -1), w_merge_gates, wbn, wbg, wo,
                     vec(ln2_g), vec(ln2_b), w2_gate, w2_up, w2_down, vec(ln3_g), vec(ln3_b))
    return out.reshape(b, s, d)


def kernel(x, ffn1_w_gate, ffn1_w_up, ffn1_w_down, ln1_g, ln1_b, w_in, na_rpb, gla_w_dec2, gla_b_dec, gla_norm_g,
           w_branch_na, w_branch_gla, w_out, ln2_g, ln2_b, ffn2_w_gate, ffn2_w_up, ffn2_w_down, ln3_g, ln3_b):
    params = (ffn1_w_gate, ffn1_w_up, ffn1_w_down, ln1_g, ln1_b, w_in, na_rpb, gla_w_dec2, gla_b_dec, gla_norm_g,
              ln2_g, ln2_b, ln3_g, ln3_b)
    late_stacked = (ffn2_w_gate, ffn2_w_up, ffn2_w_down, w_out, w_branch_na, w_branch_gla, w_in)
    for l in range(DEPTH):
        x = _layer(x, l, late_stacked, *(p[l] for p in params))
    return x
```

```python
import functools

import jax
import jax.numpy as jnp
import numpy as np
from jax import lax
from jax.experimental import pallas as pl
from jax.experimental.pallas import tpu as pltpu

F32 = jnp.float32
BF16 = jnp.bfloat16

D_MODEL = 1024
D_FF = 2816
GRID_W = 64
NA_HEADS = 8
NA_HEAD_DIM = 64
NA_WIDTH = NA_HEADS * NA_HEAD_DIM
NA_WIN_H = 8
NA_WIN_W = 16
GLA_HEADS = 4
GLA_DK = 64
GLA_DV = 128
GLA_KEY_WIDTH = GLA_HEADS * GLA_DK
GLA_VAL_WIDTH = GLA_HEADS * GLA_DV
GLA_GATE_RANK = 16
GLA_GATE_TAU = 16.0
GLA_CHUNK = 64
DEPTH = 1
ALPHA = (2 * DEPTH) ** 0.25
LN_EPS = 1e-5
RMS_EPS = 1e-6

OFF_NA_END = 3 * NA_WIDTH
OFF_GQK_END = OFF_NA_END + 2 * GLA_KEY_WIDTH
OFF_GVG_END = OFF_GQK_END + 2 * GLA_VAL_WIDTH
OFF_GDL_END = OFF_GVG_END + 2 * GLA_GATE_RANK
N_IN = OFF_GDL_END + 2 * D_MODEL

LANES = 128
BF16_SUBLANES = 16
N_LATE_WEIGHTS = 7
VMEM_LIMIT_BYTES = 56 * 1024 * 1024

NA_RQ = 4
NA_BAND = NA_RQ + NA_WIN_H
NA_QT = NA_RQ * GRID_W
NA_KT = NA_BAND * GRID_W
NA_UNROLL = 16
NA_ONES_ROWS = 16
NEG_BIG = -1e30
LOG2E = 1.4426950408889634
NA_Q_SCALE = LOG2E * NA_HEAD_DIM ** -0.5

GLA_GROUP = 4
GLA_BLOCK = 1024
GLA_SEQS = 2
TOKEN_TILE = 512
SUB_TILES = 2
SUB_TILE_ROWS = TOKEN_TILE // SUB_TILES
MERGE_TILE = 1024
MERGE_VMEM_LIMIT_BYTES = 60 * 1024 * 1024


def _const_spec(shape):
    nd = len(shape)
    return pl.BlockSpec(shape, lambda *_: (0,) * nd, pipeline_mode=pl.Buffered(1))


def _layer_norm(z, g, b):
    mu = jnp.mean(z, axis=-1, keepdims=True)
    zc = z - mu
    var = jnp.mean(zc * zc, axis=-1, keepdims=True)
    return zc * lax.rsqrt(var + LN_EPS) * g + b


def _swiglu(xb, wg_ref, wu_ref, wd_ref):
    g = jnp.dot(xb, wg_ref[...], preferred_element_type=F32)
    u = jnp.dot(xb, wu_ref[...], preferred_element_type=F32)
    h = (g * jax.nn.sigmoid(g)) * u
    return jnp.dot(h.astype(BF16), wd_ref[...], preferred_element_type=F32)


def _ffn_inproj_body(x_ref, wg_ref, wu_ref, wd_ref, g1_ref, b1_ref, win_ref, *refs):
    late_f32, outs = refs[:N_LATE_WEIGHTS], refs[N_LATE_WEIGHTS:]
    x1_ref, nak_ref, naqt_ref, navt_ref, gqk_ref, gvg_ref, gdl_ref = outs[:-N_LATE_WEIGHTS]
    late_bf16 = outs[-N_LATE_WEIGHTS:]
    for src, dst in zip(late_f32[:-1], late_bf16[:-1]):
        dst[...] = src[...].astype(BF16)
    late_bf16[-1][...] = late_f32[-1][:, OFF_GDL_END:].astype(BF16)

    tm = x_ref.shape[0]
    subs = [pl.ds(r, tm // SUB_TILES) for r in range(0, tm, tm // SUB_TILES)]
    zs = []
    for rows in subs:
        x = x_ref[rows, :]
        zs.append(ALPHA * x + 0.5 * _swiglu(x.astype(BF16), wg_ref, wu_ref, wd_ref))
    for g, (rows, z) in enumerate(zip(subs, zs)):
        x1 = _layer_norm(z, g1_ref[...], b1_ref[...])
        x1_ref[rows, :] = x1
        proj = jnp.dot(x1.astype(BF16), win_ref[:, :OFF_GDL_END], preferred_element_type=F32)
        nak_ref[rows, :] = proj[:, NA_WIDTH:2 * NA_WIDTH].astype(BF16)
        naqt_ref[g] = (proj[:, :NA_WIDTH] * NA_Q_SCALE).T.astype(BF16)
        navt_ref[g] = proj[:, 2 * NA_WIDTH:OFF_NA_END].T.astype(BF16)
        gqk_ref[rows, :] = proj[:, OFF_NA_END:OFF_GQK_END].astype(BF16)
        gvg_ref[rows, :] = proj[:, OFF_GQK_END:OFF_GVG_END].astype(BF16)
        gdl_ref[rows, :] = proj[:, OFF_GVG_END:OFF_GDL_END].astype(BF16)


def _row_chunk(nrows, nsteps):
    return next(c for c in range(BF16_SUBLANES, nrows + 1, BF16_SUBLANES) if nrows % c == 0 and c * nsteps >= nrows)


def _ffn_inproj(x2d, wg, wu, wd, g1, b1, win, late_weights, layer):
    t = x2d.shape[0]
    tm = min(TOKEN_TILE, t)
    n = t // tm
    assert tm // SUB_TILES == NA_QT and len(late_weights) == N_LATE_WEIGHTS
    row = lambda w: pl.BlockSpec((tm, w), lambda i: (i, 0))
    widths = (NA_WIDTH, OFF_GQK_END - OFF_NA_END, OFF_GVG_END - OFF_GQK_END, OFF_GDL_END - OFF_GVG_END)
    rows_bf16 = [jax.ShapeDtypeStruct((t, w), BF16) for w in widths]
    late_shapes = [w.shape[1:] for w in late_weights[:-1]] + [(D_MODEL, N_IN - OFF_GDL_END)]

    def chunk_specs(w_rows, cols_in, cols_out):
        c = _row_chunk(w_rows, n)
        last = w_rows // c - 1
        return (pl.BlockSpec((None, c, cols_in), lambda i: (layer, jnp.minimum(i, last), 0)),
                pl.BlockSpec((c, cols_out), lambda i: (jnp.minimum(i, last), 0)))

    late_specs = [chunk_specs(w.shape[1], w.shape[2], so[1]) for w, so in zip(late_weights, late_shapes)]
    return pl.pallas_call(
        _ffn_inproj_body,
        grid=(n,),
        in_specs=[row(D_MODEL), _const_spec(wg.shape), _const_spec(wu.shape), _const_spec(wd.shape),
                  _const_spec(g1.shape), _const_spec(b1.shape), _const_spec(win.shape)]
                 + [si for si, _ in late_specs],
        out_specs=[row(D_MODEL), row(widths[0])] + [pl.BlockSpec((SUB_TILES, NA_WIDTH, NA_QT), lambda i: (i, 0, 0))] * 2
                  + [row(w) for w in widths[1:]] + [so for _, so in late_specs],
        out_shape=[jax.ShapeDtypeStruct((t, D_MODEL), F32), rows_bf16[0]]
                  + [jax.ShapeDtypeStruct((t // NA_QT, NA_WIDTH, NA_QT), BF16)] * 2 + rows_bf16[1:]
                  + [jax.ShapeDtypeStruct(so, BF16) for so in late_shapes],
        compiler_params=pltpu.CompilerParams(dimension_semantics=("arbitrary",),
                                             vmem_limit_bytes=VMEM_LIMIT_BYTES),
        name="ffn1_ln1_inproj",
    )(x2d, wg, wu, wd, g1, b1, win, *late_weights)


NA_ROW_OFFSETS = 2 * NA_WIN_H - 1


def _na_bias_tiles(rpb):
    kw = NA_WIN_W
    c = np.arange(GRID_W)[None, :]
    kc = np.arange(GRID_W)[:, None]
    c_start = np.clip(c - kw // 2, 0, GRID_W - kw)
    col_ok = (kc >= c_start) & (kc < c_start + kw)
    dc = kc - c + (kw - 1)
    sel_c = ((dc[..., None] == np.arange(2 * kw - 1)) & col_ok[..., None]).astype(np.float32)
    tiles = jnp.einsum('hdw,kcw->hdkc', rpb, sel_c, precision=lax.Precision.HIGHEST)
    tiles = jnp.where(col_ok, tiles * LOG2E, NEG_BIG)
    tiles = jnp.concatenate([tiles, jnp.full_like(tiles[:, :1], NEG_BIG)], axis=1)
    return jnp.concatenate([tiles, tiles], axis=-1).astype(F32)


def _na_row_offsets(rows):
    kh = NA_WIN_H
    ri = np.arange(NA_RQ)[:, None]
    bj = np.arange(NA_BAND)[None, :]
    out = []
    for r0, b0 in ((0, 0), (NA_RQ, 0), (rows - NA_RQ, rows - NA_BAND)):
        r = r0 + ri
        kr = b0 + bj
        r_start = np.clip(r - kh // 2, 0, rows - kh)
        row_ok = (kr >= r_start) & (kr < r_start + kh)
        out.append(np.where(row_ok, kr - r + (kh - 1), NA_ROW_OFFSETS).tolist())
    return out


def _na_body(qt_ref, k_ref, vt_ref, tile_ref, o_ref, s_ref, bias_ref, *, rows):
    nblk = rows // NA_RQ
    lane = lax.broadcasted_iota(jnp.int32, (1, LANES), 1)
    first_head = lane < NA_HEAD_DIM
    first_head_row = lax.broadcasted_iota(jnp.int32, (LANES, 1), 0) < NA_HEAD_DIM

    @pl.when(pl.program_id(1) == 0)
    def _():
        for kind, offsets in enumerate(_na_row_offsets(rows)):
            for hh in range(2):
                for bj in range(NA_BAND):
                    for ri in range(0, NA_RQ, 2):
                        tile = jnp.where(first_head, tile_ref[hh, offsets[ri][bj]], tile_ref[hh, offsets[ri + 1][bj]])
                        bias_ref[kind, pl.ds(bj * GRID_W, GRID_W), pl.ds(hh * NA_QT + ri * GRID_W, LANES)] = tile

    def band_row(i):
        return jnp.clip(NA_RQ * i - NA_WIN_H // 2, 0, rows - NA_BAND)

    def scores(i, slot):
        q0 = pl.multiple_of(i * NA_QT, NA_QT)
        k0 = pl.multiple_of(band_row(i) * GRID_W, NA_QT)
        kind = jnp.where(i == 0, 0, jnp.where(i == nblk - 1, 2, 1))
        qt = qt_ref[i]
        k = k_ref[0, pl.ds(k0, NA_KT), :]
        zero = jnp.zeros((), BF16)
        qt2 = jnp.concatenate([jnp.where(first_head_row, qt, zero), jnp.where(first_head_row, zero, qt)], axis=1)
        s = jnp.dot(k, qt2, preferred_element_type=F32)
        s_ref[slot] = s + bias_ref[kind]

    def finish(i, slot):
        q0 = pl.multiple_of(i * NA_QT, NA_QT)
        g0 = band_row(i) // NA_RQ
        s = s_ref[slot]
        p = jnp.exp2(s - jnp.max(s, axis=0, keepdims=True)).astype(BF16)
        ones = jnp.ones((NA_ONES_ROWS, NA_QT), BF16)
        heads = []
        for hh in range(2):
            acc = None
            for g in range(NA_BAND // NA_RQ):
                vt = jnp.concatenate([vt_ref[g0 + g, hh * NA_HEAD_DIM:(hh + 1) * NA_HEAD_DIM, :], ones], axis=0)
                part = jnp.dot(vt, p[g * NA_QT:(g + 1) * NA_QT, hh * NA_QT:(hh + 1) * NA_QT],
                               preferred_element_type=F32)
                acc = part if acc is None else acc + part
            heads.append(acc[:NA_HEAD_DIM] * (1.0 / acc[NA_HEAD_DIM:NA_HEAD_DIM + 1]))
        o_ref[0, pl.ds(q0, NA_QT), :] = jnp.concatenate(heads, axis=0).T.astype(BF16)

    scores(0, 0)

    def group(j, carry):
        for u in range(NA_UNROLL):
            i = NA_UNROLL * j + u
            scores(jnp.minimum(i + 1, nblk - 1), (u + 1) % 2)
            finish(i, u % 2)
        return carry

    lax.fori_loop(0, nblk // NA_UNROLL, group, 0)


def _neighborhood_attention(nak, naqt, navt, tiles, rows):
    b, s, _ = nak.shape
    npairs = NA_HEADS // 2
    tok = pl.BlockSpec((1, s, LANES), lambda p, bi: (bi, 0, p))
    slabs = pl.BlockSpec((s // NA_QT, LANES, NA_QT), lambda p, bi: (bi, p, 0))
    return pl.pallas_call(
        functools.partial(_na_body, rows=rows),
        grid=(npairs, b),
        in_specs=[slabs, tok, slabs,
                  pl.BlockSpec((2, NA_ROW_OFFSETS + 1, GRID_W, LANES), lambda p, bi: (p, 0, 0, 0))],
        out_specs=tok,
        out_shape=jax.ShapeDtypeStruct((b, s, NA_WIDTH), BF16),
        scratch_shapes=[pltpu.VMEM((2, NA_KT, 2 * NA_QT), F32), pltpu.VMEM((3, NA_KT, 2 * NA_QT), F32)],
        compiler_params=pltpu.CompilerParams(dimension_semantics=("arbitrary", "arbitrary"),
                                             vmem_limit_bytes=VMEM_LIMIT_BYTES),
        name="neighborhood_attention",
    )(naqt, nak, navt, tiles)


def _split3(x):
    top16 = lambda a: lax.bitcast_convert_type(
        lax.bitcast_convert_type(a, jnp.int32) & jnp.int32(-65536), F32)
    hi = top16(x)
    r1 = x - hi
    mid = top16(r1)
    lo = r1 - mid
    return hi.astype(BF16), mid.astype(BF16), lo.astype(BF16)


def _gla_cumsum_matrix(fwd, nchunks):
    G = nchunks * GLA_CHUNK
    ci = lax.broadcasted_iota(jnp.int32, (G, G), 0)
    si = lax.broadcasted_iota(jnp.int32, (G, G), 1)
    in_order = (si <= ci) if fwd else (si >= ci)
    return jnp.where(in_order & ((ci // GLA_CHUNK) == (si // GLA_CHUNK)), 1.0, 0.0).astype(BF16)


def _gla_group(qk_ref, vg_ref, dl_ref, wdec_ref, bdec_ref, tri, states, r0, fwd, nchunks):
    C, KW, H = GLA_CHUNK, GLA_KEY_WIDTH, GLA_HEADS
    G = nchunks * C
    nseq = len(states)
    d = 0 if fwd else 1
    rows = pl.ds(r0, G)
    last = C - 1 if fwd else 0
    chunk = lambda a, u: a[u * C:(u + 1) * C]
    nt = (((1,), (1,)), ((), ()))

    zs = [(jnp.dot(dl_ref[b, rows, :], wdec_ref[d], preferred_element_type=F32) + bdec_ref[d:d + 1, :]) * LOG2E
          for b in range(nseq)]
    log_a = [(jnp.minimum(x, 0.0) - jnp.log2(1.0 + jnp.exp2(-jnp.abs(x)))) * (1.0 / GLA_GATE_TAU) for x in zs]

    bc = [jnp.dot(tri, jnp.concatenate(_split3(la), axis=1), preferred_element_type=F32) for la in log_a]
    b_cum = [(x[:, 0:KW] + x[:, KW:2 * KW]) + x[:, 2 * KW:3 * KW] for x in bc]
    b_last_rows = [[bcb[u * C + last:u * C + last + 1, :] for u in range(nchunks)] for bcb in b_cum]
    b_last = [jnp.concatenate([jnp.broadcast_to(r, (C, KW)) for r in rws], axis=0) for rws in b_last_rows]

    lane_head = lax.broadcasted_iota(jnp.int32, (1, KW), 1) // GLA_DK
    zero = jnp.zeros((), BF16)
    block_diag = lambda a: jnp.concatenate([jnp.where(lane_head == h, a, zero) for h in range(H)], axis=0)
    ci = lax.broadcasted_iota(jnp.int32, (C, C), 0)
    si = lax.broadcasted_iota(jnp.int32, (C, C), 1)
    causal = (si <= ci) if fwd else (si > ci)

    q_bd, k_t, kend_bd, v_heads = [], [], [], []
    for b in range(nseq):
        q = qk_ref[b, rows, 0:KW].astype(F32)
        k = qk_ref[b, rows, KW:2 * KW].astype(F32)
        v = vg_ref[b, rows, 0:GLA_VAL_WIDTH]
        q_t = ((q * (GLA_DK ** -0.5)) * jnp.exp2(b_cum[b])).astype(BF16)
        k_t.append((k * jnp.exp2(-b_cum[b])).astype(BF16))
        k_end = (k * jnp.exp2(b_last[b] - b_cum[b])).astype(BF16)
        q_bd.append([block_diag(chunk(q_t, u)) for u in range(nchunks)])
        kend_bd.append([block_diag(chunk(k_end, u)) for u in range(nchunks)])
        v_heads.append([[chunk(v, u)[:, h * GLA_DV:(h + 1) * GLA_DV] for h in range(H)] for u in range(nchunks)])

    units = [(b, u) for b in range(nseq) for u in range(nchunks)]
    attn = {bu: lax.dot_general(q_bd[bu[0]][bu[1]], chunk(k_t[bu[0]], bu[1]), nt, preferred_element_type=F32)
            for bu in units}
    attn = {bu: jnp.concatenate([jnp.where(causal, chunk(a, h), 0.0) for h in range(H)], axis=0).astype(BF16)
            for bu, a in attn.items()}
    st_add = {(b, u): lax.dot_general(jnp.concatenate(v_heads[b][u], axis=0), kend_bd[b][u],
                                      (((0,), (0,)), ((), ())), preferred_element_type=F32) for b, u in units}
    o_intra = {(b, u): jnp.concatenate([jnp.dot(chunk(attn[b, u], h), v_heads[b][u][h],
                                                preferred_element_type=F32) for h in range(H)], axis=0)
               for b, u in units}
    outs = [[None] * nchunks for _ in range(nseq)]
    states = list(states)
    for u in (range(nchunks) if fwd else reversed(range(nchunks))):
        for b in range(nseq):
            st = states[b]
            outs[b][u] = o_intra[b, u] + lax.dot_general(q_bd[b][u], st.astype(BF16), nt,
                                                         preferred_element_type=F32)
            states[b] = jnp.exp2(b_last_rows[b][u]) * st + st_add[b, u]
    return outs, states


def _gla_body(qk_ref, vg_ref, dl_ref, wdec_ref, bdec_ref, ng_ref, o_ref, st_ref, ob_ref, *, cbk, nbk):
    C, H = GLA_CHUNK, GLA_HEADS
    nseq = qk_ref.shape[0]
    ph = pl.program_id(1)
    j = pl.program_id(2)
    chunks_per_block = cbk // C
    nchunks = min(GLA_GROUP, chunks_per_block)
    ngroups = chunks_per_block // nchunks

    @pl.when(j == 0)
    def _():
        st_ref[...] = jnp.zeros_like(st_ref)

    def sweep(fwd):
        blk = j if fwd else nbk - 1 - j
        tri = _gla_cumsum_matrix(fwd, nchunks)

        def group(gi, carry):
            g0 = gi if fwd else ngroups - 1 - gi
            r0 = pl.multiple_of(g0 * nchunks * C, nchunks * C)
            outs, states = _gla_group(qk_ref, vg_ref, dl_ref, wdec_ref, bdec_ref, tri,
                                      [st_ref[b] for b in range(nseq)], r0, fwd, nchunks)
            for b in range(nseq):
                st_ref[b] = states[b]
                for u in range(nchunks):
                    cg = blk * chunks_per_block + g0 * nchunks + u
                    if not fwd:
                        ob_ref[b, cg] = outs[b][u]
                        continue
                    c0 = pl.multiple_of(r0 + u * C, C)
                    o = outs[b][u] + ob_ref[b, cg]
                    o = o * lax.rsqrt(jnp.mean(o * o, axis=-1, keepdims=True) + RMS_EPS) * ng_ref[...]
                    gate = vg_ref[b, pl.ds(c0, C), GLA_VAL_WIDTH:2 * GLA_VAL_WIDTH].astype(F32)
                    gate = jnp.concatenate([gate[:, h * GLA_DV:(h + 1) * GLA_DV] for h in range(H)], axis=0)
                    o = o * (gate * jax.nn.sigmoid(gate))
                    o_ref[b, pl.ds(c0, C), :] = jnp.concatenate(
                        [o[h * C:(h + 1) * C] for h in range(H)], axis=1).astype(BF16)
            return carry

        lax.fori_loop(0, ngroups, group, 0, unroll=True)

    @pl.when(ph == 0)
    def _():
        sweep(False)

    @pl.when(ph == 1)
    def _():
        sweep(True)


def _gla(gqk, gvg, gdl, wdec_pad, bdec, norm_g):
    b, s, _ = gqk.shape
    cbk = min(GLA_BLOCK, s)
    nbk = s // cbk
    nseq = GLA_SEQS if b % GLA_SEQS == 0 else 1
    blk = lambda w: pl.BlockSpec((nseq, cbk, w), lambda bi, ph, j: (bi, jnp.where(ph == 0, nbk - 1 - j, j), 0))
    return pl.pallas_call(
        functools.partial(_gla_body, cbk=cbk, nbk=nbk),
        grid=(b // nseq, 2, nbk),
        in_specs=[blk(2 * GLA_KEY_WIDTH), blk(2 * GLA_VAL_WIDTH), blk(2 * GLA_GATE_RANK),
                  _const_spec(wdec_pad.shape), _const_spec(bdec.shape), _const_spec(norm_g.shape)],
        out_specs=pl.BlockSpec((nseq, cbk, GLA_VAL_WIDTH), lambda bi, ph, j: (bi, jnp.where(ph == 0, 0, j), 0)),
        out_shape=jax.ShapeDtypeStruct((b, s, GLA_VAL_WIDTH), BF16),
        scratch_shapes=[pltpu.VMEM((nseq, GLA_DV, GLA_KEY_WIDTH), F32),
                        pltpu.VMEM((nseq, s // GLA_CHUNK, GLA_HEADS * GLA_CHUNK, GLA_DV), F32)],
        compiler_params=pltpu.CompilerParams(dimension_semantics=("arbitrary", "arbitrary", "arbitrary"),
                                             vmem_limit_bytes=VMEM_LIMIT_BYTES),
        name="gla_bidirectional",
    )(gqk, gvg, gdl, wdec_pad, bdec, norm_g)


def _merge_ffn_body(x1_ref, na_ref, gla_ref, wgate_ref, wbn_ref, wbg_ref, wo_ref, g2_ref, b2_ref,
                    wg_ref, wu_ref, wd_ref, g3_ref, b3_ref, o_ref):
    tm = x1_ref.shape[0]
    subs = [pl.ds(r, SUB_TILE_ROWS) for r in range(0, tm, SUB_TILE_ROWS)]
    dot = functools.partial(jnp.dot, preferred_element_type=F32)
    x1s = [x1_ref[rows, :] for rows in subs]
    pre = [(dot(x1.astype(BF16), wgate_ref[...]), dot(na_ref[rows, :], wbn_ref[...]),
            dot(gla_ref[rows, :], wbg_ref[...])) for rows, x1 in zip(subs, x1s)]
    z2s = []
    for x1, (gates, y_na, y_gla) in zip(x1s, pre):
        merged = jax.nn.sigmoid(gates[:, :D_MODEL]) * y_na + jax.nn.sigmoid(gates[:, D_MODEL:]) * y_gla
        z2s.append(ALPHA * x1 + dot(merged.astype(BF16), wo_ref[...]))
    x2s = [_layer_norm(z2, g2_ref[...], b2_ref[...]) for z2 in z2s]
    z3s = [ALPHA * x2 + 0.5 * _swiglu(x2.astype(BF16), wg_ref, wu_ref, wd_ref) for x2 in x2s]
    for rows, z3 in zip(subs, z3s):
        o_ref[rows, :] = _layer_norm(z3, g3_ref[...], b3_ref[...])


def _merge_ffn(x1, na_o, gla_o, wgate, wbn, wbg, wo, g2, b2, wg, wu, wd, g3, b3):
    t = x1.shape[0]
    tm = min(MERGE_TILE, t)
    row = lambda w: pl.BlockSpec((tm, w), lambda i: (i, 0))
    consts = (wgate, wbn, wbg, wo, g2, b2, wg, wu, wd, g3, b3)
    return pl.pallas_call(
        _merge_ffn_body,
        grid=(t // tm,),
        in_specs=[row(D_MODEL), row(NA_WIDTH), row(GLA_VAL_WIDTH)] + [_const_spec(c.shape) for c in consts],
        out_specs=row(D_MODEL),
        out_shape=jax.ShapeDtypeStruct((t, D_MODEL), F32),
        compiler_params=pltpu.CompilerParams(dimension_semantics=("arbitrary",),
                                             vmem_limit_bytes=MERGE_VMEM_LIMIT_BYTES),
        name="merge_ln2_ffn2_ln3",
    )(x1, na_o, gla_o, *consts)


def _layer(x, layer, late_stacked, ffn1_w_gate, ffn1_w_up, ffn1_w_down, ln1_g, ln1_b, w_in, na_rpb, gla_w_dec2,
           gla_b_dec, gla_norm_g, ln2_g, ln2_b, ln3_g, ln3_b):
    b, s, d = x.shape
    assert d == D_MODEL and s % (NA_UNROLL * NA_QT) == 0 and s // GRID_W >= NA_BAND
    rows = s // GRID_W
    bf = lambda w: w.astype(BF16)
    vec = lambda p: p.reshape(1, -1)

    x1, nak, naqt, navt, gqk, gvg, gdl, w2_gate, w2_up, w2_down, wo, wbn, wbg, w_merge_gates = _ffn_inproj(
        x.reshape(b * s, d), bf(ffn1_w_gate), bf(ffn1_w_up), bf(ffn1_w_down), vec(ln1_g), vec(ln1_b), bf(w_in),
        late_stacked, layer)

    na_o = _neighborhood_attention(nak.reshape(b, s, -1), naqt, navt, _na_bias_tiles(na_rpb), rows)

    zr = jnp.zeros((GLA_GATE_RANK, GLA_KEY_WIDTH), gla_w_dec2.dtype)
    wdec_pad = bf(jnp.stack([jnp.concatenate([gla_w_dec2[0], zr]), jnp.concatenate([zr, gla_w_dec2[1]])]))
    gla_o = _gla(gqk.reshape(b, s, -1), gvg.reshape(b, s, -1), gdl.reshape(b, s, -1),
                 wdec_pad, gla_b_dec, vec(gla_norm_g))

    out = _merge_ffn(x1, na_o.reshape(b * s, -1), gla_o.reshape(b * s, -1), w_merge_gates, wbn, wbg, wo,
                     vec(ln2_g), vec(ln2_b), w2_gate, w2_up, w2_down, vec(ln3_g), vec(ln3_b))
    return out.reshape(b, s, d)


def kernel(x, ffn1_w_gate, ffn1_w_up, ffn1_w_down, ln1_g, ln1_b, w_in, na_rpb, gla_w_dec2, gla_b_dec, gla_norm_g,
           w_branch_na, w_branch_gla, w_out, ln2_g, ln2_b, ffn2_w_gate, ffn2_w_up, ffn2_w_down, ln3_g, ln3_b):
    params = (ffn1_w_gate, ffn1_w_up, ffn1_w_down, ln1_g, ln1_b, w_in, na_rpb, gla_w_dec2, gla_b_dec, gla_norm_g,
              ln2_g, ln2_b, ln3_g, ln3_b)
    late_stacked = (ffn2_w_gate, ffn2_w_up, ffn2_w_down, w_out, w_branch_na, w_branch_gla, w_in)
    for l in range(DEPTH):
        x = _layer(x, l, late_stacked, *(p[l] for p in params))
    return x
```

```python
import functools

import jax
import jax.numpy as jnp
import numpy as np
from jax import lax
from jax.experimental import pallas as pl
from jax.experimental.pallas import tpu as pltpu

F32 = jnp.float32
BF16 = jnp.bfloat16

D_MODEL = 1024
D_FF = 2816
GRID_W = 64
NA_HEADS = 8
NA_HEAD_DIM = 64
NA_WIDTH = NA_HEADS * NA_HEAD_DIM
NA_WIN_H = 8
NA_WIN_W = 16
GLA_HEADS = 4
GLA_DK = 64
GLA_DV = 128
GLA_KEY_WIDTH = GLA_HEADS * GLA_DK
GLA_VAL_WIDTH = GLA_HEADS * GLA_DV
GLA_GATE_RANK = 16
GLA_GATE_TAU = 16.0
GLA_CHUNK = 64
DEPTH = 1
ALPHA = (2 * DEPTH) ** 0.25
LN_EPS = 1e-5
RMS_EPS = 1e-6

OFF_NA_END = 3 * NA_WIDTH
OFF_GQK_END = OFF_NA_END + 2 * GLA_KEY_WIDTH
OFF_GVG_END = OFF_GQK_END + 2 * GLA_VAL_WIDTH
OFF_GDL_END = OFF_GVG_END + 2 * GLA_GATE_RANK
N_IN = OFF_GDL_END + 2 * D_MODEL

LANES = 128
BF16_SUBLANES = 16
F32_TOP16_MASK = -65536
N_LATE_WEIGHTS = 7
VMEM_LIMIT_BYTES = 56 * 1024 * 1024

NA_RQ = 4
NA_BAND = NA_RQ + NA_WIN_H
NA_QT = NA_RQ * GRID_W
NA_KT = NA_BAND * GRID_W
NA_UNROLL = 16
NA_ONES_ROWS = 16
NEG_BIG = -1e30
LOG2E = 1.4426950408889634
NA_Q_SCALE = LOG2E * NA_HEAD_DIM ** -0.5

GLA_GROUP = 4
GLA_BLOCK = 1024
GLA_SEQS = 2
TOKEN_TILE = 512
SUB_TILES = 2
SUB_TILE_ROWS = TOKEN_TILE // SUB_TILES
MERGE_TILE = 1024
MERGE_VMEM_LIMIT_BYTES = 60 * 1024 * 1024


def _const_spec(shape):
    nd = len(shape)
    return pl.BlockSpec(shape, lambda *_: (0,) * nd, pipeline_mode=pl.Buffered(1))


def _layer_norm(z, g, b):
    mu = jnp.mean(z, axis=-1, keepdims=True)
    zc = z - mu
    var = jnp.mean(zc * zc, axis=-1, keepdims=True)
    return zc * lax.rsqrt(var + LN_EPS) * g + b


def _swiglu(xb, wg_ref, wu_ref, wd_ref):
    g = jnp.dot(xb, wg_ref[...], preferred_element_type=F32)
    u = jnp.dot(xb, wu_ref[...], preferred_element_type=F32)
    h = (g * jax.nn.sigmoid(g)) * u
    return jnp.dot(h.astype(BF16), wd_ref[...], preferred_element_type=F32)


def _ffn_inproj_body(x_ref, wg_ref, wu_ref, wd_ref, g1_ref, b1_ref, win_ref, *refs):
    late_f32, outs = refs[:N_LATE_WEIGHTS], refs[N_LATE_WEIGHTS:]
    x1_ref, nak_ref, naqt_ref, navt_ref, gqk_ref, gvg_ref, gdl_ref = outs[:-N_LATE_WEIGHTS]
    late_bf16 = outs[-N_LATE_WEIGHTS:]
    for src, dst in zip(late_f32[:-1], late_bf16[:-1]):
        dst[...] = src[...].astype(BF16)
    late_bf16[-1][...] = late_f32[-1][:, OFF_GDL_END:].astype(BF16)

    tm = x_ref.shape[0]
    subs = [pl.ds(r, tm // SUB_TILES) for r in range(0, tm, tm // SUB_TILES)]
    zs = []
    for rows in subs:
        x = x_ref[rows, :]
        zs.append(ALPHA * x + 0.5 * _swiglu(x.astype(BF16), wg_ref, wu_ref, wd_ref))
    for g, (rows, z) in enumerate(zip(subs, zs)):
        x1 = _layer_norm(z, g1_ref[...], b1_ref[...])
        x1_ref[rows, :] = x1
        proj = jnp.dot(x1.astype(BF16), win_ref[:, :OFF_GDL_END], preferred_element_type=F32)
        nak_ref[rows, :] = proj[:, NA_WIDTH:2 * NA_WIDTH].astype(BF16)
        naqt_ref[g] = (proj[:, :NA_WIDTH] * NA_Q_SCALE).T.astype(BF16)
        navt_ref[g] = proj[:, 2 * NA_WIDTH:OFF_NA_END].T.astype(BF16)
        gqk_ref[rows, :] = proj[:, OFF_NA_END:OFF_GQK_END].astype(BF16)
        gvg_ref[rows, :] = proj[:, OFF_GQK_END:OFF_GVG_END].astype(BF16)
        gdl_ref[rows, :] = proj[:, OFF_GVG_END:OFF_GDL_END].astype(BF16)


def _row_chunk(nrows, nsteps):
    return next(c for c in range(BF16_SUBLANES, nrows + 1, BF16_SUBLANES) if nrows % c == 0 and c * nsteps >= nrows)


def _ffn_inproj(x2d, wg, wu, wd, g1, b1, win, late_weights, layer):
    t = x2d.shape[0]
    tm = min(TOKEN_TILE, t)
    n = t // tm
    assert tm // SUB_TILES == NA_QT and len(late_weights) == N_LATE_WEIGHTS
    row = lambda w: pl.BlockSpec((tm, w), lambda i: (i, 0))
    widths = (NA_WIDTH, OFF_GQK_END - OFF_NA_END, OFF_GVG_END - OFF_GQK_END, OFF_GDL_END - OFF_GVG_END)
    rows_bf16 = [jax.ShapeDtypeStruct((t, w), BF16) for w in widths]
    late_shapes = [w.shape[1:] for w in late_weights[:-1]] + [(D_MODEL, N_IN - OFF_GDL_END)]

    def chunk_specs(w_rows, cols_in, cols_out):
        c = _row_chunk(w_rows, n)
        last = w_rows // c - 1
        return (pl.BlockSpec((None, c, cols_in), lambda i: (layer, jnp.minimum(i, last), 0)),
                pl.BlockSpec((c, cols_out), lambda i: (jnp.minimum(i, last), 0)))

    late_specs = [chunk_specs(w.shape[1], w.shape[2], so[1]) for w, so in zip(late_weights, late_shapes)]
    return pl.pallas_call(
        _ffn_inproj_body,
        grid=(n,),
        in_specs=[row(D_MODEL), _const_spec(wg.shape), _const_spec(wu.shape), _const_spec(wd.shape),
                  _const_spec(g1.shape), _const_spec(b1.shape), _const_spec(win.shape)]
                 + [si for si, _ in late_specs],
        out_specs=[row(D_MODEL), row(widths[0])] + [pl.BlockSpec((SUB_TILES, NA_WIDTH, NA_QT), lambda i: (i, 0, 0))] * 2
                  + [row(w) for w in widths[1:]] + [so for _, so in late_specs],
        out_shape=[jax.ShapeDtypeStruct((t, D_MODEL), F32), rows_bf16[0]]
                  + [jax.ShapeDtypeStruct((t // NA_QT, NA_WIDTH, NA_QT), BF16)] * 2 + rows_bf16[1:]
                  + [jax.ShapeDtypeStruct(so, BF16) for so in late_shapes],
        compiler_params=pltpu.CompilerParams(dimension_semantics=("arbitrary",),
                                             vmem_limit_bytes=VMEM_LIMIT_BYTES),
        name="ffn1_ln1_inproj",
    )(x2d, wg, wu, wd, g1, b1, win, *late_weights)


NA_ROW_OFFSETS = 2 * NA_WIN_H - 1


def _na_bias_tiles(rpb):
    kw = NA_WIN_W
    c = np.arange(GRID_W)[None, :]
    kc = np.arange(GRID_W)[:, None]
    c_start = np.clip(c - kw // 2, 0, GRID_W - kw)
    col_ok = (kc >= c_start) & (kc < c_start + kw)
    dc = kc - c + (kw - 1)
    sel_c = ((dc[..., None] == np.arange(2 * kw - 1)) & col_ok[..., None]).astype(np.float32)
    tiles = jnp.einsum('hdw,kcw->hdkc', rpb, sel_c, precision=lax.Precision.HIGHEST)
    tiles = jnp.where(col_ok, tiles * LOG2E, NEG_BIG)
    tiles = jnp.concatenate([tiles, jnp.full_like(tiles[:, :1], NEG_BIG)], axis=1)
    return jnp.concatenate([tiles, tiles], axis=-1).astype(F32)


def _na_row_offsets(rows):
    kh = NA_WIN_H
    ri = np.arange(NA_RQ)[:, None]
    bj = np.arange(NA_BAND)[None, :]
    out = []
    for r0, b0 in ((0, 0), (NA_RQ, 0), (rows - NA_RQ, rows - NA_BAND)):
        r = r0 + ri
        kr = b0 + bj
        r_start = np.clip(r - kh // 2, 0, rows - kh)
        row_ok = (kr >= r_start) & (kr < r_start + kh)
        out.append(np.where(row_ok, kr - r + (kh - 1), NA_ROW_OFFSETS).tolist())
    return out


def _na_body(qt_ref, k_ref, vt_ref, tile_ref, o_ref, s_ref, bias_ref, *, rows):
    nblk = rows // NA_RQ
    lane = lax.broadcasted_iota(jnp.int32, (1, LANES), 1)
    first_head = lane < NA_HEAD_DIM
    first_head_row = lax.broadcasted_iota(jnp.int32, (LANES, 1), 0) < NA_HEAD_DIM

    @pl.when(pl.program_id(1) == 0)
    def _():
        for kind, offsets in enumerate(_na_row_offsets(rows)):
            for hh in range(2):
                for bj in range(NA_BAND):
                    for ri in range(0, NA_RQ, 2):
                        tile = jnp.where(first_head, tile_ref[hh, offsets[ri][bj]], tile_ref[hh, offsets[ri + 1][bj]])
                        bias_ref[kind, pl.ds(bj * GRID_W, GRID_W), pl.ds(hh * NA_QT + ri * GRID_W, LANES)] = tile

    def band_row(i):
        return jnp.clip(NA_RQ * i - NA_WIN_H // 2, 0, rows - NA_BAND)

    def scores(i, slot):
        q0 = pl.multiple_of(i * NA_QT, NA_QT)
        k0 = pl.multiple_of(band_row(i) * GRID_W, NA_QT)
        kind = jnp.where(i == 0, 0, jnp.where(i == nblk - 1, 2, 1))
        qt = qt_ref[i]
        k = k_ref[0, pl.ds(k0, NA_KT), :]
        zero = jnp.zeros((), BF16)
        qt2 = jnp.concatenate([jnp.where(first_head_row, qt, zero), jnp.where(first_head_row, zero, qt)], axis=1)
        s = jnp.dot(k, qt2, preferred_element_type=F32)
        s_ref[slot] = s + bias_ref[kind]

    def finish(i, slot):
        q0 = pl.multiple_of(i * NA_QT, NA_QT)
        g0 = band_row(i) // NA_RQ
        s = s_ref[slot]
        p = jnp.exp2(s - jnp.max(s, axis=0, keepdims=True)).astype(BF16)
        ones = jnp.ones((NA_ONES_ROWS, NA_KT), BF16)
        heads = []
        for hh in range(2):
            vt = jnp.concatenate([vt_ref[g0 + g, hh * NA_HEAD_DIM:(hh + 1) * NA_HEAD_DIM, :]
                                  for g in range(NA_BAND // NA_RQ)], axis=1)
            acc = jnp.dot(jnp.concatenate([vt, ones], axis=0), p[:, hh * NA_QT:(hh + 1) * NA_QT],
                          preferred_element_type=F32)
            heads.append(acc[:NA_HEAD_DIM] * (1.0 / acc[NA_HEAD_DIM:NA_HEAD_DIM + 1]))
        o_ref[0, pl.ds(q0, NA_QT), :] = jnp.concatenate(heads, axis=0).T.astype(BF16)

    scores(0, 0)

    def group(j, carry):
        for u in range(NA_UNROLL):
            i = NA_UNROLL * j + u
            scores(jnp.minimum(i + 1, nblk - 1), (u + 1) % 2)
            finish(i, u % 2)
        return carry

    lax.fori_loop(0, nblk // NA_UNROLL, group, 0)


def _neighborhood_attention(nak, naqt, navt, tiles, rows):
    b, s, _ = nak.shape
    npairs = NA_HEADS // 2
    tok = pl.BlockSpec((1, s, LANES), lambda p, bi: (bi, 0, p))
    slabs = pl.BlockSpec((s // NA_QT, LANES, NA_QT), lambda p, bi: (bi, p, 0))
    return pl.pallas_call(
        functools.partial(_na_body, rows=rows),
        grid=(npairs, b),
        in_specs=[slabs, tok, slabs,
                  pl.BlockSpec((2, NA_ROW_OFFSETS + 1, GRID_W, LANES), lambda p, bi: (p, 0, 0, 0))],
        out_specs=tok,
        out_shape=jax.ShapeDtypeStruct((b, s, NA_WIDTH), BF16),
        scratch_shapes=[pltpu.VMEM((2, NA_KT, 2 * NA_QT), F32), pltpu.VMEM((3, NA_KT, 2 * NA_QT), F32)],
        compiler_params=pltpu.CompilerParams(dimension_semantics=("arbitrary", "arbitrary"),
                                             vmem_limit_bytes=VMEM_LIMIT_BYTES),
        name="neighborhood_attention",
    )(naqt, nak, navt, tiles)


def _split3(x):
    top16 = lambda a: lax.bitcast_convert_type(
        lax.bitcast_convert_type(a, jnp.int32) & jnp.int32(F32_TOP16_MASK), F32)
    hi = top16(x)
    r1 = x - hi
    mid = top16(r1)
    lo = r1 - mid
    return hi.astype(BF16), mid.astype(BF16), lo.astype(BF16)


def _gla_cumsum_matrix(fwd, nchunks):
    G = nchunks * GLA_CHUNK
    ci = lax.broadcasted_iota(jnp.int32, (G, G), 0)
    si = lax.broadcasted_iota(jnp.int32, (G, G), 1)
    in_order = (si <= ci) if fwd else (si >= ci)
    return jnp.where(in_order & ((ci // GLA_CHUNK) == (si // GLA_CHUNK)), 1.0, 0.0).astype(BF16)


def _gla_group(qk_ref, vg_ref, dl_ref, wdec_ref, bdec_ref, tri, states, r0, fwd, nchunks):
    C, KW, H = GLA_CHUNK, GLA_KEY_WIDTH, GLA_HEADS
    G = nchunks * C
    nseq = len(states)
    d = 0 if fwd else 1
    rows = pl.ds(r0, G)
    last = C - 1 if fwd else 0
    chunk = lambda a, u: a[u * C:(u + 1) * C]
    nt = (((1,), (1,)), ((), ()))

    zs = [(jnp.dot(dl_ref[b, rows, :], wdec_ref[d], preferred_element_type=F32) + bdec_ref[d:d + 1, :]) * LOG2E
          for b in range(nseq)]
    log_a = [(jnp.minimum(x, 0.0) - jnp.log2(1.0 + jnp.exp2(-jnp.abs(x)))) * (1.0 / GLA_GATE_TAU) for x in zs]

    bc = [jnp.dot(tri, jnp.concatenate(_split3(la), axis=1), preferred_element_type=F32) for la in log_a]
    b_cum = [(x[:, 0:KW] + x[:, KW:2 * KW]) + x[:, 2 * KW:3 * KW] for x in bc]
    b_last_rows = [[bcb[u * C + last:u * C + last + 1, :] for u in range(nchunks)] for bcb in b_cum]
    b_last = [jnp.concatenate([jnp.broadcast_to(r, (C, KW)) for r in rws], axis=0) for rws in b_last_rows]

    lane_head = lax.broadcasted_iota(jnp.int32, (1, KW), 1) // GLA_DK
    zero = jnp.zeros((), BF16)
    block_diag = lambda a: jnp.concatenate([jnp.where(lane_head == h, a, zero) for h in range(H)], axis=0)
    ci = lax.broadcasted_iota(jnp.int32, (C, C), 0)
    si = lax.broadcasted_iota(jnp.int32, (C, C), 1)
    causal = (si <= ci) if fwd else (si > ci)

    q_bd, k_t, kend_bd, v_heads = [], [], [], []
    for b in range(nseq):
        q = qk_ref[b, rows, 0:KW].astype(F32)
        k = qk_ref[b, rows, KW:2 * KW].astype(F32)
        v = vg_ref[b, rows, 0:GLA_VAL_WIDTH]
        q_t = ((q * (GLA_DK ** -0.5)) * jnp.exp2(b_cum[b])).astype(BF16)
        k_t.append((k * jnp.exp2(-b_cum[b])).astype(BF16))
        k_end = (k * jnp.exp2(b_last[b] - b_cum[b])).astype(BF16)
        q_bd.append([block_diag(chunk(q_t, u)) for u in range(nchunks)])
        kend_bd.append([block_diag(chunk(k_end, u)) for u in range(nchunks)])
        v_heads.append([[chunk(v, u)[:, h * GLA_DV:(h + 1) * GLA_DV] for h in range(H)] for u in range(nchunks)])

    units = [(b, u) for b in range(nseq) for u in range(nchunks)]
    attn = {bu: lax.dot_general(q_bd[bu[0]][bu[1]], chunk(k_t[bu[0]], bu[1]), nt, preferred_element_type=F32)
            for bu in units}
    attn = {bu: jnp.concatenate([jnp.where(causal, chunk(a, h), 0.0) for h in range(H)], axis=0).astype(BF16)
            for bu, a in attn.items()}
    st_add = {(b, u): lax.dot_general(jnp.concatenate(v_heads[b][u], axis=0), kend_bd[b][u],
                                      (((0,), (0,)), ((), ())), preferred_element_type=F32) for b, u in units}
    o_intra = {(b, u): jnp.concatenate([jnp.dot(chunk(attn[b, u], h), v_heads[b][u][h],
                                                preferred_element_type=F32) for h in range(H)], axis=0)
               for b, u in units}
    outs = [[None] * nchunks for _ in range(nseq)]
    states = list(states)
    for u in (range(nchunks) if fwd else reversed(range(nchunks))):
        for b in range(nseq):
            st = states[b]
            outs[b][u] = o_intra[b, u] + lax.dot_general(q_bd[b][u], st.astype(BF16), nt,
                                                         preferred_element_type=F32)
            states[b] = jnp.exp2(b_last_rows[b][u]) * st + st_add[b, u]
    return outs, states


def _gla_body(qk_ref, vg_ref, dl_ref, wdec_ref, bdec_ref, ng_ref, o_ref, st_ref, ob_ref, *, cbk, nbk):
    C, H = GLA_CHUNK, GLA_HEADS
    nseq = qk_ref.shape[0]
    ph = pl.program_id(1)
    j = pl.program_id(2)
    chunks_per_block = cbk // C
    nchunks = min(GLA_GROUP, chunks_per_block)
    ngroups = chunks_per_block // nchunks

    @pl.when(j == 0)
    def _():
        st_ref[...] = jnp.zeros_like(st_ref)

    def sweep(fwd):
        blk = j if fwd else nbk - 1 - j
        tri = _gla_cumsum_matrix(fwd, nchunks)

        def group(gi, carry):
            g0 = gi if fwd else ngroups - 1 - gi
            r0 = pl.multiple_of(g0 * nchunks * C, nchunks * C)
            outs, states = _gla_group(qk_ref, vg_ref, dl_ref, wdec_ref, bdec_ref, tri,
                                      [st_ref[b] for b in range(nseq)], r0, fwd, nchunks)
            for b in range(nseq):
                st_ref[b] = states[b]
                for u in range(nchunks):
                    cg = blk * chunks_per_block + g0 * nchunks + u
                    if not fwd:
                        ob_ref[b, cg] = outs[b][u]
                        continue
                    c0 = pl.multiple_of(r0 + u * C, C)
                    o = outs[b][u] + ob_ref[b, cg]
                    o = o * lax.rsqrt(jnp.mean(o * o, axis=-1, keepdims=True) + RMS_EPS) * ng_ref[...]
                    gate = vg_ref[b, pl.ds(c0, C), GLA_VAL_WIDTH:2 * GLA_VAL_WIDTH].astype(F32)
                    gate = jnp.concatenate([gate[:, h * GLA_DV:(h + 1) * GLA_DV] for h in range(H)], axis=0)
                    o = o * (gate * jax.nn.sigmoid(gate))
                    o_ref[b, pl.ds(c0, C), :] = jnp.concatenate(
                        [o[h * C:(h + 1) * C] for h in range(H)], axis=1).astype(BF16)
            return carry

        lax.fori_loop(0, ngroups, group, 0, unroll=True)

    @pl.when(ph == 0)
    def _():
        sweep(False)

    @pl.when(ph == 1)
    def _():
        sweep(True)


def _gla(gqk, gvg, gdl, wdec_pad, bdec, norm_g):
    b, s, _ = gqk.shape
    cbk = min(GLA_BLOCK, s)
    nbk = s // cbk
    nseq = GLA_SEQS if b % GLA_SEQS == 0 else 1
    blk = lambda w: pl.BlockSpec((nseq, cbk, w), lambda bi, ph, j: (bi, jnp.where(ph == 0, nbk - 1 - j, j), 0))
    return pl.pallas_call(
        functools.partial(_gla_body, cbk=cbk, nbk=nbk),
        grid=(b // nseq, 2, nbk),
        in_specs=[blk(2 * GLA_KEY_WIDTH), blk(2 * GLA_VAL_WIDTH), blk(2 * GLA_GATE_RANK),
                  _const_spec(wdec_pad.shape), _const_spec(bdec.shape), _const_spec(norm_g.shape)],
        out_specs=pl.BlockSpec((nseq, cbk, GLA_VAL_WIDTH), lambda bi, ph, j: (bi, jnp.where(ph == 0, 0, j), 0)),
        out_shape=jax.ShapeDtypeStruct((b, s, GLA_VAL_WIDTH), BF16),
        scratch_shapes=[pltpu.VMEM((nseq, GLA_DV, GLA_KEY_WIDTH), F32),
                        pltpu.VMEM((nseq, s // GLA_CHUNK, GLA_HEADS * GLA_CHUNK, GLA_DV), F32)],
        compiler_params=pltpu.CompilerParams(dimension_semantics=("arbitrary", "arbitrary", "arbitrary"),
                                             vmem_limit_bytes=VMEM_LIMIT_BYTES),
        name="gla_bidirectional",
    )(gqk, gvg, gdl, wdec_pad, bdec, norm_g)


def _merge_ffn_body(x1_ref, na_ref, gla_ref, wgate_ref, wbn_ref, wbg_ref, wo_ref, g2_ref, b2_ref,
                    wg_ref, wu_ref, wd_ref, g3_ref, b3_ref, o_ref):
    tm = x1_ref.shape[0]
    subs = [pl.ds(r, SUB_TILE_ROWS) for r in range(0, tm, SUB_TILE_ROWS)]
    dot = functools.partial(jnp.dot, preferred_element_type=F32)
    x1s = [x1_ref[rows, :] for rows in subs]
    pre = [(dot(x1.astype(BF16), wgate_ref[...]), dot(na_ref[rows, :], wbn_ref[...]),
            dot(gla_ref[rows, :], wbg_ref[...])) for rows, x1 in zip(subs, x1s)]
    z2s = []
    for x1, (gates, y_na, y_gla) in zip(x1s, pre):
        merged = jax.nn.sigmoid(gates[:, :D_MODEL]) * y_na + jax.nn.sigmoid(gates[:, D_MODEL:]) * y_gla
        z2s.append(ALPHA * x1 + dot(merged.astype(BF16), wo_ref[...]))
    x2s = [_layer_norm(z2, g2_ref[...], b2_ref[...]) for z2 in z2s]
    z3s = [ALPHA * x2 + 0.5 * _swiglu(x2.astype(BF16), wg_ref, wu_ref, wd_ref) for x2 in x2s]
    for rows, z3 in zip(subs, z3s):
        o_ref[rows, :] = _layer_norm(z3, g3_ref[...], b3_ref[...])


def _merge_ffn(x1, na_o, gla_o, wgate, wbn, wbg, wo, g2, b2, wg, wu, wd, g3, b3):
    t = x1.shape[0]
    tm = min(MERGE_TILE, t)
    row = lambda w: pl.BlockSpec((tm, w), lambda i: (i, 0))
    consts = (wgate, wbn, wbg, wo, g2, b2, wg, wu, wd, g3, b3)
    return pl.pallas_call(
        _merge_ffn_body,
        grid=(t // tm,),
        in_specs=[row(D_MODEL), row(NA_WIDTH), row(GLA_VAL_WIDTH)] + [_const_spec(c.shape) for c in consts],
        out_specs=row(D_MODEL),
        out_shape=jax.ShapeDtypeStruct((t, D_MODEL), F32),
        compiler_params=pltpu.CompilerParams(dimension_semantics=("arbitrary",),
                                             vmem_limit_bytes=MERGE_VMEM_LIMIT_BYTES),
        name="merge_ln2_ffn2_ln3",
    )(x1, na_o, gla_o, *consts)


def _layer(x, layer, late_stacked, ffn1_w_gate, ffn1_w_up, ffn1_w_down, ln1_g, ln1_b, w_in, na_rpb, gla_w_dec2,
           gla_b_dec, gla_norm_g, ln2_g, ln2_b, ln3_g, ln3_b):
    b, s, d = x.shape
    assert d == D_MODEL and s % (NA_UNROLL * NA_QT) == 0 and s // GRID_W >= NA_BAND
    rows = s // GRID_W
    bf = lambda w: w.astype(BF16)
    vec = lambda p: p.reshape(1, -1)

    x1, nak, naqt, navt, gqk, gvg, gdl, w2_gate, w2_up, w2_down, wo, wbn, wbg, w_merge_gates = _ffn_inproj(
        x.reshape(b * s, d), bf(ffn1_w_gate), bf(ffn1_w_up), bf(ffn1_w_down), vec(ln1_g), vec(ln1_b), bf(w_in),
        late_stacked, layer)

    na_o = _neighborhood_attention(nak.reshape(b, s, -1), naqt, navt, _na_bias_tiles(na_rpb), rows)

    zr = jnp.zeros((GLA_GATE_RANK, GLA_KEY_WIDTH), gla_w_dec2.dtype)
    wdec_pad = bf(jnp.stack([jnp.concatenate([gla_w_dec2[0], zr]), jnp.concatenate([zr, gla_w_dec2[1]])]))
    gla_o = _gla(gqk.reshape(b, s, -1), gvg.reshape(b, s, -1), gdl.reshape(b, s, -1),
                 wdec_pad, gla_b_dec, vec(gla_norm_g))

    out = _merge_ffn(x1, na_o.reshape(b * s, -1), gla_o.reshape(b * s, -1), w_merge_gates, wbn, wbg, wo,
                     vec(ln2_g), vec(ln2_b), w2_gate, w2_up, w2_down, vec(ln3_g), vec(ln3_b))
    return out.reshape(b, s, d)


def kernel(x, ffn1_w_gate, ffn1_w_up, ffn1_w_down, ln1_g, ln1_b, w_in, na_rpb, gla_w_dec2, gla_b_dec, gla_norm_g,
           w_branch_na, w_branch_gla, w_out, ln2_g, ln2_b, ffn2_w_gate, ffn2_w_up, ffn2_w_down, ln3_g, ln3_b):
    params = (ffn1_w_gate, ffn1_w_up, ffn1_w_down, ln1_g, ln1_b, w_in, na_rpb, gla_w_dec2, gla_b_dec, gla_norm_g,
              ln2_g, ln2_b, ln3_g, ln3_b)
    late_stacked = (ffn2_w_gate, ffn2_w_up, ffn2_w_down, w_out, w_branch_na, w_branch_gla, w_in)
    for l in range(DEPTH):
        x = _layer(x, l, late_stacked, *(p[l] for p in params))
    return x
```

```python
import functools

import jax
import jax.numpy as jnp
import numpy as np
from jax import lax
from jax.experimental import pallas as pl
from jax.experimental.pallas import tpu as pltpu

F32 = jnp.float32
BF16 = jnp.bfloat16

D_MODEL = 1024
D_FF = 2816
GRID_W = 64
NA_HEADS = 8
NA_HEAD_DIM = 64
NA_WIDTH = NA_HEADS * NA_HEAD_DIM
NA_WIN_H = 8
NA_WIN_W = 16
GLA_HEADS = 4
GLA_DK = 64
GLA_DV = 128
GLA_KEY_WIDTH = GLA_HEADS * GLA_DK
GLA_VAL_WIDTH = GLA_HEADS * GLA_DV
GLA_GATE_RANK = 16
GLA_GATE_TAU = 16.0
GLA_CHUNK = 64
DEPTH = 1
ALPHA = (2 * DEPTH) ** 0.25
LN_EPS = 1e-5
RMS_EPS = 1e-6

OFF_NA_END = 3 * NA_WIDTH
OFF_GQK_END = OFF_NA_END + 2 * GLA_KEY_WIDTH
OFF_GVG_END = OFF_GQK_END + 2 * GLA_VAL_WIDTH
OFF_GDL_END = OFF_GVG_END + 2 * GLA_GATE_RANK
N_IN = OFF_GDL_END + 2 * D_MODEL

LANES = 128
BF16_SUBLANES = 16
F32_TOP16_MASK = -65536
N_LATE_WEIGHTS = 6
VMEM_LIMIT_BYTES = 56 * 1024 * 1024

NA_RQ = 4
NA_BAND = NA_RQ + NA_WIN_H
NA_QT = NA_RQ * GRID_W
NA_KT = NA_BAND * GRID_W
NA_UNROLL = 16
NA_ONES_ROWS = 16
NEG_BIG = -1e30
LOG2E = 1.4426950408889634
NA_Q_SCALE = LOG2E * NA_HEAD_DIM ** -0.5

GLA_GROUP = 4
GLA_BLOCK = 1024
GLA_SEQS = 2
TOKEN_TILE = 512
SUB_TILES = 2
SUB_TILE_ROWS = TOKEN_TILE // SUB_TILES
MERGE_TILE = 1024
MERGE_VMEM_LIMIT_BYTES = 60 * 1024 * 1024


def _const_spec(shape):
    nd = len(shape)
    return pl.BlockSpec(shape, lambda *_: (0,) * nd, pipeline_mode=pl.Buffered(1))


def _layer_norm(z, g, b):
    mu = jnp.mean(z, axis=-1, keepdims=True)
    zc = z - mu
    var = jnp.mean(zc * zc, axis=-1, keepdims=True)
    return zc * lax.rsqrt(var + LN_EPS) * g + b


def _swiglu(xb, wg_ref, wu_ref, wd_ref):
    g = jnp.dot(xb, wg_ref[...], preferred_element_type=F32)
    u = jnp.dot(xb, wu_ref[...], preferred_element_type=F32)
    h = (g * jax.nn.sigmoid(g)) * u
    return jnp.dot(h.astype(BF16), wd_ref[...], preferred_element_type=F32)


def _ffn_inproj_body(x_ref, wg_ref, wu_ref, wd_ref, g1_ref, b1_ref, win_ref, *refs):
    late_f32, outs = refs[:N_LATE_WEIGHTS], refs[N_LATE_WEIGHTS:]
    x1_ref, nak_ref, naqt_ref, navt_ref, gqk_ref, gvg_ref, gdl_ref = outs[:-N_LATE_WEIGHTS - 1]
    late_bf16, wgates_ref = outs[-N_LATE_WEIGHTS - 1:-1], outs[-1]
    for src, dst in zip(late_f32, late_bf16):
        dst[...] = src[...].astype(BF16)
    gate_rows = wgates_ref.shape[0]
    r0 = jnp.minimum(pl.program_id(0), win_ref.shape[0] // gate_rows - 1) * gate_rows
    wgates_ref[...] = win_ref[pl.ds(pl.multiple_of(r0, gate_rows), gate_rows), OFF_GDL_END:]

    tm = x_ref.shape[0]
    subs = [pl.ds(r, tm // SUB_TILES) for r in range(0, tm, tm // SUB_TILES)]
    zs = []
    for rows in subs:
        x = x_ref[rows, :]
        zs.append(ALPHA * x + 0.5 * _swiglu(x.astype(BF16), wg_ref, wu_ref, wd_ref))
    for g, (rows, z) in enumerate(zip(subs, zs)):
        x1 = _layer_norm(z, g1_ref[...], b1_ref[...])
        x1_ref[rows, :] = x1
        proj = jnp.dot(x1.astype(BF16), win_ref[:, :OFF_GDL_END], preferred_element_type=F32)
        nak_ref[rows, :] = proj[:, NA_WIDTH:2 * NA_WIDTH].astype(BF16)
        naqt_ref[g] = (proj[:, :NA_WIDTH] * NA_Q_SCALE).T.astype(BF16)
        navt_ref[g] = proj[:, 2 * NA_WIDTH:OFF_NA_END].T.astype(BF16)
        gqk_ref[rows, :] = proj[:, OFF_NA_END:OFF_GQK_END].astype(BF16)
        gvg_ref[rows, :] = proj[:, OFF_GQK_END:OFF_GVG_END].astype(BF16)
        gdl_ref[rows, :] = proj[:, OFF_GVG_END:OFF_GDL_END].astype(BF16)


def _row_chunk(nrows, nsteps):
    return next(c for c in range(BF16_SUBLANES, nrows + 1, BF16_SUBLANES) if nrows % c == 0 and c * nsteps >= nrows)


def _ffn_inproj(x2d, wg, wu, wd, g1, b1, win, late_weights, layer):
    t = x2d.shape[0]
    tm = min(TOKEN_TILE, t)
    n = t // tm
    assert tm // SUB_TILES == NA_QT and len(late_weights) == N_LATE_WEIGHTS
    row = lambda w: pl.BlockSpec((tm, w), lambda i: (i, 0))
    widths = (NA_WIDTH, OFF_GQK_END - OFF_NA_END, OFF_GVG_END - OFF_GQK_END, OFF_GDL_END - OFF_GVG_END)
    rows_bf16 = [jax.ShapeDtypeStruct((t, w), BF16) for w in widths]
    late_shapes = [w.shape[1:] for w in late_weights] + [(D_MODEL, N_IN - OFF_GDL_END)]

    def chunk_specs(w_rows, cols):
        c = _row_chunk(w_rows, n)
        last = w_rows // c - 1
        return (pl.BlockSpec((None, c, cols), lambda i: (layer, jnp.minimum(i, last), 0)),
                pl.BlockSpec((c, cols), lambda i: (jnp.minimum(i, last), 0)))

    late_specs = [chunk_specs(*so) for so in late_shapes]
    return pl.pallas_call(
        _ffn_inproj_body,
        grid=(n,),
        in_specs=[row(D_MODEL), _const_spec(wg.shape), _const_spec(wu.shape), _const_spec(wd.shape),
                  _const_spec(g1.shape), _const_spec(b1.shape), _const_spec(win.shape)]
                 + [si for si, _ in late_specs[:-1]],
        out_specs=[row(D_MODEL), row(widths[0])] + [pl.BlockSpec((SUB_TILES, NA_WIDTH, NA_QT), lambda i: (i, 0, 0))] * 2
                  + [row(w) for w in widths[1:]] + [so for _, so in late_specs],
        out_shape=[jax.ShapeDtypeStruct((t, D_MODEL), F32), rows_bf16[0]]
                  + [jax.ShapeDtypeStruct((t // NA_QT, NA_WIDTH, NA_QT), BF16)] * 2 + rows_bf16[1:]
                  + [jax.ShapeDtypeStruct(so, BF16) for so in late_shapes],
        compiler_params=pltpu.CompilerParams(dimension_semantics=("arbitrary",),
                                             vmem_limit_bytes=VMEM_LIMIT_BYTES),
        name="ffn1_ln1_inproj",
    )(x2d, wg, wu, wd, g1, b1, win, *late_weights)


NA_ROW_OFFSETS = 2 * NA_WIN_H - 1


def _na_bias_tiles(rpb):
    kw = NA_WIN_W
    c = np.arange(GRID_W)[None, :]
    kc = np.arange(GRID_W)[:, None]
    c_start = np.clip(c - kw // 2, 0, GRID_W - kw)
    col_ok = (kc >= c_start) & (kc < c_start + kw)
    dc = kc - c + (kw - 1)
    sel_c = ((dc[..., None] == np.arange(2 * kw - 1)) & col_ok[..., None]).astype(np.float32)
    tiles = jnp.einsum('hdw,kcw->hdkc', rpb, sel_c, precision=lax.Precision.HIGHEST)
    tiles = jnp.where(col_ok, tiles * LOG2E, NEG_BIG)
    tiles = jnp.concatenate([tiles, jnp.full_like(tiles[:, :1], NEG_BIG)], axis=1)
    return jnp.concatenate([tiles, tiles], axis=-1).astype(F32)


def _na_row_offsets(rows):
    kh = NA_WIN_H
    ri = np.arange(NA_RQ)[:, None]
    bj = np.arange(NA_BAND)[None, :]
    out = []
    for r0, b0 in ((0, 0), (NA_RQ, 0), (rows - NA_RQ, rows - NA_BAND)):
        r = r0 + ri
        kr = b0 + bj
        r_start = np.clip(r - kh // 2, 0, rows - kh)
        row_ok = (kr >= r_start) & (kr < r_start + kh)
        out.append(np.where(row_ok, kr - r + (kh - 1), NA_ROW_OFFSETS).tolist())
    return out


def _na_body(qt_ref, k_ref, vt_ref, tile_ref, o_ref, s_ref, bias_ref, *, rows):
    nblk = rows // NA_RQ
    lane = lax.broadcasted_iota(jnp.int32, (1, LANES), 1)
    first_head = lane < NA_HEAD_DIM
    first_head_row = lax.broadcasted_iota(jnp.int32, (LANES, 1), 0) < NA_HEAD_DIM

    @pl.when(pl.program_id(1) == 0)
    def _():
        for kind, offsets in enumerate(_na_row_offsets(rows)):
            for hh in range(2):
                for bj in range(NA_BAND):
                    for ri in range(0, NA_RQ, 2):
                        tile = jnp.where(first_head, tile_ref[hh, offsets[ri][bj]], tile_ref[hh, offsets[ri + 1][bj]])
                        bias_ref[kind, pl.ds(bj * GRID_W, GRID_W), pl.ds(hh * NA_QT + ri * GRID_W, LANES)] = tile

    def band_row(i):
        return jnp.clip(NA_RQ * i - NA_WIN_H // 2, 0, rows - NA_BAND)

    def scores(i, slot):
        q0 = pl.multiple_of(i * NA_QT, NA_QT)
        k0 = pl.multiple_of(band_row(i) * GRID_W, NA_QT)
        kind = jnp.where(i == 0, 0, jnp.where(i == nblk - 1, 2, 1))
        qt = qt_ref[i]
        k = k_ref[0, pl.ds(k0, NA_KT), :]
        zero = jnp.zeros((), BF16)
        qt2 = jnp.concatenate([jnp.where(first_head_row, qt, zero), jnp.where(first_head_row, zero, qt)], axis=1)
        s = jnp.dot(k, qt2, preferred_element_type=F32)
        s_ref[slot] = s + bias_ref[kind]

    def finish(i, slot):
        q0 = pl.multiple_of(i * NA_QT, NA_QT)
        g0 = band_row(i) // NA_RQ
        s = s_ref[slot]
        p = jnp.exp2(s - jnp.max(s, axis=0, keepdims=True)).astype(BF16)
        ones = jnp.ones((NA_ONES_ROWS, NA_KT), BF16)
        heads = []
        for hh in range(2):
            vt = jnp.concatenate([vt_ref[g0 + g, hh * NA_HEAD_DIM:(hh + 1) * NA_HEAD_DIM, :]
                                  for g in range(NA_BAND // NA_RQ)], axis=1)
            acc = jnp.dot(jnp.concatenate([vt, ones], axis=0), p[:, hh * NA_QT:(hh + 1) * NA_QT],
                          preferred_element_type=F32)
            heads.append(acc[:NA_HEAD_DIM] * (1.0 / acc[NA_HEAD_DIM:NA_HEAD_DIM + 1]))
        o_ref[0, pl.ds(q0, NA_QT), :] = jnp.concatenate(heads, axis=0).T.astype(BF16)

    scores(0, 0)

    def group(j, carry):
        for u in range(NA_UNROLL):
            i = NA_UNROLL * j + u
            scores(jnp.minimum(i + 1, nblk - 1), (u + 1) % 2)
            finish(i, u % 2)
        return carry

    lax.fori_loop(0, nblk // NA_UNROLL, group, 0)


def _neighborhood_attention(nak, naqt, navt, tiles, rows):
    b, s, _ = nak.shape
    npairs = NA_HEADS // 2
    tok = pl.BlockSpec((1, s, LANES), lambda p, bi: (bi, 0, p))
    slabs = pl.BlockSpec((s // NA_QT, LANES, NA_QT), lambda p, bi: (bi, p, 0))
    return pl.pallas_call(
        functools.partial(_na_body, rows=rows),
        grid=(npairs, b),
        in_specs=[slabs, tok, slabs,
                  pl.BlockSpec((2, NA_ROW_OFFSETS + 1, GRID_W, LANES), lambda p, bi: (p, 0, 0, 0))],
        out_specs=tok,
        out_shape=jax.ShapeDtypeStruct((b, s, NA_WIDTH), BF16),
        scratch_shapes=[pltpu.VMEM((2, NA_KT, 2 * NA_QT), F32), pltpu.VMEM((3, NA_KT, 2 * NA_QT), F32)],
        compiler_params=pltpu.CompilerParams(dimension_semantics=("arbitrary", "arbitrary"),
                                             vmem_limit_bytes=VMEM_LIMIT_BYTES),
        name="neighborhood_attention",
    )(naqt, nak, navt, tiles)


def _split3(x):
    top16 = lambda a: lax.bitcast_convert_type(
        lax.bitcast_convert_type(a, jnp.int32) & jnp.int32(F32_TOP16_MASK), F32)
    hi = top16(x)
    r1 = x - hi
    mid = top16(r1)
    lo = r1 - mid
    return hi.astype(BF16), mid.astype(BF16), lo.astype(BF16)


def _gla_cumsum_matrix(fwd, nchunks):
    G = nchunks * GLA_CHUNK
    ci = lax.broadcasted_iota(jnp.int32, (G, G), 0)
    si = lax.broadcasted_iota(jnp.int32, (G, G), 1)
    in_order = (si <= ci) if fwd else (si >= ci)
    return jnp.where(in_order & ((ci // GLA_CHUNK) == (si // GLA_CHUNK)), 1.0, 0.0).astype(BF16)


def _gla_group(qk_ref, vg_ref, dl_ref, wdec_ref, bdec_ref, tri, states, r0, fwd, nchunks):
    C, KW, H = GLA_CHUNK, GLA_KEY_WIDTH, GLA_HEADS
    G = nchunks * C
    nseq = len(states)
    d = 0 if fwd else 1
    rows = pl.ds(r0, G)
    last = C - 1 if fwd else 0
    chunk = lambda a, u: a[u * C:(u + 1) * C]
    nt = (((1,), (1,)), ((), ()))

    zs = [(jnp.dot(dl_ref[b, rows, :], wdec_ref[d], preferred_element_type=F32) + bdec_ref[d:d + 1, :]) * LOG2E
          for b in range(nseq)]
    log_a = [(jnp.minimum(x, 0.0) - jnp.log2(1.0 + jnp.exp2(-jnp.abs(x)))) * (1.0 / GLA_GATE_TAU) for x in zs]

    bc = [jnp.dot(tri, jnp.concatenate(_split3(la), axis=1), preferred_element_type=F32) for la in log_a]
    b_cum = [(x[:, 0:KW] + x[:, KW:2 * KW]) + x[:, 2 * KW:3 * KW] for x in bc]
    b_last_rows = [[bcb[u * C + last:u * C + last + 1, :] for u in range(nchunks)] for bcb in b_cum]
    b_last = [jnp.concatenate([jnp.broadcast_to(r, (C, KW)) for r in rws], axis=0) for rws in b_last_rows]

    lane_head = lax.broadcasted_iota(jnp.int32, (1, KW), 1) // GLA_DK
    zero = jnp.zeros((), BF16)
    block_diag = lambda a: jnp.concatenate([jnp.where(lane_head == h, a, zero) for h in range(H)], axis=0)
    ci = lax.broadcasted_iota(jnp.int32, (C, C), 0)
    si = lax.broadcasted_iota(jnp.int32, (C, C), 1)
    causal = (si <= ci) if fwd else (si > ci)

    q_bd, k_t, kend_bd, v_heads = [], [], [], []
    for b in range(nseq):
        q = qk_ref[b, rows, 0:KW].astype(F32)
        k = qk_ref[b, rows, KW:2 * KW].astype(F32)
        v = vg_ref[b, rows, 0:GLA_VAL_WIDTH]
        q_t = ((q * (GLA_DK ** -0.5)) * jnp.exp2(b_cum[b])).astype(BF16)
        k_t.append((k * jnp.exp2(-b_cum[b])).astype(BF16))
        k_end = (k * jnp.exp2(b_last[b] - b_cum[b])).astype(BF16)
        q_bd.append([block_diag(chunk(q_t, u)) for u in range(nchunks)])
        kend_bd.append([block_diag(chunk(k_end, u)) for u in range(nchunks)])
        v_heads.append([[chunk(v, u)[:, h * GLA_DV:(h + 1) * GLA_DV] for h in range(H)] for u in range(nchunks)])

    units = [(b, u) for b in range(nseq) for u in range(nchunks)]
    attn = {bu: lax.dot_general(q_bd[bu[0]][bu[1]], chunk(k_t[bu[0]], bu[1]), nt, preferred_element_type=F32)
            for bu in units}
    attn = {bu: jnp.concatenate([jnp.where(causal, chunk(a, h), 0.0) for h in range(H)], axis=0).astype(BF16)
            for bu, a in attn.items()}
    st_add = {(b, u): lax.dot_general(jnp.concatenate(v_heads[b][u], axis=0), kend_bd[b][u],
                                      (((0,), (0,)), ((), ())), preferred_element_type=F32) for b, u in units}
    o_intra = {(b, u): jnp.concatenate([jnp.dot(chunk(attn[b, u], h), v_heads[b][u][h],
                                                preferred_element_type=F32) for h in range(H)], axis=0)
               for b, u in units}
    outs = [[None] * nchunks for _ in range(nseq)]
    states = list(states)
    for u in (range(nchunks) if fwd else reversed(range(nchunks))):
        for b in range(nseq):
            st = states[b]
            outs[b][u] = o_intra[b, u] + lax.dot_general(q_bd[b][u], st.astype(BF16), nt,
                                                         preferred_element_type=F32)
            states[b] = jnp.exp2(b_last_rows[b][u]) * st + st_add[b, u]
    return outs, states


def _gla_body(qk_ref, vg_ref, dl_ref, wdec_ref, bdec_ref, ng_ref, o_ref, st_ref, ob_ref, *, cbk, nbk):
    C, H = GLA_CHUNK, GLA_HEADS
    nseq = qk_ref.shape[0]
    ph = pl.program_id(1)
    j = pl.program_id(2)
    chunks_per_block = cbk // C
    nchunks = min(GLA_GROUP, chunks_per_block)
    ngroups = chunks_per_block // nchunks

    @pl.when(j == 0)
    def _():
        st_ref[...] = jnp.zeros_like(st_ref)

    def sweep(fwd):
        blk = j if fwd else nbk - 1 - j
        tri = _gla_cumsum_matrix(fwd, nchunks)

        def group(gi, carry):
            g0 = gi if fwd else ngroups - 1 - gi
            r0 = pl.multiple_of(g0 * nchunks * C, nchunks * C)
            outs, states = _gla_group(qk_ref, vg_ref, dl_ref, wdec_ref, bdec_ref, tri,
                                      [st_ref[b] for b in range(nseq)], r0, fwd, nchunks)
            for b in range(nseq):
                st_ref[b] = states[b]
                for u in range(nchunks):
                    cg = blk * chunks_per_block + g0 * nchunks + u
                    if not fwd:
                        ob_ref[b, cg] = outs[b][u]
                        continue
                    c0 = pl.multiple_of(r0 + u * C, C)
                    o = outs[b][u] + ob_ref[b, cg]
                    o = o * lax.rsqrt(jnp.mean(o * o, axis=-1, keepdims=True) + RMS_EPS) * ng_ref[...]
                    gate = vg_ref[b, pl.ds(c0, C), GLA_VAL_WIDTH:2 * GLA_VAL_WIDTH].astype(F32)
                    gate = jnp.concatenate([gate[:, h * GLA_DV:(h + 1) * GLA_DV] for h in range(H)], axis=0)
                    o = o * (gate * jax.nn.sigmoid(gate))
                    o_ref[b, pl.ds(c0, C), :] = jnp.concatenate(
                        [o[h * C:(h + 1) * C] for h in range(H)], axis=1).astype(BF16)
            return carry

        lax.fori_loop(0, ngroups, group, 0, unroll=True)

    @pl.when(ph == 0)
    def _():
        sweep(False)

    @pl.when(ph == 1)
    def _():
        sweep(True)


def _gla(gqk, gvg, gdl, wdec_pad, bdec, norm_g):
    b, s, _ = gqk.shape
    cbk = min(GLA_BLOCK, s)
    nbk = s // cbk
    nseq = GLA_SEQS if b % GLA_SEQS == 0 else 1
    blk = lambda w: pl.BlockSpec((nseq, cbk, w), lambda bi, ph, j: (bi, jnp.where(ph == 0, nbk - 1 - j, j), 0))
    return pl.pallas_call(
        functools.partial(_gla_body, cbk=cbk, nbk=nbk),
        grid=(b // nseq, 2, nbk),
        in_specs=[blk(2 * GLA_KEY_WIDTH), blk(2 * GLA_VAL_WIDTH), blk(2 * GLA_GATE_RANK),
                  _const_spec(wdec_pad.shape), _const_spec(bdec.shape), _const_spec(norm_g.shape)],
        out_specs=pl.BlockSpec((nseq, cbk, GLA_VAL_WIDTH), lambda bi, ph, j: (bi, jnp.where(ph == 0, 0, j), 0)),
        out_shape=jax.ShapeDtypeStruct((b, s, GLA_VAL_WIDTH), BF16),
        scratch_shapes=[pltpu.VMEM((nseq, GLA_DV, GLA_KEY_WIDTH), F32),
                        pltpu.VMEM((nseq, s // GLA_CHUNK, GLA_HEADS * GLA_CHUNK, GLA_DV), F32)],
        compiler_params=pltpu.CompilerParams(dimension_semantics=("arbitrary", "arbitrary", "arbitrary"),
                                             vmem_limit_bytes=VMEM_LIMIT_BYTES),
        name="gla_bidirectional",
    )(gqk, gvg, gdl, wdec_pad, bdec, norm_g)


def _merge_ffn_body(x1_ref, na_ref, gla_ref, wgate_ref, wbn_ref, wbg_ref, wo_ref, g2_ref, b2_ref,
                    wg_ref, wu_ref, wd_ref, g3_ref, b3_ref, o_ref):
    tm = x1_ref.shape[0]
    subs = [pl.ds(r, SUB_TILE_ROWS) for r in range(0, tm, SUB_TILE_ROWS)]
    dot = functools.partial(jnp.dot, preferred_element_type=F32)
    x1s = [x1_ref[rows, :] for rows in subs]
    pre = [(dot(x1.astype(BF16), wgate_ref[...]), dot(na_ref[rows, :], wbn_ref[...]),
            dot(gla_ref[rows, :], wbg_ref[...])) for rows, x1 in zip(subs, x1s)]
    z2s = []
    for x1, (gates, y_na, y_gla) in zip(x1s, pre):
        merged = jax.nn.sigmoid(gates[:, :D_MODEL]) * y_na + jax.nn.sigmoid(gates[:, D_MODEL:]) * y_gla
        z2s.append(ALPHA * x1 + dot(merged.astype(BF16), wo_ref[...]))
    x2s = [_layer_norm(z2, g2_ref[...], b2_ref[...]) for z2 in z2s]
    z3s = [ALPHA * x2 + 0.5 * _swiglu(x2.astype(BF16), wg_ref, wu_ref, wd_ref) for x2 in x2s]
    for rows, z3 in zip(subs, z3s):
        o_ref[rows, :] = _layer_norm(z3, g3_ref[...], b3_ref[...])


def _merge_ffn(x1, na_o, gla_o, wgate, wbn, wbg, wo, g2, b2, wg, wu, wd, g3, b3):
    t = x1.shape[0]
    tm = min(MERGE_TILE, t)
    row = lambda w: pl.BlockSpec((tm, w), lambda i: (i, 0))
    consts = (wgate, wbn, wbg, wo, g2, b2, wg, wu, wd, g3, b3)
    return pl.pallas_call(
        _merge_ffn_body,
        grid=(t // tm,),
        in_specs=[row(D_MODEL), row(NA_WIDTH), row(GLA_VAL_WIDTH)] + [_const_spec(c.shape) for c in consts],
        out_specs=row(D_MODEL),
        out_shape=jax.ShapeDtypeStruct((t, D_MODEL), F32),
        compiler_params=pltpu.CompilerParams(dimension_semantics=("arbitrary",),
                                             vmem_limit_bytes=MERGE_VMEM_LIMIT_BYTES),
        name="merge_ln2_ffn2_ln3",
    )(x1, na_o, gla_o, *consts)


def _layer(x, layer, late_stacked, ffn1_w_gate, ffn1_w_up, ffn1_w_down, ln1_g, ln1_b, w_in, na_rpb, gla_w_dec2,
           gla_b_dec, gla_norm_g, ln2_g, ln2_b, ln3_g, ln3_b):
    b, s, d = x.shape
    assert d == D_MODEL and s % (NA_UNROLL * NA_QT) == 0 and s // GRID_W >= NA_BAND
    rows = s // GRID_W
    bf = lambda w: w.astype(BF16)
    vec = lambda p: p.reshape(1, -1)

    x1, nak, naqt, navt, gqk, gvg, gdl, w2_gate, w2_up, w2_down, wo, wbn, wbg, w_merge_gates = _ffn_inproj(
        x.reshape(b * s, d), bf(ffn1_w_gate), bf(ffn1_w_up), bf(ffn1_w_down), vec(ln1_g), vec(ln1_b), bf(w_in),
        late_stacked, layer)

    na_o = _neighborhood_attention(nak.reshape(b, s, -1), naqt, navt, _na_bias_tiles(na_rpb), rows)

    zr = jnp.zeros((GLA_GATE_RANK, GLA_KEY_WIDTH), gla_w_dec2.dtype)
    wdec_pad = bf(jnp.stack([jnp.concatenate([gla_w_dec2[0], zr]), jnp.concatenate([zr, gla_w_dec2[1]])]))
    gla_o = _gla(gqk.reshape(b, s, -1), gvg.reshape(b, s, -1), gdl.reshape(b, s, -1),
                 wdec_pad, gla_b_dec, vec(gla_norm_g))

    out = _merge_ffn(x1, na_o.reshape(b * s, -1), gla_o.reshape(b * s, -1), w_merge_gates, wbn, wbg, wo,
                     vec(ln2_g), vec(ln2_b), w2_gate, w2_up, w2_down, vec(ln3_g), vec(ln3_b))
    return out.reshape(b, s, d)


def kernel(x, ffn1_w_gate, ffn1_w_up, ffn1_w_down, ln1_g, ln1_b, w_in, na_rpb, gla_w_dec2, gla_b_dec, gla_norm_g,
           w_branch_na, w_branch_gla, w_out, ln2_g, ln2_b, ffn2_w_gate, ffn2_w_up, ffn2_w_down, ln3_g, ln3_b):
    params = (ffn1_w_gate, ffn1_w_up, ffn1_w_down, ln1_g, ln1_b, w_in, na_rpb, gla_w_dec2, gla_b_dec, gla_norm_g,
              ln2_g, ln2_b, ln3_g, ln3_b)
    late_stacked = (ffn2_w_gate, ffn2_w_up, ffn2_w_down, w_out, w_branch_na, w_branch_gla)
    for l in range(DEPTH):
        x = _layer(x, l, late_stacked, *(p[l] for p in params))
    return x
```

```python
import functools

import jax
import jax.numpy as jnp
import numpy as np
from jax import lax
from jax.experimental import pallas as pl
from jax.experimental.pallas import tpu as pltpu

F32 = jnp.float32
BF16 = jnp.bfloat16

D_MODEL = 1024
D_FF = 2816
GRID_W = 64
NA_HEADS = 8
NA_HEAD_DIM = 64
NA_WIDTH = NA_HEADS * NA_HEAD_DIM
NA_WIN_H = 8
NA_WIN_W = 16
GLA_HEADS = 4
GLA_DK = 64
GLA_DV = 128
GLA_KEY_WIDTH = GLA_HEADS * GLA_DK
GLA_VAL_WIDTH = GLA_HEADS * GLA_DV
GLA_GATE_RANK = 16
GLA_GATE_TAU = 16.0
GLA_CHUNK = 64
DEPTH = 1
ALPHA = (2 * DEPTH) ** 0.25
LN_EPS = 1e-5
RMS_EPS = 1e-6

OFF_NA_END = 3 * NA_WIDTH
OFF_GQK_END = OFF_NA_END + 2 * GLA_KEY_WIDTH
OFF_GVG_END = OFF_GQK_END + 2 * GLA_VAL_WIDTH
OFF_GDL_END = OFF_GVG_END + 2 * GLA_GATE_RANK
N_IN = OFF_GDL_END + 2 * D_MODEL

LANES = 128
BF16_SUBLANES = 16
F32_TOP16_MASK = -65536
N_LATE_WEIGHTS = 6
VMEM_LIMIT_BYTES = 56 * 1024 * 1024

NA_RQ = 4
NA_BAND = NA_RQ + NA_WIN_H
NA_QT = NA_RQ * GRID_W
NA_KT = NA_BAND * GRID_W
NA_UNROLL = 16
NA_ONES_ROWS = 16
NEG_BIG = -1e30
LOG2E = 1.4426950408889634
NA_Q_SCALE = LOG2E * NA_HEAD_DIM ** -0.5

GLA_GROUP = 4
GLA_BLOCK = 1024
GLA_SEQS = 2
TOKEN_TILE = 512
SUB_TILES = 2
SUB_TILE_ROWS = TOKEN_TILE // SUB_TILES
MERGE_TILE = 1024
MERGE_VMEM_LIMIT_BYTES = 60 * 1024 * 1024


def _const_spec(shape):
    nd = len(shape)
    return pl.BlockSpec(shape, lambda *_: (0,) * nd, pipeline_mode=pl.Buffered(1))


def _layer_norm(z, g, b):
    mu = jnp.mean(z, axis=-1, keepdims=True)
    zc = z - mu
    var = jnp.mean(zc * zc, axis=-1, keepdims=True)
    return zc * lax.rsqrt(var + LN_EPS) * g + b


def _swiglu(xb, wg_ref, wu_ref, wd_ref):
    g = jnp.dot(xb, wg_ref[...], preferred_element_type=F32)
    u = jnp.dot(xb, wu_ref[...], preferred_element_type=F32)
    h = (g * jax.nn.sigmoid(g)) * u
    return jnp.dot(h.astype(BF16), wd_ref[...], preferred_element_type=F32)


def _ffn_inproj_body(x_ref, wg_ref, wu_ref, wd_ref, g1_ref, b1_ref, win_ref, *refs):
    late_f32, outs = refs[:N_LATE_WEIGHTS], refs[N_LATE_WEIGHTS:]
    x1_ref, nak_ref, naqt_ref, navt_ref, gqk_ref, ggate_ref, gvt_ref, gdl_ref = outs[:-N_LATE_WEIGHTS - 1]
    late_bf16, wgates_ref = outs[-N_LATE_WEIGHTS - 1:-1], outs[-1]
    for src, dst in zip(late_f32, late_bf16):
        dst[...] = src[...].astype(BF16)
    gate_rows = wgates_ref.shape[0]
    r0 = jnp.minimum(pl.program_id(0), win_ref.shape[0] // gate_rows - 1) * gate_rows
    wgates_ref[...] = win_ref[pl.ds(pl.multiple_of(r0, gate_rows), gate_rows), OFF_GDL_END:]

    tm = x_ref.shape[0]
    subs = [pl.ds(r, tm // SUB_TILES) for r in range(0, tm, tm // SUB_TILES)]
    zs = []
    for rows in subs:
        x = x_ref[rows, :]
        zs.append(ALPHA * x + 0.5 * _swiglu(x.astype(BF16), wg_ref, wu_ref, wd_ref))
    for g, (rows, z) in enumerate(zip(subs, zs)):
        x1 = _layer_norm(z, g1_ref[...], b1_ref[...])
        x1_ref[rows, :] = x1
        proj = jnp.dot(x1.astype(BF16), win_ref[:, :OFF_GDL_END], preferred_element_type=F32)
        nak_ref[rows, :] = proj[:, NA_WIDTH:2 * NA_WIDTH].astype(BF16)
        naqt_ref[g] = (proj[:, :NA_WIDTH] * NA_Q_SCALE).T.astype(BF16)
        navt_ref[g] = proj[:, 2 * NA_WIDTH:OFF_NA_END].T.astype(BF16)
        gqk_ref[rows, :] = proj[:, OFF_NA_END:OFF_GQK_END].astype(BF16)
        ggate_ref[rows, :] = proj[:, OFF_GQK_END + GLA_VAL_WIDTH:OFF_GVG_END].astype(BF16)
        gv = proj[:, OFF_GQK_END:OFF_GQK_END + GLA_VAL_WIDTH]
        cps = tm // SUB_TILES // GLA_CHUNK
        for u in range(cps):
            v_stack = jnp.concatenate([gv[u * GLA_CHUNK:(u + 1) * GLA_CHUNK, h * GLA_DV:(h + 1) * GLA_DV]
                                       for h in range(GLA_HEADS)], axis=0)
            gvt_ref[g * cps + u] = v_stack.T.astype(BF16)
        gdl_ref[rows, :] = proj[:, OFF_GVG_END:OFF_GDL_END].astype(BF16)


def _row_chunk(nrows, nsteps):
    return next(c for c in range(BF16_SUBLANES, nrows + 1, BF16_SUBLANES) if nrows % c == 0 and c * nsteps >= nrows)


def _ffn_inproj(x2d, wg, wu, wd, g1, b1, win, late_weights, layer):
    t = x2d.shape[0]
    tm = min(TOKEN_TILE, t)
    n = t // tm
    assert tm // SUB_TILES == NA_QT and len(late_weights) == N_LATE_WEIGHTS
    row = lambda w: pl.BlockSpec((tm, w), lambda i: (i, 0))
    cpt = tm // GLA_CHUNK
    slab = (GLA_DV, GLA_HEADS * GLA_CHUNK)
    late_shapes = [w.shape[1:] for w in late_weights] + [(D_MODEL, N_IN - OFF_GDL_END)]

    def chunk_specs(w_rows, cols):
        c = _row_chunk(w_rows, n)
        last = w_rows // c - 1
        return (pl.BlockSpec((None, c, cols), lambda i: (layer, jnp.minimum(i, last), 0)),
                pl.BlockSpec((c, cols), lambda i: (jnp.minimum(i, last), 0)))

    late_specs = [chunk_specs(*so) for so in late_shapes]
    return pl.pallas_call(
        _ffn_inproj_body,
        grid=(n,),
        in_specs=[row(D_MODEL), _const_spec(wg.shape), _const_spec(wu.shape), _const_spec(wd.shape),
                  _const_spec(g1.shape), _const_spec(b1.shape), _const_spec(win.shape)]
                 + [si for si, _ in late_specs[:-1]],
        out_specs=[row(D_MODEL), row(NA_WIDTH)] + [pl.BlockSpec((SUB_TILES, NA_WIDTH, NA_QT), lambda i: (i, 0, 0))] * 2
                  + [row(2 * GLA_KEY_WIDTH), row(GLA_VAL_WIDTH), pl.BlockSpec((cpt,) + slab, lambda i: (i, 0, 0)),
                     row(2 * GLA_GATE_RANK)] + [so for _, so in late_specs],
        out_shape=[jax.ShapeDtypeStruct((t, D_MODEL), F32), jax.ShapeDtypeStruct((t, NA_WIDTH), BF16)]
                  + [jax.ShapeDtypeStruct((t // NA_QT, NA_WIDTH, NA_QT), BF16)] * 2
                  + [jax.ShapeDtypeStruct((t, 2 * GLA_KEY_WIDTH), BF16), jax.ShapeDtypeStruct((t, GLA_VAL_WIDTH), BF16),
                     jax.ShapeDtypeStruct((t // GLA_CHUNK,) + slab, BF16),
                     jax.ShapeDtypeStruct((t, 2 * GLA_GATE_RANK), BF16)]
                  + [jax.ShapeDtypeStruct(so, BF16) for so in late_shapes],
        compiler_params=pltpu.CompilerParams(dimension_semantics=("arbitrary",),
                                             vmem_limit_bytes=VMEM_LIMIT_BYTES),
        name="ffn1_ln1_inproj",
    )(x2d, wg, wu, wd, g1, b1, win, *late_weights)


NA_ROW_OFFSETS = 2 * NA_WIN_H - 1


def _na_bias_tiles(rpb):
    kw = NA_WIN_W
    c = np.arange(GRID_W)[None, :]
    kc = np.arange(GRID_W)[:, None]
    c_start = np.clip(c - kw // 2, 0, GRID_W - kw)
    col_ok = (kc >= c_start) & (kc < c_start + kw)
    dc = kc - c + (kw - 1)
    sel_c = ((dc[..., None] == np.arange(2 * kw - 1)) & col_ok[..., None]).astype(np.float32)
    tiles = jnp.einsum('hdw,kcw->hdkc', rpb, sel_c, precision=lax.Precision.HIGHEST)
    tiles = jnp.where(col_ok, tiles * LOG2E, NEG_BIG)
    tiles = jnp.concatenate([tiles, jnp.full_like(tiles[:, :1], NEG_BIG)], axis=1)
    return jnp.concatenate([tiles, tiles], axis=-1).astype(F32)


def _na_row_offsets(rows):
    kh = NA_WIN_H
    ri = np.arange(NA_RQ)[:, None]
    bj = np.arange(NA_BAND)[None, :]
    out = []
    for r0, b0 in ((0, 0), (NA_RQ, 0), (rows - NA_RQ, rows - NA_BAND)):
        r = r0 + ri
        kr = b0 + bj
        r_start = np.clip(r - kh // 2, 0, rows - kh)
        row_ok = (kr >= r_start) & (kr < r_start + kh)
        out.append(np.where(row_ok, kr - r + (kh - 1), NA_ROW_OFFSETS).tolist())
    return out


def _na_body(qt_ref, k_ref, vt_ref, tile_ref, o_ref, s_ref, bias_ref, *, rows):
    nblk = rows // NA_RQ
    lane = lax.broadcasted_iota(jnp.int32, (1, LANES), 1)
    first_head = lane < NA_HEAD_DIM
    first_head_row = lax.broadcasted_iota(jnp.int32, (LANES, 1), 0) < NA_HEAD_DIM

    @pl.when(pl.program_id(1) == 0)
    def _():
        for kind, offsets in enumerate(_na_row_offsets(rows)):
            for hh in range(2):
                for bj in range(NA_BAND):
                    for ri in range(0, NA_RQ, 2):
                        tile = jnp.where(first_head, tile_ref[hh, offsets[ri][bj]], tile_ref[hh, offsets[ri + 1][bj]])
                        bias_ref[kind, pl.ds(bj * GRID_W, GRID_W), pl.ds(hh * NA_QT + ri * GRID_W, LANES)] = tile

    def band_row(i):
        return jnp.clip(NA_RQ * i - NA_WIN_H // 2, 0, rows - NA_BAND)

    def scores(i, slot):
        q0 = pl.multiple_of(i * NA_QT, NA_QT)
        k0 = pl.multiple_of(band_row(i) * GRID_W, NA_QT)
        kind = jnp.where(i == 0, 0, jnp.where(i == nblk - 1, 2, 1))
        qt = qt_ref[i]
        k = k_ref[0, pl.ds(k0, NA_KT), :]
        zero = jnp.zeros((), BF16)
        qt2 = jnp.concatenate([jnp.where(first_head_row, qt, zero), jnp.where(first_head_row, zero, qt)], axis=1)
        s = jnp.dot(k, qt2, preferred_element_type=F32)
        s_ref[slot] = s + bias_ref[kind]

    def finish(i, slot):
        q0 = pl.multiple_of(i * NA_QT, NA_QT)
        g0 = band_row(i) // NA_RQ
        s = s_ref[slot]
        p = jnp.exp2(s - jnp.max(s, axis=0, keepdims=True)).astype(BF16)
        ones = jnp.ones((NA_ONES_ROWS, NA_KT), BF16)
        heads = []
        for hh in range(2):
            vt = jnp.concatenate([vt_ref[g0 + g, hh * NA_HEAD_DIM:(hh + 1) * NA_HEAD_DIM, :]
                                  for g in range(NA_BAND // NA_RQ)], axis=1)
            acc = jnp.dot(jnp.concatenate([vt, ones], axis=0), p[:, hh * NA_QT:(hh + 1) * NA_QT],
                          preferred_element_type=F32)
            heads.append(acc[:NA_HEAD_DIM] * (1.0 / acc[NA_HEAD_DIM:NA_HEAD_DIM + 1]))
        o_ref[0, pl.ds(q0, NA_QT), :] = jnp.concatenate(heads, axis=0).T.astype(BF16)

    scores(0, 0)

    def group(j, carry):
        for u in range(NA_UNROLL):
            i = NA_UNROLL * j + u
            scores(jnp.minimum(i + 1, nblk - 1), (u + 1) % 2)
            finish(i, u % 2)
        return carry

    lax.fori_loop(0, nblk // NA_UNROLL, group, 0)


def _neighborhood_attention(nak, naqt, navt, tiles, rows):
    b, s, _ = nak.shape
    npairs = NA_HEADS // 2
    tok = pl.BlockSpec((1, s, LANES), lambda p, bi: (bi, 0, p))
    slabs = pl.BlockSpec((s // NA_QT, LANES, NA_QT), lambda p, bi: (bi, p, 0))
    return pl.pallas_call(
        functools.partial(_na_body, rows=rows),
        grid=(npairs, b),
        in_specs=[slabs, tok, slabs,
                  pl.BlockSpec((2, NA_ROW_OFFSETS + 1, GRID_W, LANES), lambda p, bi: (p, 0, 0, 0))],
        out_specs=tok,
        out_shape=jax.ShapeDtypeStruct((b, s, NA_WIDTH), BF16),
        scratch_shapes=[pltpu.VMEM((2, NA_KT, 2 * NA_QT), F32), pltpu.VMEM((3, NA_KT, 2 * NA_QT), F32)],
        compiler_params=pltpu.CompilerParams(dimension_semantics=("arbitrary", "arbitrary"),
                                             vmem_limit_bytes=VMEM_LIMIT_BYTES),
        name="neighborhood_attention",
    )(naqt, nak, navt, tiles)


def _split3(x):
    top16 = lambda a: lax.bitcast_convert_type(
        lax.bitcast_convert_type(a, jnp.int32) & jnp.int32(F32_TOP16_MASK), F32)
    hi = top16(x)
    r1 = x - hi
    mid = top16(r1)
    lo = r1 - mid
    return hi.astype(BF16), mid.astype(BF16), lo.astype(BF16)


def _gla_cumsum_matrix(fwd, nchunks):
    G = nchunks * GLA_CHUNK
    ci = lax.broadcasted_iota(jnp.int32, (G, G), 0)
    si = lax.broadcasted_iota(jnp.int32, (G, G), 1)
    in_order = (si <= ci) if fwd else (si >= ci)
    return jnp.where(in_order & ((ci // GLA_CHUNK) == (si // GLA_CHUNK)), 1.0, 0.0).astype(BF16)


def _gla_group(qk_ref, vt_ref, dl_ref, wdec_ref, bdec_ref, tri, states, r0, c0, fwd, nchunks):
    C, KW, H = GLA_CHUNK, GLA_KEY_WIDTH, GLA_HEADS
    G = nchunks * C
    nseq = len(states)
    d = 0 if fwd else 1
    rows = pl.ds(r0, G)
    last = C - 1 if fwd else 0
    chunk = lambda a, u: a[u * C:(u + 1) * C]
    nt = (((1,), (1,)), ((), ()))

    zs = [(jnp.dot(dl_ref[b, rows, :], wdec_ref[d], preferred_element_type=F32) + bdec_ref[d:d + 1, :]) * LOG2E
          for b in range(nseq)]
    log_a = [(jnp.minimum(x, 0.0) - jnp.log2(1.0 + jnp.exp2(-jnp.abs(x)))) * (1.0 / GLA_GATE_TAU) for x in zs]

    bc = [jnp.dot(tri, jnp.concatenate(_split3(la), axis=1), preferred_element_type=F32) for la in log_a]
    b_cum = [(x[:, 0:KW] + x[:, KW:2 * KW]) + x[:, 2 * KW:3 * KW] for x in bc]
    b_last_rows = [[bcb[u * C + last:u * C + last + 1, :] for u in range(nchunks)] for bcb in b_cum]
    b_last = [jnp.concatenate([jnp.broadcast_to(r, (C, KW)) for r in rws], axis=0) for rws in b_last_rows]

    lane_head = lax.broadcasted_iota(jnp.int32, (1, KW), 1) // GLA_DK
    zero = jnp.zeros((), BF16)
    block_diag = lambda a: jnp.concatenate([jnp.where(lane_head == h, a, zero) for h in range(H)], axis=0)
    key = lax.broadcasted_iota(jnp.int32, (C, H * C), 0)
    qry = lax.broadcasted_iota(jnp.int32, (C, H * C), 1) % C
    causal_t = (key <= qry) if fwd else (key > qry)

    q_bd, k_t, kend_bd, v_t = [], [], [], []
    for b in range(nseq):
        q = qk_ref[b, rows, 0:KW].astype(F32)
        k = qk_ref[b, rows, KW:2 * KW].astype(F32)
        q_t = ((q * (GLA_DK ** -0.5)) * jnp.exp2(b_cum[b])).astype(BF16)
        k_t.append((k * jnp.exp2(-b_cum[b])).astype(BF16))
        k_end = (k * jnp.exp2(b_last[b] - b_cum[b])).astype(BF16)
        q_bd.append([block_diag(chunk(q_t, u)) for u in range(nchunks)])
        kend_bd.append([block_diag(chunk(k_end, u)) for u in range(nchunks)])
        v_t.append([vt_ref[b, c0 + u] for u in range(nchunks)])

    units = [(b, u) for b in range(nseq) for u in range(nchunks)]
    attn_t = {(b, u): lax.dot_general(chunk(k_t[b], u), q_bd[b][u], nt, preferred_element_type=F32)
              for b, u in units}
    attn_bd = {bu: block_diag(jnp.where(causal_t, a, 0.0).astype(BF16)) for bu, a in attn_t.items()}
    st_add = {(b, u): jnp.dot(v_t[b][u], kend_bd[b][u], preferred_element_type=F32) for b, u in units}
    o_intra = {(b, u): jnp.dot(v_t[b][u], attn_bd[b, u], preferred_element_type=F32) for b, u in units}
    outs = [[None] * nchunks for _ in range(nseq)]
    states = list(states)
    for u in (range(nchunks) if fwd else reversed(range(nchunks))):
        for b in range(nseq):
            st = states[b]
            o_t = o_intra[b, u] + lax.dot_general(st.astype(BF16), q_bd[b][u], nt, preferred_element_type=F32)
            outs[b][u] = o_t.T
            states[b] = jnp.exp2(b_last_rows[b][u]) * st + st_add[b, u]
    return outs, states


def _gla_body(qk_ref, gate_ref, vt_ref, dl_ref, wdec_ref, bdec_ref, ng_ref, o_ref, st_ref, ob_ref, *, cbk, nbk):
    C, H = GLA_CHUNK, GLA_HEADS
    nseq = qk_ref.shape[0]
    ph = pl.program_id(1)
    j = pl.program_id(2)
    chunks_per_block = cbk // C
    nchunks = min(GLA_GROUP, chunks_per_block)
    ngroups = chunks_per_block // nchunks

    @pl.when(j == 0)
    def _():
        st_ref[...] = jnp.zeros_like(st_ref)

    def sweep(fwd):
        blk = j if fwd else nbk - 1 - j
        tri = _gla_cumsum_matrix(fwd, nchunks)

        def group(gi, carry):
            g0 = gi if fwd else ngroups - 1 - gi
            r0 = pl.multiple_of(g0 * nchunks * C, nchunks * C)
            outs, states = _gla_group(qk_ref, vt_ref, dl_ref, wdec_ref, bdec_ref, tri,
                                      [st_ref[b] for b in range(nseq)], r0, g0 * nchunks, fwd, nchunks)
            for b in range(nseq):
                st_ref[b] = states[b]
                for u in range(nchunks):
                    cg = blk * chunks_per_block + g0 * nchunks + u
                    if not fwd:
                        ob_ref[b, cg] = outs[b][u]
                        continue
                    c0 = pl.multiple_of(r0 + u * C, C)
                    o = outs[b][u] + ob_ref[b, cg]
                    o = o * lax.rsqrt(jnp.mean(o * o, axis=-1, keepdims=True) + RMS_EPS) * ng_ref[...]
                    gate = gate_ref[b, pl.ds(c0, C), :].astype(F32)
                    gate = jnp.concatenate([gate[:, h * GLA_DV:(h + 1) * GLA_DV] for h in range(H)], axis=0)
                    o = o * (gate * jax.nn.sigmoid(gate))
                    o_ref[b, pl.ds(c0, C), :] = jnp.concatenate(
                        [o[h * C:(h + 1) * C] for h in range(H)], axis=1).astype(BF16)
            return carry

        lax.fori_loop(0, ngroups, group, 0, unroll=True)

    @pl.when(ph == 0)
    def _():
        sweep(False)

    @pl.when(ph == 1)
    def _():
        sweep(True)


def _gla(gqk, ggate, gvt, gdl, wdec_pad, bdec, norm_g):
    b, s, _ = gqk.shape
    cbk = min(GLA_BLOCK, s)
    nbk = s // cbk
    nseq = GLA_SEQS if b % GLA_SEQS == 0 else 1
    blk = lambda w: pl.BlockSpec((nseq, cbk, w), lambda bi, ph, j: (bi, jnp.where(ph == 0, nbk - 1 - j, j), 0))
    return pl.pallas_call(
        functools.partial(_gla_body, cbk=cbk, nbk=nbk),
        grid=(b // nseq, 2, nbk),
        in_specs=[blk(2 * GLA_KEY_WIDTH), blk(GLA_VAL_WIDTH),
                  pl.BlockSpec((nseq, cbk // GLA_CHUNK) + gvt.shape[2:],
                               lambda bi, ph, j: (bi, jnp.where(ph == 0, nbk - 1 - j, j), 0, 0)),
                  blk(2 * GLA_GATE_RANK),
                  _const_spec(wdec_pad.shape), _const_spec(bdec.shape), _const_spec(norm_g.shape)],
        out_specs=pl.BlockSpec((nseq, cbk, GLA_VAL_WIDTH), lambda bi, ph, j: (bi, jnp.where(ph == 0, 0, j), 0)),
        out_shape=jax.ShapeDtypeStruct((b, s, GLA_VAL_WIDTH), BF16),
        scratch_shapes=[pltpu.VMEM((nseq, GLA_DV, GLA_KEY_WIDTH), F32),
                        pltpu.VMEM((nseq, s // GLA_CHUNK, GLA_HEADS * GLA_CHUNK, GLA_DV), F32)],
        compiler_params=pltpu.CompilerParams(dimension_semantics=("arbitrary", "arbitrary", "arbitrary"),
                                             vmem_limit_bytes=VMEM_LIMIT_BYTES),
        name="gla_bidirectional",
    )(gqk, ggate, gvt, gdl, wdec_pad, bdec, norm_g)


def _merge_ffn_body(x1_ref, na_ref, gla_ref, wgate_ref, wbn_ref, wbg_ref, wo_ref, g2_ref, b2_ref,
                    wg_ref, wu_ref, wd_ref, g3_ref, b3_ref, o_ref):
    tm = x1_ref.shape[0]
    subs = [pl.ds(r, SUB_TILE_ROWS) for r in range(0, tm, SUB_TILE_ROWS)]
    dot = functools.partial(jnp.dot, preferred_element_type=F32)
    x1s = [x1_ref[rows, :] for rows in subs]
    pre = [(dot(x1.astype(BF16), wgate_ref[...]), dot(na_ref[rows, :], wbn_ref[...]),
            dot(gla_ref[rows, :], wbg_ref[...])) for rows, x1 in zip(subs, x1s)]
    z2s = []
    for x1, (gates, y_na, y_gla) in zip(x1s, pre):
        merged = jax.nn.sigmoid(gates[:, :D_MODEL]) * y_na + jax.nn.sigmoid(gates[:, D_MODEL:]) * y_gla
        z2s.append(ALPHA * x1 + dot(merged.astype(BF16), wo_ref[...]))
    x2s = [_layer_norm(z2, g2_ref[...], b2_ref[...]) for z2 in z2s]
    z3s = [ALPHA * x2 + 0.5 * _swiglu(x2.astype(BF16), wg_ref, wu_ref, wd_ref) for x2 in x2s]
    for rows, z3 in zip(subs, z3s):
        o_ref[rows, :] = _layer_norm(z3, g3_ref[...], b3_ref[...])


def _merge_ffn(x1, na_o, gla_o, wgate, wbn, wbg, wo, g2, b2, wg, wu, wd, g3, b3):
    t = x1.shape[0]
    tm = min(MERGE_TILE, t)
    row = lambda w: pl.BlockSpec((tm, w), lambda i: (i, 0))
    consts = (wgate, wbn, wbg, wo, g2, b2, wg, wu, wd, g3, b3)
    return pl.pallas_call(
        _merge_ffn_body,
        grid=(t // tm,),
        in_specs=[row(D_MODEL), row(NA_WIDTH), row(GLA_VAL_WIDTH)] + [_const_spec(c.shape) for c in consts],
        out_specs=row(D_MODEL),
        out_shape=jax.ShapeDtypeStruct((t, D_MODEL), F32),
        compiler_params=pltpu.CompilerParams(dimension_semantics=("arbitrary",),
                                             vmem_limit_bytes=MERGE_VMEM_LIMIT_BYTES),
        name="merge_ln2_ffn2_ln3",
    )(x1, na_o, gla_o, *consts)


def _layer(x, layer, late_stacked, ffn1_w_gate, ffn1_w_up, ffn1_w_down, ln1_g, ln1_b, w_in, na_rpb, gla_w_dec2,
           gla_b_dec, gla_norm_g, ln2_g, ln2_b, ln3_g, ln3_b):
    b, s, d = x.shape
    assert d == D_MODEL and s % (NA_UNROLL * NA_QT) == 0 and s // GRID_W >= NA_BAND
    rows = s // GRID_W
    bf = lambda w: w.astype(BF16)
    vec = lambda p: p.reshape(1, -1)

    x1, nak, naqt, navt, gqk, ggate, gvt, gdl, w2_gate, w2_up, w2_down, wo, wbn, wbg, w_merge_gates = _ffn_inproj(
        x.reshape(b * s, d), bf(ffn1_w_gate), bf(ffn1_w_up), bf(ffn1_w_down), vec(ln1_g), vec(ln1_b), bf(w_in),
        late_stacked, layer)

    na_o = _neighborhood_attention(nak.reshape(b, s, -1), naqt, navt, _na_bias_tiles(na_rpb), rows)

    zr = jnp.zeros((GLA_GATE_RANK, GLA_KEY_WIDTH), gla_w_dec2.dtype)
    wdec_pad = bf(jnp.stack([jnp.concatenate([gla_w_dec2[0], zr]), jnp.concatenate([zr, gla_w_dec2[1]])]))
    gla_o = _gla(gqk.reshape(b, s, -1), ggate.reshape(b, s, -1), gvt.reshape((b, s // GLA_CHUNK) + gvt.shape[1:]),
                 gdl.reshape(b, s, -1), wdec_pad, gla_b_dec, vec(gla_norm_g))

    out = _merge_ffn(x1, na_o.reshape(b * s, -1), gla_o.reshape(b * s, -1), w_merge_gates, wbn, wbg, wo,
                     vec(ln2_g), vec(ln2_b), w2_gate, w2_up, w2_down, vec(ln3_g), vec(ln3_b))
    return out.reshape(b, s, d)


def kernel(x, ffn1_w_gate, ffn1_w_up, ffn1_w_down, ln1_g, ln1_b, w_in, na_rpb, gla_w_dec2, gla_b_dec, gla_norm_g,
           w_branch_na, w_branch_gla, w_out, ln2_g, ln2_b, ffn2_w_gate, ffn2_w_up, ffn2_w_down, ln3_g, ln3_b):
    params = (ffn1_w_gate, ffn1_w_up, ffn1_w_down, ln1_g, ln1_b, w_in, na_rpb, gla_w_dec2, gla_b_dec, gla_norm_g,
              ln2_g, ln2_b, ln3_g, ln3_b)
    late_stacked = (ffn2_w_gate, ffn2_w_up, ffn2_w_down, w_out, w_branch_na, w_branch_gla)
    for l in range(DEPTH):
        x = _layer(x, l, late_stacked, *(p[l] for p in params))
    return x
```

```python
import functools

import jax
import jax.numpy as jnp
import numpy as np
from jax import lax
from jax.experimental import pallas as pl
from jax.experimental.pallas import tpu as pltpu

F32 = jnp.float32
BF16 = jnp.bfloat16

D_MODEL = 1024
D_FF = 2816
GRID_W = 64
NA_HEADS = 8
NA_HEAD_DIM = 64
NA_WIDTH = NA_HEADS * NA_HEAD_DIM
NA_WIN_H = 8
NA_WIN_W = 16
GLA_HEADS = 4
GLA_DK = 64
GLA_DV = 128
GLA_KEY_WIDTH = GLA_HEADS * GLA_DK
GLA_VAL_WIDTH = GLA_HEADS * GLA_DV
GLA_GATE_RANK = 16
GLA_GATE_TAU = 16.0
GLA_CHUNK = 64
DEPTH = 1
ALPHA = (2 * DEPTH) ** 0.25
LN_EPS = 1e-5
RMS_EPS = 1e-6

OFF_NA_END = 3 * NA_WIDTH
OFF_GQK_END = OFF_NA_END + 2 * GLA_KEY_WIDTH
OFF_GVG_END = OFF_GQK_END + 2 * GLA_VAL_WIDTH
OFF_GDL_END = OFF_GVG_END + 2 * GLA_GATE_RANK
N_IN = OFF_GDL_END + 2 * D_MODEL

LANES = 128
BF16_SUBLANES = 16
F32_TOP16_MASK = -65536
N_LATE_WEIGHTS = 6
VMEM_LIMIT_BYTES = 56 * 1024 * 1024

NA_RQ = 4
NA_BAND = NA_RQ + NA_WIN_H
NA_QT = NA_RQ * GRID_W
NA_KT = NA_BAND * GRID_W
NA_UNROLL = 16
NA_ONES_ROWS = 16
NEG_BIG = -1e30
LOG2E = 1.4426950408889634
NA_Q_SCALE = LOG2E * NA_HEAD_DIM ** -0.5

GLA_GROUP = 8
GLA_CUMSUM_CHUNKS = 4
GLA_BLOCK = 1024
GLA_SEQS = 2
TOKEN_TILE = 512
SUB_TILES = 2
SUB_TILE_ROWS = TOKEN_TILE // SUB_TILES
MERGE_TILE = 1024
MERGE_VMEM_LIMIT_BYTES = 60 * 1024 * 1024


def _const_spec(shape):
    nd = len(shape)
    return pl.BlockSpec(shape, lambda *_: (0,) * nd, pipeline_mode=pl.Buffered(1))


def _layer_norm(z, g, b):
    mu = jnp.mean(z, axis=-1, keepdims=True)
    zc = z - mu
    var = jnp.mean(zc * zc, axis=-1, keepdims=True)
    return zc * lax.rsqrt(var + LN_EPS) * g + b


def _swiglu(xb, wg_ref, wu_ref, wd_ref):
    g = jnp.dot(xb, wg_ref[...], preferred_element_type=F32)
    u = jnp.dot(xb, wu_ref[...], preferred_element_type=F32)
    h = (g * jax.nn.sigmoid(g)) * u
    return jnp.dot(h.astype(BF16), wd_ref[...], preferred_element_type=F32)


def _ffn_inproj_body(x_ref, wg_ref, wu_ref, wd_ref, g1_ref, b1_ref, win_ref, *refs):
    late_f32, outs = refs[:N_LATE_WEIGHTS], refs[N_LATE_WEIGHTS:]
    x1_ref, nak_ref, naqt_ref, navt_ref, gqk_ref, gvg_ref, gdl_ref = outs[:-N_LATE_WEIGHTS - 1]
    late_bf16, wgates_ref = outs[-N_LATE_WEIGHTS - 1:-1], outs[-1]
    for src, dst in zip(late_f32, late_bf16):
        dst[...] = src[...].astype(BF16)
    gate_rows = wgates_ref.shape[0]
    r0 = jnp.minimum(pl.program_id(0), win_ref.shape[0] // gate_rows - 1) * gate_rows
    wgates_ref[...] = win_ref[pl.ds(pl.multiple_of(r0, gate_rows), gate_rows), OFF_GDL_END:]

    tm = x_ref.shape[0]
    subs = [pl.ds(r, tm // SUB_TILES) for r in range(0, tm, tm // SUB_TILES)]
    zs = []
    for rows in subs:
        x = x_ref[rows, :]
        zs.append(ALPHA * x + 0.5 * _swiglu(x.astype(BF16), wg_ref, wu_ref, wd_ref))
    for g, (rows, z) in enumerate(zip(subs, zs)):
        x1 = _layer_norm(z, g1_ref[...], b1_ref[...])
        x1_ref[rows, :] = x1
        proj = jnp.dot(x1.astype(BF16), win_ref[:, :OFF_GDL_END], preferred_element_type=F32)
        nak_ref[rows, :] = proj[:, NA_WIDTH:2 * NA_WIDTH].astype(BF16)
        naqt_ref[g] = (proj[:, :NA_WIDTH] * NA_Q_SCALE).T.astype(BF16)
        navt_ref[g] = proj[:, 2 * NA_WIDTH:OFF_NA_END].T.astype(BF16)
        gqk_ref[rows, :] = proj[:, OFF_NA_END:OFF_GQK_END].astype(BF16)
        gvg_ref[rows, :] = proj[:, OFF_GQK_END:OFF_GVG_END].astype(BF16)
        gdl_ref[rows, :] = proj[:, OFF_GVG_END:OFF_GDL_END].astype(BF16)


def _row_chunk(nrows, nsteps):
    return next(c for c in range(BF16_SUBLANES, nrows + 1, BF16_SUBLANES) if nrows % c == 0 and c * nsteps >= nrows)


def _ffn_inproj(x2d, wg, wu, wd, g1, b1, win, late_weights, layer):
    t = x2d.shape[0]
    tm = min(TOKEN_TILE, t)
    n = t // tm
    assert tm // SUB_TILES == NA_QT and len(late_weights) == N_LATE_WEIGHTS
    row = lambda w: pl.BlockSpec((tm, w), lambda i: (i, 0))
    widths = (NA_WIDTH, OFF_GQK_END - OFF_NA_END, OFF_GVG_END - OFF_GQK_END, OFF_GDL_END - OFF_GVG_END)
    rows_bf16 = [jax.ShapeDtypeStruct((t, w), BF16) for w in widths]
    late_shapes = [w.shape[1:] for w in late_weights] + [(D_MODEL, N_IN - OFF_GDL_END)]

    def chunk_specs(w_rows, cols):
        c = _row_chunk(w_rows, n)
        last = w_rows // c - 1
        return (pl.BlockSpec((None, c, cols), lambda i: (layer, jnp.minimum(i, last), 0)),
                pl.BlockSpec((c, cols), lambda i: (jnp.minimum(i, last), 0)))

    late_specs = [chunk_specs(*so) for so in late_shapes]
    return pl.pallas_call(
        _ffn_inproj_body,
        grid=(n,),
        in_specs=[row(D_MODEL), _const_spec(wg.shape), _const_spec(wu.shape), _const_spec(wd.shape),
                  _const_spec(g1.shape), _const_spec(b1.shape), _const_spec(win.shape)]
                 + [si for si, _ in late_specs[:-1]],
        out_specs=[row(D_MODEL), row(widths[0])] + [pl.BlockSpec((SUB_TILES, NA_WIDTH, NA_QT), lambda i: (i, 0, 0))] * 2
                  + [row(w) for w in widths[1:]] + [so for _, so in late_specs],
        out_shape=[jax.ShapeDtypeStruct((t, D_MODEL), F32), rows_bf16[0]]
                  + [jax.ShapeDtypeStruct((t // NA_QT, NA_WIDTH, NA_QT), BF16)] * 2 + rows_bf16[1:]
                  + [jax.ShapeDtypeStruct(so, BF16) for so in late_shapes],
        compiler_params=pltpu.CompilerParams(dimension_semantics=("arbitrary",),
                                             vmem_limit_bytes=VMEM_LIMIT_BYTES),
        name="ffn1_ln1_inproj",
    )(x2d, wg, wu, wd, g1, b1, win, *late_weights)


NA_ROW_OFFSETS = 2 * NA_WIN_H - 1


def _na_bias_tiles(rpb):
    kw = NA_WIN_W
    c = np.arange(GRID_W)[None, :]
    kc = np.arange(GRID_W)[:, None]
    c_start = np.clip(c - kw // 2, 0, GRID_W - kw)
    col_ok = (kc >= c_start) & (kc < c_start + kw)
    dc = kc - c + (kw - 1)
    sel_c = ((dc[..., None] == np.arange(2 * kw - 1)) & col_ok[..., None]).astype(np.float32)
    tiles = jnp.einsum('hdw,kcw->hdkc', rpb, sel_c, precision=lax.Precision.HIGHEST)
    tiles = jnp.where(col_ok, tiles * LOG2E, NEG_BIG)
    tiles = jnp.concatenate([tiles, jnp.full_like(tiles[:, :1], NEG_BIG)], axis=1)
    return jnp.concatenate([tiles, tiles], axis=-1).astype(F32)


def _na_row_offsets(rows):
    kh = NA_WIN_H
    ri = np.arange(NA_RQ)[:, None]
    bj = np.arange(NA_BAND)[None, :]
    out = []
    for r0, b0 in ((0, 0), (NA_RQ, 0), (rows - NA_RQ, rows - NA_BAND)):
        r = r0 + ri
        kr = b0 + bj
        r_start = np.clip(r - kh // 2, 0, rows - kh)
        row_ok = (kr >= r_start) & (kr < r_start + kh)
        out.append(np.where(row_ok, kr - r + (kh - 1), NA_ROW_OFFSETS).tolist())
    return out


def _na_body(qt_ref, k_ref, vt_ref, tile_ref, o_ref, s_ref, bias_ref, *, rows):
    nblk = rows // NA_RQ
    lane = lax.broadcasted_iota(jnp.int32, (1, LANES), 1)
    first_head = lane < NA_HEAD_DIM
    first_head_row = lax.broadcasted_iota(jnp.int32, (LANES, 1), 0) < NA_HEAD_DIM

    @pl.when(pl.program_id(1) == 0)
    def _():
        for kind, offsets in enumerate(_na_row_offsets(rows)):
            for hh in range(2):
                for bj in range(NA_BAND):
                    for ri in range(0, NA_RQ, 2):
                        tile = jnp.where(first_head, tile_ref[hh, offsets[ri][bj]], tile_ref[hh, offsets[ri + 1][bj]])
                        bias_ref[kind, pl.ds(bj * GRID_W, GRID_W), pl.ds(hh * NA_QT + ri * GRID_W, LANES)] = tile

    def band_row(i):
        return jnp.clip(NA_RQ * i - NA_WIN_H // 2, 0, rows - NA_BAND)

    def scores(i, slot):
        q0 = pl.multiple_of(i * NA_QT, NA_QT)
        k0 = pl.multiple_of(band_row(i) * GRID_W, NA_QT)
        kind = jnp.where(i == 0, 0, jnp.where(i == nblk - 1, 2, 1))
        qt = qt_ref[i]
        k = k_ref[0, pl.ds(k0, NA_KT), :]
        zero = jnp.zeros((), BF16)
        qt2 = jnp.concatenate([jnp.where(first_head_row, qt, zero), jnp.where(first_head_row, zero, qt)], axis=1)
        s = jnp.dot(k, qt2, preferred_element_type=F32)
        s_ref[slot] = s + bias_ref[kind]

    def finish(i, slot):
        q0 = pl.multiple_of(i * NA_QT, NA_QT)
        g0 = band_row(i) // NA_RQ
        s = s_ref[slot]
        p = jnp.exp2(s - jnp.max(s, axis=0, keepdims=True)).astype(BF16)
        ones = jnp.ones((NA_ONES_ROWS, NA_KT), BF16)
        heads = []
        for hh in range(2):
            vt = jnp.concatenate([vt_ref[g0 + g, hh * NA_HEAD_DIM:(hh + 1) * NA_HEAD_DIM, :]
                                  for g in range(NA_BAND // NA_RQ)], axis=1)
            acc = jnp.dot(jnp.concatenate([vt, ones], axis=0), p[:, hh * NA_QT:(hh + 1) * NA_QT],
                          preferred_element_type=F32)
            heads.append(acc[:NA_HEAD_DIM] * (1.0 / acc[NA_HEAD_DIM:NA_HEAD_DIM + 1]))
        o_ref[0, pl.ds(q0, NA_QT), :] = jnp.concatenate(heads, axis=0).T.astype(BF16)

    scores(0, 0)

    def group(j, carry):
        for u in range(NA_UNROLL):
            i = NA_UNROLL * j + u
            scores(jnp.minimum(i + 1, nblk - 1), (u + 1) % 2)
            finish(i, u % 2)
        return carry

    lax.fori_loop(0, nblk // NA_UNROLL, group, 0)


def _neighborhood_attention(nak, naqt, navt, tiles, rows):
    b, s, _ = nak.shape
    npairs = NA_HEADS // 2
    tok = pl.BlockSpec((1, s, LANES), lambda p, bi: (bi, 0, p))
    slabs = pl.BlockSpec((s // NA_QT, LANES, NA_QT), lambda p, bi: (bi, p, 0))
    return pl.pallas_call(
        functools.partial(_na_body, rows=rows),
        grid=(npairs, b),
        in_specs=[slabs, tok, slabs,
                  pl.BlockSpec((2, NA_ROW_OFFSETS + 1, GRID_W, LANES), lambda p, bi: (p, 0, 0, 0))],
        out_specs=tok,
        out_shape=jax.ShapeDtypeStruct((b, s, NA_WIDTH), BF16),
        scratch_shapes=[pltpu.VMEM((2, NA_KT, 2 * NA_QT), F32), pltpu.VMEM((3, NA_KT, 2 * NA_QT), F32)],
        compiler_params=pltpu.CompilerParams(dimension_semantics=("arbitrary", "arbitrary"),
                                             vmem_limit_bytes=VMEM_LIMIT_BYTES),
        name="neighborhood_attention",
    )(naqt, nak, navt, tiles)


def _split3(x):
    top16 = lambda a: lax.bitcast_convert_type(
        lax.bitcast_convert_type(a, jnp.int32) & jnp.int32(F32_TOP16_MASK), F32)
    hi = top16(x)
    r1 = x - hi
    mid = top16(r1)
    lo = r1 - mid
    return hi.astype(BF16), mid.astype(BF16), lo.astype(BF16)


def _gla_cumsum_matrix(fwd, nchunks):
    G = nchunks * GLA_CHUNK
    ci = lax.broadcasted_iota(jnp.int32, (G, G), 0)
    si = lax.broadcasted_iota(jnp.int32, (G, G), 1)
    in_order = (si <= ci) if fwd else (si >= ci)
    return jnp.where(in_order & ((ci // GLA_CHUNK) == (si // GLA_CHUNK)), 1.0, 0.0).astype(BF16)


def _gla_group(qk_ref, vg_ref, dl_ref, wdec_ref, bdec_ref, tri, states, r0, fwd, nchunks):
    C, KW, H = GLA_CHUNK, GLA_KEY_WIDTH, GLA_HEADS
    G = nchunks * C
    nseq = len(states)
    d = 0 if fwd else 1
    rows = pl.ds(r0, G)
    last = C - 1 if fwd else 0
    chunk = lambda a, u: a[u * C:(u + 1) * C]
    nt = (((1,), (1,)), ((), ()))

    zs = [(jnp.dot(dl_ref[b, rows, :], wdec_ref[d], preferred_element_type=F32) + bdec_ref[d:d + 1, :]) * LOG2E
          for b in range(nseq)]
    log_a = [(jnp.minimum(x, 0.0) - jnp.log2(1.0 + jnp.exp2(-jnp.abs(x)))) * (1.0 / GLA_GATE_TAU) for x in zs]

    sub = tri.shape[0]
    bc = [jnp.concatenate([jnp.dot(tri, jnp.concatenate(_split3(la[r:r + sub]), axis=1), preferred_element_type=F32)
                           for r in range(0, G, sub)], axis=0) for la in log_a]
    b_cum = [(x[:, 0:KW] + x[:, KW:2 * KW]) + x[:, 2 * KW:3 * KW] for x in bc]
    b_last_rows = [[bcb[u * C + last:u * C + last + 1, :] for u in range(nchunks)] for bcb in b_cum]
    b_last = [jnp.concatenate([jnp.broadcast_to(r, (C, KW)) for r in rws], axis=0) for rws in b_last_rows]

    lane_head = lax.broadcasted_iota(jnp.int32, (1, KW), 1) // GLA_DK
    zero = jnp.zeros((), BF16)
    block_diag = lambda a: jnp.concatenate([jnp.where(lane_head == h, a, zero) for h in range(H)], axis=0)
    ci = lax.broadcasted_iota(jnp.int32, (C, C), 0)
    si = lax.broadcasted_iota(jnp.int32, (C, C), 1)
    causal = (si <= ci) if fwd else (si > ci)

    q_bd, k_t, kend_bd, v_heads = [], [], [], []
    for b in range(nseq):
        q = qk_ref[b, rows, 0:KW].astype(F32)
        k = qk_ref[b, rows, KW:2 * KW].astype(F32)
        v = vg_ref[b, rows, 0:GLA_VAL_WIDTH]
        q_t = ((q * (GLA_DK ** -0.5)) * jnp.exp2(b_cum[b])).astype(BF16)
        k_t.append((k * jnp.exp2(-b_cum[b])).astype(BF16))
        k_end = (k * jnp.exp2(b_last[b] - b_cum[b])).astype(BF16)
        q_bd.append([block_diag(chunk(q_t, u)) for u in range(nchunks)])
        kend_bd.append([block_diag(chunk(k_end, u)) for u in range(nchunks)])
        v_heads.append([[chunk(v, u)[:, h * GLA_DV:(h + 1) * GLA_DV] for h in range(H)] for u in range(nchunks)])

    units = [(b, u) for b in range(nseq) for u in range(nchunks)]
    attn = {bu: lax.dot_general(q_bd[bu[0]][bu[1]], chunk(k_t[bu[0]], bu[1]), nt, preferred_element_type=F32)
            for bu in units}
    attn = {bu: jnp.concatenate([jnp.where(causal, chunk(a, h), 0.0) for h in range(H)], axis=0).astype(BF16)
            for bu, a in attn.items()}
    st_add = {(b, u): lax.dot_general(jnp.concatenate(v_heads[b][u], axis=0), kend_bd[b][u],
                                      (((0,), (0,)), ((), ())), preferred_element_type=F32) for b, u in units}
    o_intra = {(b, u): jnp.concatenate([jnp.dot(chunk(attn[b, u], h), v_heads[b][u][h],
                                                preferred_element_type=F32) for h in range(H)], axis=0)
               for b, u in units}
    outs = [[None] * nchunks for _ in range(nseq)]
    states = list(states)
    for u in (range(nchunks) if fwd else reversed(range(nchunks))):
        for b in range(nseq):
            st = states[b]
            outs[b][u] = o_intra[b, u] + lax.dot_general(q_bd[b][u], st.astype(BF16), nt,
                                                         preferred_element_type=F32)
            states[b] = jnp.exp2(b_last_rows[b][u]) * st + st_add[b, u]
    return outs, states


def _gla_body(qk_ref, vg_ref, dl_ref, wdec_ref, bdec_ref, ng_ref, o_ref, st_ref, ob_ref, *, cbk, nbk):
    C, H = GLA_CHUNK, GLA_HEADS
    nseq = qk_ref.shape[0]
    ph = pl.program_id(1)
    j = pl.program_id(2)
    chunks_per_block = cbk // C
    nchunks = min(GLA_GROUP, chunks_per_block)
    ngroups = chunks_per_block // nchunks

    @pl.when(j == 0)
    def _():
        st_ref[...] = jnp.zeros_like(st_ref)

    def sweep(fwd):
        blk = j if fwd else nbk - 1 - j
        tri = _gla_cumsum_matrix(fwd, min(nchunks, GLA_CUMSUM_CHUNKS))

        def group(gi, carry):
            g0 = gi if fwd else ngroups - 1 - gi
            r0 = pl.multiple_of(g0 * nchunks * C, nchunks * C)
            outs, states = _gla_group(qk_ref, vg_ref, dl_ref, wdec_ref, bdec_ref, tri,
                                      [st_ref[b] for b in range(nseq)], r0, fwd, nchunks)
            for b in range(nseq):
                st_ref[b] = states[b]
                for u in range(nchunks):
                    cg = blk * chunks_per_block + g0 * nchunks + u
                    if not fwd:
                        ob_ref[b, cg] = outs[b][u]
                        continue
                    c0 = pl.multiple_of(r0 + u * C, C)
                    o = outs[b][u] + ob_ref[b, cg]
                    o = o * lax.rsqrt(jnp.mean(o * o, axis=-1, keepdims=True) + RMS_EPS) * ng_ref[...]
                    gate = vg_ref[b, pl.ds(c0, C), GLA_VAL_WIDTH:2 * GLA_VAL_WIDTH].astype(F32)
                    gate = jnp.concatenate([gate[:, h * GLA_DV:(h + 1) * GLA_DV] for h in range(H)], axis=0)
                    o = o * (gate * jax.nn.sigmoid(gate))
                    o_ref[b, pl.ds(c0, C), :] = jnp.concatenate(
                        [o[h * C:(h + 1) * C] for h in range(H)], axis=1).astype(BF16)
            return carry

        lax.fori_loop(0, ngroups, group, 0, unroll=True)

    @pl.when(ph == 0)
    def _():
        sweep(False)

    @pl.when(ph == 1)
    def _():
        sweep(True)


def _gla(gqk, gvg, gdl, wdec_pad, bdec, norm_g):
    b, s, _ = gqk.shape
    cbk = min(GLA_BLOCK, s)
    nbk = s // cbk
    nseq = GLA_SEQS if b % GLA_SEQS == 0 else 1
    blk = lambda w: pl.BlockSpec((nseq, cbk, w), lambda bi, ph, j: (bi, jnp.where(ph == 0, nbk - 1 - j, j), 0))
    return pl.pallas_call(
        functools.partial(_gla_body, cbk=cbk, nbk=nbk),
        grid=(b // nseq, 2, nbk),
        in_specs=[blk(2 * GLA_KEY_WIDTH), blk(2 * GLA_VAL_WIDTH), blk(2 * GLA_GATE_RANK),
                  _const_spec(wdec_pad.shape), _const_spec(bdec.shape), _const_spec(norm_g.shape)],
        out_specs=pl.BlockSpec((nseq, cbk, GLA_VAL_WIDTH), lambda bi, ph, j: (bi, jnp.where(ph == 0, 0, j), 0)),
        out_shape=jax.ShapeDtypeStruct((b, s, GLA_VAL_WIDTH), BF16),
        scratch_shapes=[pltpu.VMEM((nseq, GLA_DV, GLA_KEY_WIDTH), F32),
                        pltpu.VMEM((nseq, s // GLA_CHUNK, GLA_HEADS * GLA_CHUNK, GLA_DV), F32)],
        compiler_params=pltpu.CompilerParams(dimension_semantics=("arbitrary", "arbitrary", "arbitrary"),
                                             vmem_limit_bytes=VMEM_LIMIT_BYTES),
        name="gla_bidirectional",
    )(gqk, gvg, gdl, wdec_pad, bdec, norm_g)


def _merge_ffn_body(x1_ref, na_ref, gla_ref, wgate_ref, wbn_ref, wbg_ref, wo_ref, g2_ref, b2_ref,
                    wg_ref, wu_ref, wd_ref, g3_ref, b3_ref, o_ref):
    tm = x1_ref.shape[0]
    subs = [pl.ds(r, SUB_TILE_ROWS) for r in range(0, tm, SUB_TILE_ROWS)]
    dot = functools.partial(jnp.dot, preferred_element_type=F32)
    x1s = [x1_ref[rows, :] for rows in subs]
    pre = [(dot(x1.astype(BF16), wgate_ref[...]), dot(na_ref[rows, :], wbn_ref[...]),
            dot(gla_ref[rows, :], wbg_ref[...])) for rows, x1 in zip(subs, x1s)]
    z2s = []
    for x1, (gates, y_na, y_gla) in zip(x1s, pre):
        merged = jax.nn.sigmoid(gates[:, :D_MODEL]) * y_na + jax.nn.sigmoid(gates[:, D_MODEL:]) * y_gla
        z2s.append(ALPHA * x1 + dot(merged.astype(BF16), wo_ref[...]))
    x2s = [_layer_norm(z2, g2_ref[...], b2_ref[...]) for z2 in z2s]
    z3s = [ALPHA * x2 + 0.5 * _swiglu(x2.astype(BF16), wg_ref, wu_ref, wd_ref) for x2 in x2s]
    for rows, z3 in zip(subs, z3s):
        o_ref[rows, :] = _layer_norm(z3, g3_ref[...], b3_ref[...])


def _merge_ffn(x1, na_o, gla_o, wgate, wbn, wbg, wo, g2, b2, wg, wu, wd, g3, b3):
    t = x1.shape[0]
    tm = min(MERGE_TILE, t)
    row = lambda w: pl.BlockSpec((tm, w), lambda i: (i, 0))
    consts = (wgate, wbn, wbg, wo, g2, b2, wg, wu, wd, g3, b3)
    return pl.pallas_call(
        _merge_ffn_body,
        grid=(t // tm,),
        in_specs=[row(D_MODEL), row(NA_WIDTH), row(GLA_VAL_WIDTH)] + [_const_spec(c.shape) for c in consts],
        out_specs=row(D_MODEL),
        out_shape=jax.ShapeDtypeStruct((t, D_MODEL), F32),
        compiler_params=pltpu.CompilerParams(dimension_semantics=("arbitrary",),
                                             vmem_limit_bytes=MERGE_VMEM_LIMIT_BYTES),
        name="merge_ln2_ffn2_ln3",
    )(x1, na_o, gla_o, *consts)


def _layer(x, layer, late_stacked, ffn1_w_gate, ffn1_w_up, ffn1_w_down, ln1_g, ln1_b, w_in, na_rpb, gla_w_dec2,
           gla_b_dec, gla_norm_g, ln2_g, ln2_b, ln3_g, ln3_b):
    b, s, d = x.shape
    assert d == D_MODEL and s % (NA_UNROLL * NA_QT) == 0 and s // GRID_W >= NA_BAND
    rows = s // GRID_W
    bf = lambda w: w.astype(BF16)
    vec = lambda p: p.reshape(1, -1)

    x1, nak, naqt, navt, gqk, gvg, gdl, w2_gate, w2_up, w2_down, wo, wbn, wbg, w_merge_gates = _ffn_inproj(
        x.reshape(b * s, d), bf(ffn1_w_gate), bf(ffn1_w_up), bf(ffn1_w_down), vec(ln1_g), vec(ln1_b), bf(w_in),
        late_stacked, layer)

    na_o = _neighborhood_attention(nak.reshape(b, s, -1), naqt, navt, _na_bias_tiles(na_rpb), rows)

    zr = jnp.zeros((GLA_GATE_RANK, GLA_KEY_WIDTH), gla_w_dec2.dtype)
    wdec_pad = bf(jnp.stack([jnp.concatenate([gla_w_dec2[0], zr]), jnp.concatenate([zr, gla_w_dec2[1]])]))
    gla_o = _gla(gqk.reshape(b, s, -1), gvg.reshape(b, s, -1), gdl.reshape(b, s, -1),
                 wdec_pad, gla_b_dec, vec(gla_norm_g))

    out = _merge_ffn(x1, na_o.reshape(b * s, -1), gla_o.reshape(b * s, -1), w_merge_gates, wbn, wbg, wo,
                     vec(ln2_g), vec(ln2_b), w2_gate, w2_up, w2_down, vec(ln3_g), vec(ln3_b))
    return out.reshape(b, s, d)


def kernel(x, ffn1_w_gate, ffn1_w_up, ffn1_w_down, ln1_g, ln1_b, w_in, na_rpb, gla_w_dec2, gla_b_dec, gla_norm_g,
           w_branch_na, w_branch_gla, w_out, ln2_g, ln2_b, ffn2_w_gate, ffn2_w_up, ffn2_w_down, ln3_g, ln3_b):
    params = (ffn1_w_gate, ffn1_w_up, ffn1_w_down, ln1_g, ln1_b, w_in, na_rpb, gla_w_dec2, gla_b_dec, gla_norm_g,
              ln2_g, ln2_b, ln3_g, ln3_b)
    late_stacked = (ffn2_w_gate, ffn2_w_up, ffn2_w_down, w_out, w_branch_na, w_branch_gla)
    for l in range(DEPTH):
        x = _layer(x, l, late_stacked, *(p[l] for p in params))
    return x
```

```python
import functools

import jax
import jax.numpy as jnp
import numpy as np
from jax import lax
from jax.experimental import pallas as pl
from jax.experimental.pallas import tpu as pltpu

F32 = jnp.float32
BF16 = jnp.bfloat16

D_MODEL = 1024
D_FF = 2816
GRID_W = 64
NA_HEADS = 8
NA_HEAD_DIM = 64
NA_WIDTH = NA_HEADS * NA_HEAD_DIM
NA_WIN_H = 8
NA_WIN_W = 16
GLA_HEADS = 4
GLA_DK = 64
GLA_DV = 128
GLA_KEY_WIDTH = GLA_HEADS * GLA_DK
GLA_VAL_WIDTH = GLA_HEADS * GLA_DV
GLA_GATE_RANK = 16
GLA_GATE_TAU = 16.0
GLA_CHUNK = 64
DEPTH = 1
ALPHA = (2 * DEPTH) ** 0.25
LN_EPS = 1e-5
RMS_EPS = 1e-6

OFF_NA_END = 3 * NA_WIDTH
OFF_GQK_END = OFF_NA_END + 2 * GLA_KEY_WIDTH
OFF_GVG_END = OFF_GQK_END + 2 * GLA_VAL_WIDTH
OFF_GDL_END = OFF_GVG_END + 2 * GLA_GATE_RANK
N_IN = OFF_GDL_END + 2 * D_MODEL

LANES = 128
BF16_SUBLANES = 16
F32_TOP16_MASK = -65536
N_FFN1_SCRATCH = 7
FFN1_STAGE_ELEMS = 128 * D_FF
N_LATE_WEIGHTS = 6
VMEM_LIMIT_BYTES = 56 * 1024 * 1024

NA_RQ = 4
NA_BAND = NA_RQ + NA_WIN_H
NA_QT = NA_RQ * GRID_W
NA_KT = NA_BAND * GRID_W
NA_UNROLL = 16
NA_ONES_ROWS = 16
NEG_BIG = -1e30
LOG2E = 1.4426950408889634
NA_Q_SCALE = LOG2E * NA_HEAD_DIM ** -0.5

GLA_GROUP = 8
GLA_CUMSUM_CHUNKS = 4
GLA_BLOCK = 1024
GLA_SEQS = 2
TOKEN_TILE = 512
SUB_TILES = 2
SUB_TILE_ROWS = TOKEN_TILE // SUB_TILES
MERGE_TILE = 1024
MERGE_VMEM_LIMIT_BYTES = 60 * 1024 * 1024


def _const_spec(shape):
    nd = len(shape)
    return pl.BlockSpec(shape, lambda *_: (0,) * nd, pipeline_mode=pl.Buffered(1))


def _layer_norm(z, g, b):
    mu = jnp.mean(z, axis=-1, keepdims=True)
    zc = z - mu
    var = jnp.mean(zc * zc, axis=-1, keepdims=True)
    return zc * lax.rsqrt(var + LN_EPS) * g + b


def _swiglu(xb, wg_ref, wu_ref, wd_ref):
    g = jnp.dot(xb, wg_ref[...], preferred_element_type=F32)
    u = jnp.dot(xb, wu_ref[...], preferred_element_type=F32)
    h = (g * jax.nn.sigmoid(g)) * u
    return jnp.dot(h.astype(BF16), wd_ref[...], preferred_element_type=F32)


def _ffn_inproj_body(layer, x_ref, wg_hbm, wu_hbm, wd_hbm, g1_ref, b1_ref, win_ref, *refs):
    late_f32, outs = refs[:N_LATE_WEIGHTS], refs[N_LATE_WEIGHTS:-N_FFN1_SCRATCH]
    wg_ref, wu_ref, wd_ref, stage_wide, stage_tall, sem_wide, sem_tall = refs[-N_FFN1_SCRATCH:]
    x1_ref, nak_ref, naqt_ref, navt_ref, gqk_ref, gvg_ref, gdl_ref = outs[:-N_LATE_WEIGHTS - 1]
    late_bf16, wgates_ref = outs[-N_LATE_WEIGHTS - 1:-1], outs[-1]

    @pl.when(pl.program_id(0) == 0)
    def _():
        for sources, stage, sem in (((wg_hbm, wg_ref), (wu_hbm, wu_ref)), stage_wide, sem_wide), \
                                   (((wd_hbm, wd_ref),), stage_tall, sem_tall):
            rows = stage.shape[1]
            chunks = [(src, dst, r) for src, dst in sources for r in range(0, dst.shape[0], rows)]
            copy = lambda n: pltpu.make_async_copy(chunks[n][0].at[layer, pl.ds(chunks[n][2], rows), :],
                                                   stage.at[n % 2], sem.at[n % 2])
            copy(0).start()
            for n, (_, dst, r) in enumerate(chunks):
                if n + 1 < len(chunks):
                    copy(n + 1).start()
                copy(n).wait()
                dst[pl.ds(r, rows), :] = stage[n % 2].astype(BF16)

    for src, dst in zip(late_f32, late_bf16):
        dst[...] = src[...].astype(BF16)
    gate_rows = wgates_ref.shape[0]
    r0 = jnp.minimum(pl.program_id(0), win_ref.shape[0] // gate_rows - 1) * gate_rows
    wgates_ref[...] = win_ref[pl.ds(pl.multiple_of(r0, gate_rows), gate_rows), OFF_GDL_END:]

    tm = x_ref.shape[0]
    subs = [pl.ds(r, tm // SUB_TILES) for r in range(0, tm, tm // SUB_TILES)]
    zs = []
    for rows in subs:
        x = x_ref[rows, :]
        zs.append(ALPHA * x + 0.5 * _swiglu(x.astype(BF16), wg_ref, wu_ref, wd_ref))
    for g, (rows, z) in enumerate(zip(subs, zs)):
        x1 = _layer_norm(z, g1_ref[...], b1_ref[...])
        x1_ref[rows, :] = x1
        proj = jnp.dot(x1.astype(BF16), win_ref[:, :OFF_GDL_END], preferred_element_type=F32)
        nak_ref[rows, :] = proj[:, NA_WIDTH:2 * NA_WIDTH].astype(BF16)
        naqt_ref[g] = (proj[:, :NA_WIDTH] * NA_Q_SCALE).T.astype(BF16)
        navt_ref[g] = proj[:, 2 * NA_WIDTH:OFF_NA_END].T.astype(BF16)
        gqk_ref[rows, :] = proj[:, OFF_NA_END:OFF_GQK_END].astype(BF16)
        gvg_ref[rows, :] = proj[:, OFF_GQK_END:OFF_GVG_END].astype(BF16)
        gdl_ref[rows, :] = proj[:, OFF_GVG_END:OFF_GDL_END].astype(BF16)


def _row_chunk(nrows, nsteps):
    return next(c for c in range(BF16_SUBLANES, nrows + 1, BF16_SUBLANES) if nrows % c == 0 and c * nsteps >= nrows)


def _ffn_inproj(x2d, wg, wu, wd, g1, b1, win, late_weights, layer):
    t = x2d.shape[0]
    tm = min(TOKEN_TILE, t)
    n = t // tm
    assert tm // SUB_TILES == NA_QT and len(late_weights) == N_LATE_WEIGHTS
    row = lambda w: pl.BlockSpec((tm, w), lambda i: (i, 0))
    widths = (NA_WIDTH, OFF_GQK_END - OFF_NA_END, OFF_GVG_END - OFF_GQK_END, OFF_GDL_END - OFF_GVG_END)
    rows_bf16 = [jax.ShapeDtypeStruct((t, w), BF16) for w in widths]
    late_shapes = [w.shape[1:] for w in late_weights] + [(D_MODEL, N_IN - OFF_GDL_END)]

    def chunk_specs(w_rows, cols):
        c = _row_chunk(w_rows, n)
        last = w_rows // c - 1
        return (pl.BlockSpec((None, c, cols), lambda i: (layer, jnp.minimum(i, last), 0)),
                pl.BlockSpec((c, cols), lambda i: (jnp.minimum(i, last), 0)))

    late_specs = [chunk_specs(*so) for so in late_shapes]
    any_spec = pl.BlockSpec(memory_space=pl.ANY)
    stage_bytes_rows = lambda w: FFN1_STAGE_ELEMS // w.shape[2]
    return pl.pallas_call(
        functools.partial(_ffn_inproj_body, layer),
        grid=(n,),
        in_specs=[row(D_MODEL), any_spec, any_spec, any_spec,
                  _const_spec(g1.shape), _const_spec(b1.shape), _const_spec(win.shape)]
                 + [si for si, _ in late_specs[:-1]],
        out_specs=[row(D_MODEL), row(widths[0])] + [pl.BlockSpec((SUB_TILES, NA_WIDTH, NA_QT), lambda i: (i, 0, 0))] * 2
                  + [row(w) for w in widths[1:]] + [so for _, so in late_specs],
        out_shape=[jax.ShapeDtypeStruct((t, D_MODEL), F32), rows_bf16[0]]
                  + [jax.ShapeDtypeStruct((t // NA_QT, NA_WIDTH, NA_QT), BF16)] * 2 + rows_bf16[1:]
                  + [jax.ShapeDtypeStruct(so, BF16) for so in late_shapes],
        scratch_shapes=[pltpu.VMEM(wg.shape[1:], BF16), pltpu.VMEM(wu.shape[1:], BF16), pltpu.VMEM(wd.shape[1:], BF16),
                        pltpu.VMEM((2, stage_bytes_rows(wg), wg.shape[2]), F32),
                        pltpu.VMEM((2, stage_bytes_rows(wd), wd.shape[2]), F32),
                        pltpu.SemaphoreType.DMA((2,)), pltpu.SemaphoreType.DMA((2,))],
        compiler_params=pltpu.CompilerParams(dimension_semantics=("arbitrary",),
                                             vmem_limit_bytes=MERGE_VMEM_LIMIT_BYTES),
        name="ffn1_ln1_inproj",
    )(x2d, wg, wu, wd, g1, b1, win, *late_weights)


NA_ROW_OFFSETS = 2 * NA_WIN_H - 1


def _na_bias_tiles(rpb):
    kw = NA_WIN_W
    c = np.arange(GRID_W)[None, :]
    kc = np.arange(GRID_W)[:, None]
    c_start = np.clip(c - kw // 2, 0, GRID_W - kw)
    col_ok = (kc >= c_start) & (kc < c_start + kw)
    dc = kc - c + (kw - 1)
    sel_c = ((dc[..., None] == np.arange(2 * kw - 1)) & col_ok[..., None]).astype(np.float32)
    tiles = jnp.einsum('hdw,kcw->hdkc', rpb, sel_c, precision=lax.Precision.HIGHEST)
    tiles = jnp.where(col_ok, tiles * LOG2E, NEG_BIG)
    tiles = jnp.concatenate([tiles, jnp.full_like(tiles[:, :1], NEG_BIG)], axis=1)
    return jnp.concatenate([tiles, tiles], axis=-1).astype(F32)


def _na_row_offsets(rows):
    kh = NA_WIN_H
    ri = np.arange(NA_RQ)[:, None]
    bj = np.arange(NA_BAND)[None, :]
    out = []
    for r0, b0 in ((0, 0), (NA_RQ, 0), (rows - NA_RQ, rows - NA_BAND)):
        r = r0 + ri
        kr = b0 + bj
        r_start = np.clip(r - kh // 2, 0, rows - kh)
        row_ok = (kr >= r_start) & (kr < r_start + kh)
        out.append(np.where(row_ok, kr - r + (kh - 1), NA_ROW_OFFSETS).tolist())
    return out


def _na_body(qt_ref, k_ref, vt_ref, tile_ref, o_ref, s_ref, bias_ref, *, rows):
    nblk = rows // NA_RQ
    lane = lax.broadcasted_iota(jnp.int32, (1, LANES), 1)
    first_head = lane < NA_HEAD_DIM
    first_head_row = lax.broadcasted_iota(jnp.int32, (LANES, 1), 0) < NA_HEAD_DIM

    @pl.when(pl.program_id(1) == 0)
    def _():
        for kind, offsets in enumerate(_na_row_offsets(rows)):
            for hh in range(2):
                for bj in range(NA_BAND):
                    for ri in range(0, NA_RQ, 2):
                        tile = jnp.where(first_head, tile_ref[hh, offsets[ri][bj]], tile_ref[hh, offsets[ri + 1][bj]])
                        bias_ref[kind, pl.ds(bj * GRID_W, GRID_W), pl.ds(hh * NA_QT + ri * GRID_W, LANES)] = tile

    def band_row(i):
        return jnp.clip(NA_RQ * i - NA_WIN_H // 2, 0, rows - NA_BAND)

    def scores(i, slot):
        q0 = pl.multiple_of(i * NA_QT, NA_QT)
        k0 = pl.multiple_of(band_row(i) * GRID_W, NA_QT)
        kind = jnp.where(i == 0, 0, jnp.where(i == nblk - 1, 2, 1))
        qt = qt_ref[i]
        k = k_ref[0, pl.ds(k0, NA_KT), :]
        zero = jnp.zeros((), BF16)
        qt2 = jnp.concatenate([jnp.where(first_head_row, qt, zero), jnp.where(first_head_row, zero, qt)], axis=1)
        s = jnp.dot(k, qt2, preferred_element_type=F32)
        s_ref[slot] = s + bias_ref[kind]

    def finish(i, slot):
        q0 = pl.multiple_of(i * NA_QT, NA_QT)
        g0 = band_row(i) // NA_RQ
        s = s_ref[slot]
        p = jnp.exp2(s - jnp.max(s, axis=0, keepdims=True)).astype(BF16)
        ones = jnp.ones((NA_ONES_ROWS, NA_KT), BF16)
        heads = []
        for hh in range(2):
            vt = jnp.concatenate([vt_ref[g0 + g, hh * NA_HEAD_DIM:(hh + 1) * NA_HEAD_DIM, :]
                                  for g in range(NA_BAND // NA_RQ)], axis=1)
            acc = jnp.dot(jnp.concatenate([vt, ones], axis=0), p[:, hh * NA_QT:(hh + 1) * NA_QT],
                          preferred_element_type=F32)
            heads.append(acc[:NA_HEAD_DIM] * (1.0 / acc[NA_HEAD_DIM:NA_HEAD_DIM + 1]))
        o_ref[0, pl.ds(q0, NA_QT), :] = jnp.concatenate(heads, axis=0).T.astype(BF16)

    scores(0, 0)

    def group(j, carry):
        for u in range(NA_UNROLL):
            i = NA_UNROLL * j + u
            scores(jnp.minimum(i + 1, nblk - 1), (u + 1) % 2)
            finish(i, u % 2)
        return carry

    lax.fori_loop(0, nblk // NA_UNROLL, group, 0)


def _neighborhood_attention(nak, naqt, navt, tiles, rows):
    b, s, _ = nak.shape
    npairs = NA_HEADS // 2
    tok = pl.BlockSpec((1, s, LANES), lambda p, bi: (bi, 0, p))
    slabs = pl.BlockSpec((s // NA_QT, LANES, NA_QT), lambda p, bi: (bi, p, 0))
    return pl.pallas_call(
        functools.partial(_na_body, rows=rows),
        grid=(npairs, b),
        in_specs=[slabs, tok, slabs,
                  pl.BlockSpec((2, NA_ROW_OFFSETS + 1, GRID_W, LANES), lambda p, bi: (p, 0, 0, 0))],
        out_specs=tok,
        out_shape=jax.ShapeDtypeStruct((b, s, NA_WIDTH), BF16),
        scratch_shapes=[pltpu.VMEM((2, NA_KT, 2 * NA_QT), F32), pltpu.VMEM((3, NA_KT, 2 * NA_QT), F32)],
        compiler_params=pltpu.CompilerParams(dimension_semantics=("arbitrary", "arbitrary"),
                                             vmem_limit_bytes=VMEM_LIMIT_BYTES),
        name="neighborhood_attention",
    )(naqt, nak, navt, tiles)


def _split3(x):
    top16 = lambda a: lax.bitcast_convert_type(
        lax.bitcast_convert_type(a, jnp.int32) & jnp.int32(F32_TOP16_MASK), F32)
    hi = top16(x)
    r1 = x - hi
    mid = top16(r1)
    lo = r1 - mid
    return hi.astype(BF16), mid.astype(BF16), lo.astype(BF16)


def _gla_cumsum_matrix(fwd, nchunks):
    G = nchunks * GLA_CHUNK
    ci = lax.broadcasted_iota(jnp.int32, (G, G), 0)
    si = lax.broadcasted_iota(jnp.int32, (G, G), 1)
    in_order = (si <= ci) if fwd else (si >= ci)
    return jnp.where(in_order & ((ci // GLA_CHUNK) == (si // GLA_CHUNK)), 1.0, 0.0).astype(BF16)


def _gla_group(qk_ref, vg_ref, dl_ref, wdec_ref, bdec_ref, tri, states, r0, fwd, nchunks):
    C, KW, H = GLA_CHUNK, GLA_KEY_WIDTH, GLA_HEADS
    G = nchunks * C
    nseq = len(states)
    d = 0 if fwd else 1
    rows = pl.ds(r0, G)
    last = C - 1 if fwd else 0
    chunk = lambda a, u: a[u * C:(u + 1) * C]
    nt = (((1,), (1,)), ((), ()))

    zs = [(jnp.dot(dl_ref[b, rows, :], wdec_ref[d], preferred_element_type=F32) + bdec_ref[d:d + 1, :]) * LOG2E
          for b in range(nseq)]
    log_a = [(jnp.minimum(x, 0.0) - jnp.log2(1.0 + jnp.exp2(-jnp.abs(x)))) * (1.0 / GLA_GATE_TAU) for x in zs]

    sub = tri.shape[0]
    bc = [jnp.concatenate([jnp.dot(tri, jnp.concatenate(_split3(la[r:r + sub]), axis=1), preferred_element_type=F32)
                           for r in range(0, G, sub)], axis=0) for la in log_a]
    b_cum = [(x[:, 0:KW] + x[:, KW:2 * KW]) + x[:, 2 * KW:3 * KW] for x in bc]
    b_last_rows = [[bcb[u * C + last:u * C + last + 1, :] for u in range(nchunks)] for bcb in b_cum]
    b_last = [jnp.concatenate([jnp.broadcast_to(r, (C, KW)) for r in rws], axis=0) for rws in b_last_rows]

    lane_head = lax.broadcasted_iota(jnp.int32, (1, KW), 1) // GLA_DK
    zero = jnp.zeros((), BF16)
    block_diag = lambda a: jnp.concatenate([jnp.where(lane_head == h, a, zero) for h in range(H)], axis=0)
    ci = lax.broadcasted_iota(jnp.int32, (C, C), 0)
    si = lax.broadcasted_iota(jnp.int32, (C, C), 1)
    causal = (si <= ci) if fwd else (si > ci)

    q_bd, k_t, kend_bd, v_heads = [], [], [], []
    for b in range(nseq):
        q = qk_ref[b, rows, 0:KW].astype(F32)
        k = qk_ref[b, rows, KW:2 * KW].astype(F32)
        v = vg_ref[b, rows, 0:GLA_VAL_WIDTH]
        q_t = ((q * (GLA_DK ** -0.5)) * jnp.exp2(b_cum[b])).astype(BF16)
        k_t.append((k * jnp.exp2(-b_cum[b])).astype(BF16))
        k_end = (k * jnp.exp2(b_last[b] - b_cum[b])).astype(BF16)
        q_bd.append([block_diag(chunk(q_t, u)) for u in range(nchunks)])
        kend_bd.append([block_diag(chunk(k_end, u)) for u in range(nchunks)])
        v_heads.append([[chunk(v, u)[:, h * GLA_DV:(h + 1) * GLA_DV] for h in range(H)] for u in range(nchunks)])

    units = [(b, u) for b in range(nseq) for u in range(nchunks)]
    attn = {bu: lax.dot_general(q_bd[bu[0]][bu[1]], chunk(k_t[bu[0]], bu[1]), nt, preferred_element_type=F32)
            for bu in units}
    attn = {bu: jnp.concatenate([jnp.where(causal, chunk(a, h), 0.0) for h in range(H)], axis=0).astype(BF16)
            for bu, a in attn.items()}
    st_add = {(b, u): lax.dot_general(jnp.concatenate(v_heads[b][u], axis=0), kend_bd[b][u],
                                      (((0,), (0,)), ((), ())), preferred_element_type=F32) for b, u in units}
    o_intra = {(b, u): jnp.concatenate([jnp.dot(chunk(attn[b, u], h), v_heads[b][u][h],
                                                preferred_element_type=F32) for h in range(H)], axis=0)
               for b, u in units}
    outs = [[None] * nchunks for _ in range(nseq)]
    states = list(states)
    for u in (range(nchunks) if fwd else reversed(range(nchunks))):
        for b in range(nseq):
            st = states[b]
            outs[b][u] = o_intra[b, u] + lax.dot_general(q_bd[b][u], st.astype(BF16), nt,
                                                         preferred_element_type=F32)
            states[b] = jnp.exp2(b_last_rows[b][u]) * st + st_add[b, u]
    return outs, states


def _gla_body(qk_ref, vg_ref, dl_ref, wdec_ref, bdec_ref, ng_ref, o_ref, st_ref, ob_ref, *, cbk, nbk):
    C, H = GLA_CHUNK, GLA_HEADS
    nseq = qk_ref.shape[0]
    ph = pl.program_id(1)
    j = pl.program_id(2)
    chunks_per_block = cbk // C
    nchunks = min(GLA_GROUP, chunks_per_block)
    ngroups = chunks_per_block // nchunks

    @pl.when(j == 0)
    def _():
        st_ref[...] = jnp.zeros_like(st_ref)

    def sweep(fwd):
        blk = j if fwd else nbk - 1 - j
        tri = _gla_cumsum_matrix(fwd, min(nchunks, GLA_CUMSUM_CHUNKS))

        def group(gi, carry):
            g0 = gi if fwd else ngroups - 1 - gi
            r0 = pl.multiple_of(g0 * nchunks * C, nchunks * C)
            outs, states = _gla_group(qk_ref, vg_ref, dl_ref, wdec_ref, bdec_ref, tri,
                                      [st_ref[b] for b in range(nseq)], r0, fwd, nchunks)
            for b in range(nseq):
                st_ref[b] = states[b]
                for u in range(nchunks):
                    cg = blk * chunks_per_block + g0 * nchunks + u
                    if not fwd:
                        ob_ref[b, cg] = outs[b][u]
                        continue
                    c0 = pl.multiple_of(r0 + u * C, C)
                    o = outs[b][u] + ob_ref[b, cg]
                    o = o * lax.rsqrt(jnp.mean(o * o, axis=-1, keepdims=True) + RMS_EPS) * ng_ref[...]
                    gate = vg_ref[b, pl.ds(c0, C), GLA_VAL_WIDTH:2 * GLA_VAL_WIDTH].astype(F32)
                    gate = jnp.concatenate([gate[:, h * GLA_DV:(h + 1) * GLA_DV] for h in range(H)], axis=0)
                    o = o * (gate * jax.nn.sigmoid(gate))
                    o_ref[b, pl.ds(c0, C), :] = jnp.concatenate(
                        [o[h * C:(h + 1) * C] for h in range(H)], axis=1).astype(BF16)
            return carry

        lax.fori_loop(0, ngroups, group, 0, unroll=True)

    @pl.when(ph == 0)
    def _():
        sweep(False)

    @pl.when(ph == 1)
    def _():
        sweep(True)


def _gla(gqk, gvg, gdl, wdec_pad, bdec, norm_g):
    b, s, _ = gqk.shape
    cbk = min(GLA_BLOCK, s)
    nbk = s // cbk
    nseq = GLA_SEQS if b % GLA_SEQS == 0 else 1
    blk = lambda w: pl.BlockSpec((nseq, cbk, w), lambda bi, ph, j: (bi, jnp.where(ph == 0, nbk - 1 - j, j), 0))
    return pl.pallas_call(
        functools.partial(_gla_body, cbk=cbk, nbk=nbk),
        grid=(b // nseq, 2, nbk),
        in_specs=[blk(2 * GLA_KEY_WIDTH), blk(2 * GLA_VAL_WIDTH), blk(2 * GLA_GATE_RANK),
                  _const_spec(wdec_pad.shape), _const_spec(bdec.shape), _const_spec(norm_g.shape)],
        out_specs=pl.BlockSpec((nseq, cbk, GLA_VAL_WIDTH), lambda bi, ph, j: (bi, jnp.where(ph == 0, 0, j), 0)),
        out_shape=jax.ShapeDtypeStruct((b, s, GLA_VAL_WIDTH), BF16),
        scratch_shapes=[pltpu.VMEM((nseq, GLA_DV, GLA_KEY_WIDTH), F32),
                        pltpu.VMEM((nseq, s // GLA_CHUNK, GLA_HEADS * GLA_CHUNK, GLA_DV), F32)],
        compiler_params=pltpu.CompilerParams(dimension_semantics=("arbitrary", "arbitrary", "arbitrary"),
                                             vmem_limit_bytes=VMEM_LIMIT_BYTES),
        name="gla_bidirectional",
    )(gqk, gvg, gdl, wdec_pad, bdec, norm_g)


def _merge_ffn_body(x1_ref, na_ref, gla_ref, wgate_ref, wbn_ref, wbg_ref, wo_ref, g2_ref, b2_ref,
                    wg_ref, wu_ref, wd_ref, g3_ref, b3_ref, o_ref):
    tm = x1_ref.shape[0]
    subs = [pl.ds(r, SUB_TILE_ROWS) for r in range(0, tm, SUB_TILE_ROWS)]
    dot = functools.partial(jnp.dot, preferred_element_type=F32)
    x1s = [x1_ref[rows, :] for rows in subs]
    pre = [(dot(x1.astype(BF16), wgate_ref[...]), dot(na_ref[rows, :], wbn_ref[...]),
            dot(gla_ref[rows, :], wbg_ref[...])) for rows, x1 in zip(subs, x1s)]
    z2s = []
    for x1, (gates, y_na, y_gla) in zip(x1s, pre):
        merged = jax.nn.sigmoid(gates[:, :D_MODEL]) * y_na + jax.nn.sigmoid(gates[:, D_MODEL:]) * y_gla
        z2s.append(ALPHA * x1 + dot(merged.astype(BF16), wo_ref[...]))
    x2s = [_layer_norm(z2, g2_ref[...], b2_ref[...]) for z2 in z2s]
    z3s = [ALPHA * x2 + 0.5 * _swiglu(x2.astype(BF16), wg_ref, wu_ref, wd_ref) for x2 in x2s]
    for rows, z3 in zip(subs, z3s):
        o_ref[rows, :] = _layer_norm(z3, g3_ref[...], b3_ref[...])


def _merge_ffn(x1, na_o, gla_o, wgate, wbn, wbg, wo, g2, b2, wg, wu, wd, g3, b3):
    t = x1.shape[0]
    tm = min(MERGE_TILE, t)
    row = lambda w: pl.BlockSpec((tm, w), lambda i: (i, 0))
    consts = (wgate, wbn, wbg, wo, g2, b2, wg, wu, wd, g3, b3)
    return pl.pallas_call(
        _merge_ffn_body,
        grid=(t // tm,),
        in_specs=[row(D_MODEL), row(NA_WIDTH), row(GLA_VAL_WIDTH)] + [_const_spec(c.shape) for c in consts],
        out_specs=row(D_MODEL),
        out_shape=jax.ShapeDtypeStruct((t, D_MODEL), F32),
        compiler_params=pltpu.CompilerParams(dimension_semantics=("arbitrary",),
                                             vmem_limit_bytes=MERGE_VMEM_LIMIT_BYTES),
        name="merge_ln2_ffn2_ln3",
    )(x1, na_o, gla_o, *consts)


def _layer(x, layer, ffn1_stacked, late_stacked, ln1_g, ln1_b, w_in, na_rpb, gla_w_dec2,
           gla_b_dec, gla_norm_g, ln2_g, ln2_b, ln3_g, ln3_b):
    b, s, d = x.shape
    assert d == D_MODEL and s % (NA_UNROLL * NA_QT) == 0 and s // GRID_W >= NA_BAND
    rows = s // GRID_W
    bf = lambda w: w.astype(BF16)
    vec = lambda p: p.reshape(1, -1)

    x1, nak, naqt, navt, gqk, gvg, gdl, w2_gate, w2_up, w2_down, wo, wbn, wbg, w_merge_gates = _ffn_inproj(
        x.reshape(b * s, d), *ffn1_stacked, vec(ln1_g), vec(ln1_b), bf(w_in), late_stacked, layer)

    na_o = _neighborhood_attention(nak.reshape(b, s, -1), naqt, navt, _na_bias_tiles(na_rpb), rows)

    zr = jnp.zeros((GLA_GATE_RANK, GLA_KEY_WIDTH), gla_w_dec2.dtype)
    wdec_pad = bf(jnp.stack([jnp.concatenate([gla_w_dec2[0], zr]), jnp.concatenate([zr, gla_w_dec2[1]])]))
    gla_o = _gla(gqk.reshape(b, s, -1), gvg.reshape(b, s, -1), gdl.reshape(b, s, -1),
                 wdec_pad, gla_b_dec, vec(gla_norm_g))

    out = _merge_ffn(x1, na_o.reshape(b * s, -1), gla_o.reshape(b * s, -1), w_merge_gates, wbn, wbg, wo,
                     vec(ln2_g), vec(ln2_b), w2_gate, w2_up, w2_down, vec(ln3_g), vec(ln3_b))
    return out.reshape(b, s, d)


def kernel(x, ffn1_w_gate, ffn1_w_up, ffn1_w_down, ln1_g, ln1_b, w_in, na_rpb, gla_w_dec2, gla_b_dec, gla_norm_g,
           w_branch_na, w_branch_gla, w_out, ln2_g, ln2_b, ffn2_w_gate, ffn2_w_up, ffn2_w_down, ln3_g, ln3_b):
    params = (ln1_g, ln1_b, w_in, na_rpb, gla_w_dec2, gla_b_dec, gla_norm_g, ln2_g, ln2_b, ln3_g, ln3_b)
    ffn1_stacked = (ffn1_w_gate, ffn1_w_up, ffn1_w_down)
    late_stacked = (ffn2_w_gate, ffn2_w_up, ffn2_w_down, w_out, w_branch_na, w_branch_gla)
    for l in range(DEPTH):
        x = _layer(x, l, ffn1_stacked, late_stacked, *(p[l] for p in params))
    return x
```

```python
import functools

import jax
import jax.numpy as jnp
import numpy as np
from jax import lax
from jax.experimental import pallas as pl
from jax.experimental.pallas import tpu as pltpu

F32 = jnp.float32
BF16 = jnp.bfloat16

D_MODEL = 1024
D_FF = 2816
GRID_W = 64
NA_HEADS = 8
NA_HEAD_DIM = 64
NA_WIDTH = NA_HEADS * NA_HEAD_DIM
NA_WIN_H = 8
NA_WIN_W = 16
GLA_HEADS = 4
GLA_DK = 64
GLA_DV = 128
GLA_KEY_WIDTH = GLA_HEADS * GLA_DK
GLA_VAL_WIDTH = GLA_HEADS * GLA_DV
GLA_GATE_RANK = 16
GLA_GATE_TAU = 16.0
GLA_CHUNK = 64
DEPTH = 1
ALPHA = (2 * DEPTH) ** 0.25
LN_EPS = 1e-5
RMS_EPS = 1e-6

OFF_NA_END = 3 * NA_WIDTH
OFF_GQK_END = OFF_NA_END + 2 * GLA_KEY_WIDTH
OFF_GVG_END = OFF_GQK_END + 2 * GLA_VAL_WIDTH
OFF_GDL_END = OFF_GVG_END + 2 * GLA_GATE_RANK
N_IN = OFF_GDL_END + 2 * D_MODEL

LANES = 128
BF16_SUBLANES = 16
F32_TOP16_MASK = -65536
N_FFN1_SCRATCH = 7
FFN1_STAGE_ELEMS = 128 * D_FF
N_LATE_WEIGHTS = 6
VMEM_LIMIT_BYTES = 56 * 1024 * 1024

NA_RQ = 4
NA_BAND = NA_RQ + NA_WIN_H
NA_QT = NA_RQ * GRID_W
NA_KT = NA_BAND * GRID_W
NA_UNROLL = 16
NA_ONES_ROWS = 16
NEG_BIG = -1e30
LOG2E = 1.4426950408889634
NA_Q_SCALE = LOG2E * NA_HEAD_DIM ** -0.5

GLA_GROUP = 8
GLA_CUMSUM_CHUNKS = 4
GLA_BLOCK = 1024
GLA_SEQS = 2
TOKEN_TILE = 512
SUB_TILES = 2
SUB_TILE_ROWS = TOKEN_TILE // SUB_TILES
MERGE_TILE = 1024
MERGE_VMEM_LIMIT_BYTES = 60 * 1024 * 1024


def _const_spec(shape):
    nd = len(shape)
    return pl.BlockSpec(shape, lambda *_: (0,) * nd, pipeline_mode=pl.Buffered(1))


def _layer_norm(z, g, b):
    mu = jnp.mean(z, axis=-1, keepdims=True)
    zc = z - mu
    var = jnp.mean(zc * zc, axis=-1, keepdims=True)
    return zc * lax.rsqrt(var + LN_EPS) * g + b


def _swiglu(xb, wg_ref, wu_ref, wd_ref):
    g = jnp.dot(xb, wg_ref[...], preferred_element_type=F32)
    u = jnp.dot(xb, wu_ref[...], preferred_element_type=F32)
    h = (g * jax.nn.sigmoid(g)) * u
    return jnp.dot(h.astype(BF16), wd_ref[...], preferred_element_type=F32)


def _ffn_inproj_body(layer, x_ref, wg_hbm, wu_hbm, wd_hbm, g1_ref, b1_ref, win_ref, *refs):
    late_f32, outs = refs[:N_LATE_WEIGHTS], refs[N_LATE_WEIGHTS:-N_FFN1_SCRATCH]
    wg_ref, wu_ref, wd_ref, stage_wide, stage_tall, sem_wide, sem_tall = refs[-N_FFN1_SCRATCH:]
    x1_ref, nak_ref, naqt_ref, navt_ref, gqk_ref, gvg_ref, gdl_ref = outs[:-N_LATE_WEIGHTS - 1]
    late_bf16, wgates_ref = outs[-N_LATE_WEIGHTS - 1:-1], outs[-1]

    @pl.when(pl.program_id(0) == 0)
    def _():
        streams = []
        for sources, stage, sem in (((wg_hbm, wg_ref), (wu_hbm, wu_ref)), stage_wide, sem_wide), \
                                   (((wd_hbm, wd_ref),), stage_tall, sem_tall):
            rows = stage.shape[1]
            chunks = [(src, dst, r) for src, dst in sources for r in range(0, dst.shape[0], rows)]
            streams.append((chunks, stage, sem, rows))

        def copy(stream, n):
            chunks, stage, sem, rows = streams[stream]
            return pltpu.make_async_copy(chunks[n][0].at[layer, pl.ds(chunks[n][2], rows), :],
                                         stage.at[n % 2], sem.at[n % 2])

        for stream in range(len(streams)):
            for n in range(2):
                copy(stream, n).start(priority=n % 2)
        longest = max(len(chunks) for chunks, *_ in streams)
        for n in range(longest):
            for stream, (chunks, stage, _, rows) in enumerate(streams):
                if n >= len(chunks):
                    continue
                copy(stream, n).wait()
                chunks[n][1][pl.ds(chunks[n][2], rows), :] = stage[n % 2].astype(BF16)
                if n + 2 < len(chunks):
                    copy(stream, n + 2).start(priority=n % 2)

    for src, dst in zip(late_f32, late_bf16):
        dst[...] = src[...].astype(BF16)
    gate_rows = wgates_ref.shape[0]
    r0 = jnp.minimum(pl.program_id(0), win_ref.shape[0] // gate_rows - 1) * gate_rows
    wgates_ref[...] = win_ref[pl.ds(pl.multiple_of(r0, gate_rows), gate_rows), OFF_GDL_END:]

    tm = x_ref.shape[0]
    subs = [pl.ds(r, tm // SUB_TILES) for r in range(0, tm, tm // SUB_TILES)]
    zs = []
    for rows in subs:
        x = x_ref[rows, :]
        zs.append(ALPHA * x + 0.5 * _swiglu(x.astype(BF16), wg_ref, wu_ref, wd_ref))
    for g, (rows, z) in enumerate(zip(subs, zs)):
        x1 = _layer_norm(z, g1_ref[...], b1_ref[...])
        x1_ref[rows, :] = x1
        proj = jnp.dot(x1.astype(BF16), win_ref[:, :OFF_GDL_END], preferred_element_type=F32)
        nak_ref[rows, :] = proj[:, NA_WIDTH:2 * NA_WIDTH].astype(BF16)
        naqt_ref[g] = (proj[:, :NA_WIDTH] * NA_Q_SCALE).T.astype(BF16)
        navt_ref[g] = proj[:, 2 * NA_WIDTH:OFF_NA_END].T.astype(BF16)
        gqk_ref[rows, :] = proj[:, OFF_NA_END:OFF_GQK_END].astype(BF16)
        gvg_ref[rows, :] = proj[:, OFF_GQK_END:OFF_GVG_END].astype(BF16)
        gdl_ref[rows, :] = proj[:, OFF_GVG_END:OFF_GDL_END].astype(BF16)


def _row_chunk(nrows, nsteps):
    return next(c for c in range(BF16_SUBLANES, nrows + 1, BF16_SUBLANES) if nrows % c == 0 and c * nsteps >= nrows)


def _ffn_inproj(x2d, wg, wu, wd, g1, b1, win, late_weights, layer):
    t = x2d.shape[0]
    tm = min(TOKEN_TILE, t)
    n = t // tm
    assert tm // SUB_TILES == NA_QT and len(late_weights) == N_LATE_WEIGHTS
    row = lambda w: pl.BlockSpec((tm, w), lambda i: (i, 0))
    widths = (NA_WIDTH, OFF_GQK_END - OFF_NA_END, OFF_GVG_END - OFF_GQK_END, OFF_GDL_END - OFF_GVG_END)
    rows_bf16 = [jax.ShapeDtypeStruct((t, w), BF16) for w in widths]
    late_shapes = [w.shape[1:] for w in late_weights] + [(D_MODEL, N_IN - OFF_GDL_END)]

    def chunk_specs(w_rows, cols):
        c = _row_chunk(w_rows, n)
        last = w_rows // c - 1
        return (pl.BlockSpec((None, c, cols), lambda i: (layer, jnp.minimum(i, last), 0)),
                pl.BlockSpec((c, cols), lambda i: (jnp.minimum(i, last), 0)))

    late_specs = [chunk_specs(*so) for so in late_shapes]
    any_spec = pl.BlockSpec(memory_space=pl.ANY)
    stage_bytes_rows = lambda w: FFN1_STAGE_ELEMS // w.shape[2]
    return pl.pallas_call(
        functools.partial(_ffn_inproj_body, layer),
        grid=(n,),
        in_specs=[row(D_MODEL), any_spec, any_spec, any_spec,
                  _const_spec(g1.shape), _const_spec(b1.shape), _const_spec(win.shape)]
                 + [si for si, _ in late_specs[:-1]],
        out_specs=[row(D_MODEL), row(widths[0])] + [pl.BlockSpec((SUB_TILES, NA_WIDTH, NA_QT), lambda i: (i, 0, 0))] * 2
                  + [row(w) for w in widths[1:]] + [so for _, so in late_specs],
        out_shape=[jax.ShapeDtypeStruct((t, D_MODEL), F32), rows_bf16[0]]
                  + [jax.ShapeDtypeStruct((t // NA_QT, NA_WIDTH, NA_QT), BF16)] * 2 + rows_bf16[1:]
                  + [jax.ShapeDtypeStruct(so, BF16) for so in late_shapes],
        scratch_shapes=[pltpu.VMEM(wg.shape[1:], BF16), pltpu.VMEM(wu.shape[1:], BF16), pltpu.VMEM(wd.shape[1:], BF16),
                        pltpu.VMEM((2, stage_bytes_rows(wg), wg.shape[2]), F32),
                        pltpu.VMEM((2, stage_bytes_rows(wd), wd.shape[2]), F32),
                        pltpu.SemaphoreType.DMA((2,)), pltpu.SemaphoreType.DMA((2,))],
        compiler_params=pltpu.CompilerParams(dimension_semantics=("arbitrary",),
                                             vmem_limit_bytes=MERGE_VMEM_LIMIT_BYTES),
        name="ffn1_ln1_inproj",
    )(x2d, wg, wu, wd, g1, b1, win, *late_weights)


NA_ROW_OFFSETS = 2 * NA_WIN_H - 1


def _na_bias_tiles(rpb):
    kw = NA_WIN_W
    c = np.arange(GRID_W)[None, :]
    kc = np.arange(GRID_W)[:, None]
    c_start = np.clip(c - kw // 2, 0, GRID_W - kw)
    col_ok = (kc >= c_start) & (kc < c_start + kw)
    dc = kc - c + (kw - 1)
    sel_c = ((dc[..., None] == np.arange(2 * kw - 1)) & col_ok[..., None]).astype(np.float32)
    tiles = jnp.einsum('hdw,kcw->hdkc', rpb, sel_c, precision=lax.Precision.HIGHEST)
    tiles = jnp.where(col_ok, tiles * LOG2E, NEG_BIG)
    tiles = jnp.concatenate([tiles, jnp.full_like(tiles[:, :1], NEG_BIG)], axis=1)
    return jnp.concatenate([tiles, tiles], axis=-1).astype(F32)


def _na_row_offsets(rows):
    kh = NA_WIN_H
    ri = np.arange(NA_RQ)[:, None]
    bj = np.arange(NA_BAND)[None, :]
    out = []
    for r0, b0 in ((0, 0), (NA_RQ, 0), (rows - NA_RQ, rows - NA_BAND)):
        r = r0 + ri
        kr = b0 + bj
        r_start = np.clip(r - kh // 2, 0, rows - kh)
        row_ok = (kr >= r_start) & (kr < r_start + kh)
        out.append(np.where(row_ok, kr - r + (kh - 1), NA_ROW_OFFSETS).tolist())
    return out


def _na_body(qt_ref, k_ref, vt_ref, tile_ref, o_ref, s_ref, bias_ref, *, rows):
    nblk = rows // NA_RQ
    lane = lax.broadcasted_iota(jnp.int32, (1, LANES), 1)
    first_head = lane < NA_HEAD_DIM
    first_head_row = lax.broadcasted_iota(jnp.int32, (LANES, 1), 0) < NA_HEAD_DIM

    @pl.when(pl.program_id(1) == 0)
    def _():
        for kind, offsets in enumerate(_na_row_offsets(rows)):
            for hh in range(2):
                for bj in range(NA_BAND):
                    for ri in range(0, NA_RQ, 2):
                        tile = jnp.where(first_head, tile_ref[hh, offsets[ri][bj]], tile_ref[hh, offsets[ri + 1][bj]])
                        bias_ref[kind, pl.ds(bj * GRID_W, GRID_W), pl.ds(hh * NA_QT + ri * GRID_W, LANES)] = tile

    def band_row(i):
        return jnp.clip(NA_RQ * i - NA_WIN_H // 2, 0, rows - NA_BAND)

    def scores(i, slot):
        q0 = pl.multiple_of(i * NA_QT, NA_QT)
        k0 = pl.multiple_of(band_row(i) * GRID_W, NA_QT)
        kind = jnp.where(i == 0, 0, jnp.where(i == nblk - 1, 2, 1))
        qt = qt_ref[i]
        k = k_ref[0, pl.ds(k0, NA_KT), :]
        zero = jnp.zeros((), BF16)
        qt2 = jnp.concatenate([jnp.where(first_head_row, qt, zero), jnp.where(first_head_row, zero, qt)], axis=1)
        s = jnp.dot(k, qt2, preferred_element_type=F32)
        s_ref[slot] = s + bias_ref[kind]

    def finish(i, slot):
        q0 = pl.multiple_of(i * NA_QT, NA_QT)
        g0 = band_row(i) // NA_RQ
        s = s_ref[slot]
        p = jnp.exp2(s - jnp.max(s, axis=0, keepdims=True)).astype(BF16)
        ones = jnp.ones((NA_ONES_ROWS, NA_KT), BF16)
        heads = []
        for hh in range(2):
            vt = jnp.concatenate([vt_ref[g0 + g, hh * NA_HEAD_DIM:(hh + 1) * NA_HEAD_DIM, :]
                                  for g in range(NA_BAND // NA_RQ)], axis=1)
            acc = jnp.dot(jnp.concatenate([vt, ones], axis=0), p[:, hh * NA_QT:(hh + 1) * NA_QT],
                          preferred_element_type=F32)
            heads.append(acc[:NA_HEAD_DIM] * (1.0 / acc[NA_HEAD_DIM:NA_HEAD_DIM + 1]))
        o_ref[0, pl.ds(q0, NA_QT), :] = jnp.concatenate(heads, axis=0).T.astype(BF16)

    scores(0, 0)

    def group(j, carry):
        for u in range(NA_UNROLL):
            i = NA_UNROLL * j + u
            scores(jnp.minimum(i + 1, nblk - 1), (u + 1) % 2)
            finish(i, u % 2)
        return carry

    lax.fori_loop(0, nblk // NA_UNROLL, group, 0)


def _neighborhood_attention(nak, naqt, navt, tiles, rows):
    b, s, _ = nak.shape
    npairs = NA_HEADS // 2
    tok = pl.BlockSpec((1, s, LANES), lambda p, bi: (bi, 0, p))
    slabs = pl.BlockSpec((s // NA_QT, LANES, NA_QT), lambda p, bi: (bi, p, 0))
    return pl.pallas_call(
        functools.partial(_na_body, rows=rows),
        grid=(npairs, b),
        in_specs=[slabs, tok, slabs,
                  pl.BlockSpec((2, NA_ROW_OFFSETS + 1, GRID_W, LANES), lambda p, bi: (p, 0, 0, 0))],
        out_specs=tok,
        out_shape=jax.ShapeDtypeStruct((b, s, NA_WIDTH), BF16),
        scratch_shapes=[pltpu.VMEM((2, NA_KT, 2 * NA_QT), F32), pltpu.VMEM((3, NA_KT, 2 * NA_QT), F32)],
        compiler_params=pltpu.CompilerParams(dimension_semantics=("arbitrary", "arbitrary"),
                                             vmem_limit_bytes=VMEM_LIMIT_BYTES),
        name="neighborhood_attention",
    )(naqt, nak, navt, tiles)


def _split3(x):
    top16 = lambda a: lax.bitcast_convert_type(
        lax.bitcast_convert_type(a, jnp.int32) & jnp.int32(F32_TOP16_MASK), F32)
    hi = top16(x)
    r1 = x - hi
    mid = top16(r1)
    lo = r1 - mid
    return hi.astype(BF16), mid.astype(BF16), lo.astype(BF16)


def _gla_cumsum_matrix(fwd, nchunks):
    G = nchunks * GLA_CHUNK
    ci = lax.broadcasted_iota(jnp.int32, (G, G), 0)
    si = lax.broadcasted_iota(jnp.int32, (G, G), 1)
    in_order = (si <= ci) if fwd else (si >= ci)
    return jnp.where(in_order & ((ci // GLA_CHUNK) == (si // GLA_CHUNK)), 1.0, 0.0).astype(BF16)


def _gla_group(qk_ref, vg_ref, dl_ref, wdec_ref, bdec_ref, tri, states, r0, fwd, nchunks):
    C, KW, H = GLA_CHUNK, GLA_KEY_WIDTH, GLA_HEADS
    G = nchunks * C
    nseq = len(states)
    d = 0 if fwd else 1
    rows = pl.ds(r0, G)
    last = C - 1 if fwd else 0
    chunk = lambda a, u: a[u * C:(u + 1) * C]
    nt = (((1,), (1,)), ((), ()))

    zs = [(jnp.dot(dl_ref[b, rows, :], wdec_ref[d], preferred_element_type=F32) + bdec_ref[d:d + 1, :]) * LOG2E
          for b in range(nseq)]
    log_a = [(jnp.minimum(x, 0.0) - jnp.log2(1.0 + jnp.exp2(-jnp.abs(x)))) * (1.0 / GLA_GATE_TAU) for x in zs]

    sub = tri.shape[0]
    bc = [jnp.concatenate([jnp.dot(tri, jnp.concatenate(_split3(la[r:r + sub]), axis=1), preferred_element_type=F32)
                           for r in range(0, G, sub)], axis=0) for la in log_a]
    b_cum = [(x[:, 0:KW] + x[:, KW:2 * KW]) + x[:, 2 * KW:3 * KW] for x in bc]
    b_last_rows = [[bcb[u * C + last:u * C + last + 1, :] for u in range(nchunks)] for bcb in b_cum]
    b_last = [jnp.concatenate([jnp.broadcast_to(r, (C, KW)) for r in rws], axis=0) for rws in b_last_rows]

    lane_head = lax.broadcasted_iota(jnp.int32, (1, KW), 1) // GLA_DK
    zero = jnp.zeros((), BF16)
    block_diag = lambda a: jnp.concatenate([jnp.where(lane_head == h, a, zero) for h in range(H)], axis=0)
    ci = lax.broadcasted_iota(jnp.int32, (C, C), 0)
    si = lax.broadcasted_iota(jnp.int32, (C, C), 1)
    causal = (si <= ci) if fwd else (si > ci)

    q_bd, k_t, kend_bd, v_heads = [], [], [], []
    for b in range(nseq):
        q = qk_ref[b, rows, 0:KW].astype(F32)
        k = qk_ref[b, rows, KW:2 * KW].astype(F32)
        v = vg_ref[b, rows, 0:GLA_VAL_WIDTH]
        q_t = ((q * (GLA_DK ** -0.5)) * jnp.exp2(b_cum[b])).astype(BF16)
        k_t.append((k * jnp.exp2(-b_cum[b])).astype(BF16))
        k_end = (k * jnp.exp2(b_last[b] - b_cum[b])).astype(BF16)
        q_bd.append([block_diag(chunk(q_t, u)) for u in range(nchunks)])
        kend_bd.append([block_diag(chunk(k_end, u)) for u in range(nchunks)])
        v_heads.append([[chunk(v, u)[:, h * GLA_DV:(h + 1) * GLA_DV] for h in range(H)] for u in range(nchunks)])

    units = [(b, u) for b in range(nseq) for u in range(nchunks)]
    attn = {bu: lax.dot_general(q_bd[bu[0]][bu[1]], chunk(k_t[bu[0]], bu[1]), nt, preferred_element_type=F32)
            for bu in units}
    attn = {bu: jnp.concatenate([jnp.where(causal, chunk(a, h), 0.0) for h in range(H)], axis=0).astype(BF16)
            for bu, a in attn.items()}
    st_add = {(b, u): lax.dot_general(jnp.concatenate(v_heads[b][u], axis=0), kend_bd[b][u],
                                      (((0,), (0,)), ((), ())), preferred_element_type=F32) for b, u in units}
    o_intra = {(b, u): jnp.concatenate([jnp.dot(chunk(attn[b, u], h), v_heads[b][u][h],
                                                preferred_element_type=F32) for h in range(H)], axis=0)
               for b, u in units}
    outs = [[None] * nchunks for _ in range(nseq)]
    states = list(states)
    for u in (range(nchunks) if fwd else reversed(range(nchunks))):
        for b in range(nseq):
            st = states[b]
            outs[b][u] = o_intra[b, u] + lax.dot_general(q_bd[b][u], st.astype(BF16), nt,
                                                         preferred_element_type=F32)
            states[b] = jnp.exp2(b_last_rows[b][u]) * st + st_add[b, u]
    return outs, states


def _gla_body(qk_ref, vg_ref, dl_ref, wdec_ref, bdec_ref, ng_ref, o_ref, st_ref, ob_ref, *, cbk, nbk):
    C, H = GLA_CHUNK, GLA_HEADS
    nseq = qk_ref.shape[0]
    ph = pl.program_id(1)
    j = pl.program_id(2)
    chunks_per_block = cbk // C
    nchunks = min(GLA_GROUP, chunks_per_block)
    ngroups = chunks_per_block // nchunks

    @pl.when(j == 0)
    def _():
        st_ref[...] = jnp.zeros_like(st_ref)

    def sweep(fwd):
        blk = j if fwd else nbk - 1 - j
        tri = _gla_cumsum_matrix(fwd, min(nchunks, GLA_CUMSUM_CHUNKS))

        def group(gi, carry):
            g0 = gi if fwd else ngroups - 1 - gi
            r0 = pl.multiple_of(g0 * nchunks * C, nchunks * C)
            outs, states = _gla_group(qk_ref, vg_ref, dl_ref, wdec_ref, bdec_ref, tri,
                                      [st_ref[b] for b in range(nseq)], r0, fwd, nchunks)
            for b in range(nseq):
                st_ref[b] = states[b]
                for u in range(nchunks):
                    cg = blk * chunks_per_block + g0 * nchunks + u
                    if not fwd:
                        ob_ref[b, cg] = outs[b][u]
                        continue
                    c0 = pl.multiple_of(r0 + u * C, C)
                    o = outs[b][u] + ob_ref[b, cg]
                    o = o * lax.rsqrt(jnp.mean(o * o, axis=-1, keepdims=True) + RMS_EPS) * ng_ref[...]
                    gate = vg_ref[b, pl.ds(c0, C), GLA_VAL_WIDTH:2 * GLA_VAL_WIDTH].astype(F32)
                    gate = jnp.concatenate([gate[:, h * GLA_DV:(h + 1) * GLA_DV] for h in range(H)], axis=0)
                    o = o * (gate * jax.nn.sigmoid(gate))
                    o_ref[b, pl.ds(c0, C), :] = jnp.concatenate(
                        [o[h * C:(h + 1) * C] for h in range(H)], axis=1).astype(BF16)
            return carry

        lax.fori_loop(0, ngroups, group, 0, unroll=True)

    @pl.when(ph == 0)
    def _():
        sweep(False)

    @pl.when(ph == 1)
    def _():
        sweep(True)


def _gla(gqk, gvg, gdl, wdec_pad, bdec, norm_g):
    b, s, _ = gqk.shape
    cbk = min(GLA_BLOCK, s)
    nbk = s // cbk
    nseq = GLA_SEQS if b % GLA_SEQS == 0 else 1
    blk = lambda w: pl.BlockSpec((nseq, cbk, w), lambda bi, ph, j: (bi, jnp.where(ph == 0, nbk - 1 - j, j), 0))
    return pl.pallas_call(
        functools.partial(_gla_body, cbk=cbk, nbk=nbk),
        grid=(b // nseq, 2, nbk),
        in_specs=[blk(2 * GLA_KEY_WIDTH), blk(2 * GLA_VAL_WIDTH), blk(2 * GLA_GATE_RANK),
                  _const_spec(wdec_pad.shape), _const_spec(bdec.shape), _const_spec(norm_g.shape)],
        out_specs=pl.BlockSpec((nseq, cbk, GLA_VAL_WIDTH), lambda bi, ph, j: (bi, jnp.where(ph == 0, 0, j), 0)),
        out_shape=jax.ShapeDtypeStruct((b, s, GLA_VAL_WIDTH), BF16),
        scratch_shapes=[pltpu.VMEM((nseq, GLA_DV, GLA_KEY_WIDTH), F32),
                        pltpu.VMEM((nseq, s // GLA_CHUNK, GLA_HEADS * GLA_CHUNK, GLA_DV), F32)],
        compiler_params=pltpu.CompilerParams(dimension_semantics=("arbitrary", "arbitrary", "arbitrary"),
                                             vmem_limit_bytes=VMEM_LIMIT_BYTES),
        name="gla_bidirectional",
    )(gqk, gvg, gdl, wdec_pad, bdec, norm_g)


def _merge_ffn_body(x1_ref, na_ref, gla_ref, wgate_ref, wbn_ref, wbg_ref, wo_ref, g2_ref, b2_ref,
                    wg_ref, wu_ref, wd_ref, g3_ref, b3_ref, o_ref):
    tm = x1_ref.shape[0]
    subs = [pl.ds(r, SUB_TILE_ROWS) for r in range(0, tm, SUB_TILE_ROWS)]
    dot = functools.partial(jnp.dot, preferred_element_type=F32)
    x1s = [x1_ref[rows, :] for rows in subs]
    pre = [(dot(x1.astype(BF16), wgate_ref[...]), dot(na_ref[rows, :], wbn_ref[...]),
            dot(gla_ref[rows, :], wbg_ref[...])) for rows, x1 in zip(subs, x1s)]
    z2s = []
    for x1, (gates, y_na, y_gla) in zip(x1s, pre):
        merged = jax.nn.sigmoid(gates[:, :D_MODEL]) * y_na + jax.nn.sigmoid(gates[:, D_MODEL:]) * y_gla
        z2s.append(ALPHA * x1 + dot(merged.astype(BF16), wo_ref[...]))
    x2s = [_layer_norm(z2, g2_ref[...], b2_ref[...]) for z2 in z2s]
    z3s = [ALPHA * x2 + 0.5 * _swiglu(x2.astype(BF16), wg_ref, wu_ref, wd_ref) for x2 in x2s]
    for rows, z3 in zip(subs, z3s):
        o_ref[rows, :] = _layer_norm(z3, g3_ref[...], b3_ref[...])


def _merge_ffn(x1, na_o, gla_o, wgate, wbn, wbg, wo, g2, b2, wg, wu, wd, g3, b3):
    t = x1.shape[0]
    tm = min(MERGE_TILE, t)
    row = lambda w: pl.BlockSpec((tm, w), lambda i: (i, 0))
    consts = (wgate, wbn, wbg, wo, g2, b2, wg, wu, wd, g3, b3)
    return pl.pallas_call(
        _merge_ffn_body,
        grid=(t // tm,),
        in_specs=[row(D_MODEL), row(NA_WIDTH), row(GLA_VAL_WIDTH)] + [_const_spec(c.shape) for c in consts],
        out_specs=row(D_MODEL),
        out_shape=jax.ShapeDtypeStruct((t, D_MODEL), F32),
        compiler_params=pltpu.CompilerParams(dimension_semantics=("arbitrary",),
                                             vmem_limit_bytes=MERGE_VMEM_LIMIT_BYTES),
        name="merge_ln2_ffn2_ln3",
    )(x1, na_o, gla_o, *consts)


def _layer(x, layer, ffn1_stacked, late_stacked, ln1_g, ln1_b, w_in, na_rpb, gla_w_dec2,
           gla_b_dec, gla_norm_g, ln2_g, ln2_b, ln3_g, ln3_b):
    b, s, d = x.shape
    assert d == D_MODEL and s % (NA_UNROLL * NA_QT) == 0 and s // GRID_W >= NA_BAND
    rows = s // GRID_W
    bf = lambda w: w.astype(BF16)
    vec = lambda p: p.reshape(1, -1)

    x1, nak, naqt, navt, gqk, gvg, gdl, w2_gate, w2_up, w2_down, wo, wbn, wbg, w_merge_gates = _ffn_inproj(
        x.reshape(b * s, d), *ffn1_stacked, vec(ln1_g), vec(ln1_b), bf(w_in), late_stacked, layer)

    na_o = _neighborhood_attention(nak.reshape(b, s, -1), naqt, navt, _na_bias_tiles(na_rpb), rows)

    zr = jnp.zeros((GLA_GATE_RANK, GLA_KEY_WIDTH), gla_w_dec2.dtype)
    wdec_pad = bf(jnp.stack([jnp.concatenate([gla_w_dec2[0], zr]), jnp.concatenate([zr, gla_w_dec2[1]])]))
    gla_o = _gla(gqk.reshape(b, s, -1), gvg.reshape(b, s, -1), gdl.reshape(b, s, -1),
                 wdec_pad, gla_b_dec, vec(gla_norm_g))

    out = _merge_ffn(x1, na_o.reshape(b * s, -1), gla_o.reshape(b * s, -1), w_merge_gates, wbn, wbg, wo,
                     vec(ln2_g), vec(ln2_b), w2_gate, w2_up, w2_down, vec(ln3_g), vec(ln3_b))
    return out.reshape(b, s, d)


def kernel(x, ffn1_w_gate, ffn1_w_up, ffn1_w_down, ln1_g, ln1_b, w_in, na_rpb, gla_w_dec2, gla_b_dec, gla_norm_g,
           w_branch_na, w_branch_gla, w_out, ln2_g, ln2_b, ffn2_w_gate, ffn2_w_up, ffn2_w_down, ln3_g, ln3_b):
    params = (ln1_g, ln1_b, w_in, na_rpb, gla_w_dec2, gla_b_dec, gla_norm_g, ln2_g, ln2_b, ln3_g, ln3_b)
    ffn1_stacked = (ffn1_w_gate, ffn1_w_up, ffn1_w_down)
    late_stacked = (ffn2_w_gate, ffn2_w_up, ffn2_w_down, w_out, w_branch_na, w_branch_gla)
    for l in range(DEPTH):
        x = _layer(x, l, ffn1_stacked, late_stacked, *(p[l] for p in params))
    return x
```
